```python
import jax
import jax.numpy as jnp
from jax import lax
import numpy as np

D_MODEL = 1024
BATCH = 4
SEQ = 8192
DEPTH = 1
DEC_BATCH = 32
DEC_SEQ = 1
PAST_LEN = 16384
PAGE_SIZE = 128

HEAD_DIM = 64
NSA_HEADS = 8
NSA_GROUPS = 2
HPG = NSA_HEADS // NSA_GROUPS
ROT_DIM = HEAD_DIM // 4
ROPE_THETA = 500000.0
CMP_BLK = 32
CMP_STRIDE = 16
CMP_HID = HEAD_DIM
SLC_BLK = 64
N_SEL = 16
WINDOW = 512
Q_BLK = 128
HG_HEADS = 4
HG_FDIM = 128
HG_IDIM = 128
HG_CHUNK = 64
D_FF = 2816
CONV_W = 3
EPS = 1e-6
NEG = -1e30

NSA_WIDTH = NSA_HEADS * HEAD_DIM
KV_WIDTH = NSA_GROUPS * HEAD_DIM
HG_FWIDTH = HG_HEADS * HG_FDIM
HG_IWIDTH = HG_HEADS * HG_IDIM
IN_SPLITS = (NSA_WIDTH, KV_WIDTH, KV_WIDTH, KV_WIDTH, KV_WIDTH, KV_WIDTH, KV_WIDTH, 3 * NSA_HEADS,
             HG_FWIDTH, HG_FWIDTH, HG_IWIDTH, HG_IWIDTH, D_MODEL, D_MODEL)
N_IN = sum(IN_SPLITS)

kernel_name = 'nsa_hgrn2_gated_hybrid_step'


def rms_norm(x, g):
    xf = x.astype(jnp.float32)
    xf = xf * lax.rsqrt(jnp.mean(xf * xf, axis=-1, keepdims=True) + EPS)
    return xf.astype(x.dtype) * g


def modulate(x, shift, scale):
    return x * (1.0 + scale[:, None, :]) + shift[:, None, :]


def split_in(z):
    cuts = [int(c) for c in np.cumsum(IN_SPLITS)[:-1]]
    return jnp.split(z, cuts, axis=-1)


def rope(x, pos):
    half = ROT_DIM // 2
    inv = ROPE_THETA ** (-jnp.arange(0, ROT_DIM, 2, dtype=jnp.float32) / ROT_DIM)
    ang = pos.astype(jnp.float32)[:, None] * inv[None, :]
    cos = jnp.cos(ang)[None, :, None, :].astype(x.dtype)
    sin = jnp.sin(ang)[None, :, None, :].astype(x.dtype)
    x1 = x[..., :half]
    x2 = x[..., half:ROT_DIM]
    return jnp.concatenate([x1 * cos - x2 * sin, x2 * cos + x1 * sin, x[..., ROT_DIM:]], axis=-1)


def masked_softmax(s, mask):
    s = jnp.where(mask, s, NEG)
    m = jnp.max(s, axis=-1, keepdims=True)
    e = jnp.where(mask, jnp.exp(s - m), 0.0)
    return e / jnp.maximum(jnp.sum(e, axis=-1, keepdims=True), 1e-30)


def compress_one(rows, w1, w2, pe):
    b, t, g, d = rows.shape
    n_half = CMP_BLK // CMP_STRIDE
    nsub = t // CMP_STRIDE
    nc = nsub - n_half + 1
    sub = rows[:, :nsub * CMP_STRIDE].reshape(b, nsub, CMP_STRIDE, g, d)
    w = w1.reshape(n_half, CMP_STRIDE, d, CMP_HID)
    proj = jnp.einsum('bnsgd,rsdh->rbngh', sub, w)
    hid = jnp.einsum('rsd,rsdh->h', pe.reshape(n_half, CMP_STRIDE, d), w)
    for r in range(n_half):
        hid = hid + proj[r, :, r:r + nc]
    return jax.nn.gelu(hid) @ w2


def compress(k_rows, v_rows, w1, w2, pe):
    kc = compress_one(k_rows, w1[0], w2[0], pe[0])
    vc = compress_one(v_rows, w1[1], w2[1], pe[1])
    cmp_end = jnp.arange(kc.shape[1]) * CMP_STRIDE + (CMP_BLK - 1)
    return kc, vc, cmp_end


def overlap_matrix(nc, ns):
    start = jnp.arange(nc) * CMP_STRIDE
    end = start + CMP_BLK - 1
    s0 = jnp.arange(ns) * SLC_BLK
    s1 = s0 + SLC_BLK - 1
    return ((start[:, None] <= s1[None, :]) & (end[:, None] >= s0[None, :])).astype(jnp.float32)


def nsa_block(q, qpos, gates, kc, vc, cmp_end, gather_sel, kw, vw, kw_pos, ns):
    b, tq = q.shape[:2]
    scale = HEAD_DIM ** -0.5
    qg = q.reshape(b, tq, NSA_GROUPS, HPG, HEAD_DIM)
    s = jnp.einsum('bqgjd,bngd->bgjqn', qg, kc).astype(jnp.float32) * scale
    p = masked_softmax(s, cmp_end[None, :] <= qpos[:, None])
    o_cmp = jnp.einsum('bgjqn,bngd->bqgjd', p.astype(vc.dtype), vc)
    imp = jnp.einsum('bgjqn,nm->bgqm', p, overlap_matrix(kc.shape[1], ns))
    blk = jnp.arange(ns)[None, :]
    cur = (qpos // SLC_BLK)[:, None]
    valid = blk * SLC_BLK <= qpos[:, None]
    forced = (blk == 0) | (blk == cur) | (blk == cur - 1)
    score = jnp.where(forced, 1e9, jnp.where(valid, imp, -1e9))
    _, idx = lax.top_k(score, min(N_SEL, ns))
    ks, vs, kpos = gather_sel(idx)
    s = jnp.einsum('bqgjd,bgqkd->bgjqk', qg, ks).astype(jnp.float32) * scale
    p = masked_softmax(s, (kpos <= qpos[None, None, :, None])[:, :, None])
    o_slc = jnp.einsum('bgjqk,bgqkd->bqgjd', p.astype(vs.dtype), vs)
    s = jnp.einsum('bqgjd,bkgd->bgjqk', qg, kw).astype(jnp.float32) * scale
    dist = qpos[:, None] - kw_pos[None, :]
    p = masked_softmax(s, (dist >= 0) & (dist < WINDOW) & (kw_pos[None, :] >= 0))
    o_win = jnp.einsum('bgjqk,bkgd->bqgjd', p.astype(vw.dtype), vw)
    g = jax.nn.sigmoid(gates.astype(jnp.float32)).reshape(b, tq, NSA_GROUPS, HPG, 3)
    o = g[..., 0:1] * o_cmp + g[..., 1:2] * o_slc + g[..., 2:3] * o_win
    return o.astype(q.dtype).reshape(b, tq, NSA_WIDTH)


def nsa_project(parts, pos, q_gain, k_gain):
    q_raw, kc, vc, ks, vs, kw, vw, gates = parts
    b, t = q_raw.shape[:2]
    q = rope(rms_norm(q_raw.reshape(b, t, NSA_HEADS, HEAD_DIM), q_gain), pos)

    def key(k, j):
        return rope(rms_norm(k.reshape(b, t, NSA_GROUPS, HEAD_DIM), k_gain[j]), pos)

    def val(v):
        return v.reshape(b, t, NSA_GROUPS, HEAD_DIM)

    return q, gates, (key(kc, 0), val(vc), key(ks, 1), val(vs), key(kw, 2), val(vw))


def nsa_prompt(q, gates, rows, cmp_w1, cmp_w2, cmp_pe):
    kc_r, vc_r, ks_r, vs_r, kw_r, vw_r = rows
    b, t = q.shape[:2]
    kc, vc, cmp_end = compress(kc_r, vc_r, cmp_w1, cmp_w2, cmp_pe)
    ns = t // SLC_BLK
    kb = ks_r.reshape(b, ns, SLC_BLK, NSA_GROUPS, HEAD_DIM).transpose(0, 3, 1, 2, 4)
    vb = vs_r.reshape(b, ns, SLC_BLK, NSA_GROUPS, HEAD_DIM).transpose(0, 3, 1, 2, 4)
    pad = ((0, 0), (WINDOW, 0), (0, 0), (0, 0))
    kpad = jnp.pad(kw_r, pad)
    vpad = jnp.pad(vw_r, pad)
    bi = jnp.arange(b)[:, None, None, None]
    gi = jnp.arange(NSA_GROUPS)[None, :, None, None]

    def gather_sel(idx):
        tq, n = idx.shape[2], idx.shape[3]
        kpos = (idx[..., None] * SLC_BLK + jnp.arange(SLC_BLK)).reshape(b, NSA_GROUPS, tq, n * SLC_BLK)
        ks = kb[bi, gi, idx].reshape(b, NSA_GROUPS, tq, n * SLC_BLK, HEAD_DIM)
        vs = vb[bi, gi, idx].reshape(b, NSA_GROUPS, tq, n * SLC_BLK, HEAD_DIM)
        return ks, vs, kpos

    def block(i):
        q0 = i * Q_BLK
        qpos = q0 + jnp.arange(Q_BLK)
        kw = lax.dynamic_slice_in_dim(kpad, q0, WINDOW + Q_BLK, axis=1)
        vw = lax.dynamic_slice_in_dim(vpad, q0, WINDOW + Q_BLK, axis=1)
        kw_pos = q0 - WINDOW + jnp.arange(WINDOW + Q_BLK)
        qb = lax.dynamic_slice_in_dim(q, q0, Q_BLK, axis=1)
        gb = lax.dynamic_slice_in_dim(gates, q0, Q_BLK, axis=1)
        return nsa_block(qb, qpos, gb, kc, vc, cmp_end, gather_sel, kw, vw, kw_pos, ns)

    out = lax.map(block, jnp.arange(t // Q_BLK))
    return out.transpose(1, 0, 2, 3).reshape(b, t, NSA_WIDTH)


def nsa_sample(q, gates, rows, pos, cmp_w1, cmp_w2, cmp_pe, pk_c, pv_c, pk_s, pv_s, win_k, win_v, page_table):
    kc_r, vc_r, ks_r, vs_r, kw_r, vw_r = rows
    db, t = q.shape[:2]
    n_pages = page_table.shape[1]
    past = n_pages * PAGE_SIZE

    def past_rows(pool):
        return pool[page_table].reshape(db, past, NSA_GROUPS, HEAD_DIM)

    kc, vc, cmp_end = compress(jnp.concatenate([past_rows(pk_c), kc_r], axis=1),
                               jnp.concatenate([past_rows(pv_c), vc_r], axis=1), cmp_w1, cmp_w2, cmp_pe)
    ns = -(-(past + t) // SLC_BLK)
    bi = jnp.arange(db)[:, None, None, None, None]
    gi = jnp.arange(NSA_GROUPS)[None, :, None, None, None]

    def gather_sel(idx):
        tq, n = idx.shape[2], idx.shape[3]
        kpos = idx[..., None] * SLC_BLK + jnp.arange(SLC_BLK)
        phys = page_table[bi, jnp.clip(kpos // PAGE_SIZE, 0, n_pages - 1)]
        off = kpos % PAGE_SIZE
        new_i = jnp.clip(kpos - past, 0, t - 1)
        is_past = (kpos < past)[..., None]

        def sel(pool, new):
            return jnp.where(is_past, pool[phys, off, gi], new[bi, new_i, gi]).reshape(
                db, NSA_GROUPS, tq, n * SLC_BLK, HEAD_DIM)

        return sel(pk_s, ks_r), sel(pv_s, vs_r), kpos.reshape(db, NSA_GROUPS, tq, n * SLC_BLK)

    win_buf = win_k.shape[1]
    kw = jnp.concatenate([win_k, kw_r], axis=1)
    vw = jnp.concatenate([win_v, vw_r], axis=1)
    kw_pos = past - win_buf + jnp.arange(win_buf + t)
    out = nsa_block(q, pos, gates, kc, vc, cmp_end, gather_sel, kw, vw, kw_pos, ns)
    return out, kw[:, -win_buf:], vw[:, -win_buf:]


def chunk_recurrence(q, k, v, g, s0):
    b, t, hh, _ = q.shape
    idim = v.shape[-1]
    L = min(HG_CHUNK, t)
    pad = (-t) % L
    if pad:
        q, k, v, g = (jnp.pad(a, ((0, 0), (0, pad), (0, 0), (0, 0))) for a in (q, k, v, g))
    nc = (t + pad) // L

    def chunks(a):
        return a.reshape(b, nc, L, hh, a.shape[-1]).transpose(1, 0, 3, 2, 4)

    causal = (jnp.arange(L)[:, None] >= jnp.arange(L)[None, :])[:, :, None]

    def step(s, inp):
        qc, kc, vc, gc = inp
        bcum = jnp.cumsum(gc, axis=2)
        o = jnp.einsum('bhlf,bhfi->bhli', qc * jnp.exp(bcum), s)
        decay = jnp.exp(jnp.where(causal, bcum[:, :, :, None, :] - bcum[:, :, None, :, :], NEG))
        a = jnp.einsum('bhtf,bhsf,bhtsf->bhts', qc, kc, decay)
        o = o + jnp.einsum('bhts,bhsi->bhti', a, vc)
        blast = bcum[:, :, -1:, :]
        s = jnp.exp(blast[:, :, 0, :])[..., None] * s + jnp.einsum('bhsf,bhsi->bhfi', kc * jnp.exp(blast - bcum), vc)
        return s, o

    s_fin, o = lax.scan(step, s0, (chunks(q), chunks(k), chunks(v), chunks(g)))
    o = o.transpose(1, 0, 3, 2, 4).reshape(b, nc * L, hh, idim)[:, :t]
    return o, s_fin


def hgrn2(hq, hf, hi, hg, lb, norm_g, state0):
    b, t = hq.shape[:2]
    shp = (b, t, HG_HEADS, HG_FDIM)
    lbh = lb.reshape(HG_HEADS, HG_FDIM)
    f = lbh + (1.0 - lbh) * jax.nn.sigmoid(hf.reshape(shp).astype(jnp.float32))
    o, s_new = chunk_recurrence(hq.reshape(shp).astype(jnp.float32), 1.0 - f,
                                hi.reshape(b, t, HG_HEADS, HG_IDIM).astype(jnp.float32), jnp.log(f),
                                state0.astype(jnp.float32))
    o = rms_norm(o.astype(hq.dtype), norm_g) * jax.nn.silu(hg.reshape(b, t, HG_HEADS, HG_IDIM))
    return o.reshape(b, t, HG_IWIDTH), s_new.astype(state0.dtype)


def conv_ffn(h, w_up, conv_w, conv_b, w_down, buf):
    t = h.shape[1]
    u, v = jnp.split(h @ w_up, 2, axis=-1)
    u_ext = jnp.concatenate([buf.astype(u.dtype), u], axis=1)
    y = conv_b
    for j in range(CONV_W):
        y = y + conv_w[j] * u_ext[:, j:j + t]
    return (jax.nn.silu(y) * v) @ w_down, u_ext[:, -(CONV_W - 1):]


def layer_step(x, c, pos, lw, lb, hg_state0, conv_buf0, nsa_cache=None, page_table=None):
    sh1, sc1, g1, sh2, sc2, g2 = jnp.split(c @ lw['ada_w'] + lw['ada_b'], 6, axis=-1)
    h = modulate(rms_norm(x, lw['norm_attn']), sh1, sc1)
    parts = split_in(h @ lw['w_in'])
    q, gates, rows = nsa_project(parts[:8], pos, lw['q_gain'], lw['k_gain'])
    if nsa_cache is None:
        o_nsa = nsa_prompt(q, gates, rows, lw['cmp_w1'], lw['cmp_w2'], lw['cmp_pe'])
        w = min(WINDOW, x.shape[1])
        nsa_new = rows[:4] + (rows[4][:, -w:], rows[5][:, -w:])
    else:
        o_nsa, wk, wv = nsa_sample(q, gates, rows, pos, lw['cmp_w1'], lw['cmp_w2'], lw['cmp_pe'],
                                   *nsa_cache, page_table)
        nsa_new = rows[:4] + (wk, wv)
    o_hg, s_hg = hgrn2(parts[8], parts[9], parts[10], parts[11], lb, lw['hg_norm'], hg_state0)
    mix = jax.nn.sigmoid(parts[12]) * (o_nsa @ lw['w_br_nsa']) + jax.nn.sigmoid(parts[13]) * (o_hg @ lw['w_br_hg'])
    x = x + g1[:, None, :] * (mix @ lw['w_out'])
    h2 = modulate(rms_norm(x, lw['norm_ffn']), sh2, sc2)
    f_out, conv_new = conv_ffn(h2, lw['w_up'], lw['conv_w'], lw['conv_b'], lw['w_down'], conv_buf0)
    x = x + g2[:, None, :] * f_out
    return x, nsa_new + (s_hg, conv_new)


def setup_inputs(seed: int = 0) -> dict:
    key = jax.random.key(seed)
    ks = jax.random.split(key, 32)

    def nrm(k, shape, s=1.0):
        return s * jax.random.normal(k, shape, jnp.float32)

    n_pages = PAST_LEN // PAGE_SIZE
    used = DEC_BATCH * n_pages
    n_phys = used + max(1, used // 4)
    win_buf = min(WINDOW, PAST_LEN)
    page_table = jax.random.permutation(ks[0], n_phys)[:used].reshape(DEC_BATCH, n_pages).astype(jnp.int32)
    pool = (DEPTH, n_phys, PAGE_SIZE, NSA_GROUPS, HEAD_DIM)
    win = (DEPTH, DEC_BATCH, win_buf, NSA_GROUPS, HEAD_DIM)
    return {
        'x_prompt': nrm(ks[1], (BATCH, SEQ, D_MODEL)),
        'x_sample': nrm(ks[2], (DEC_BATCH, DEC_SEQ, D_MODEL)),
        'cache_cmp_k': nrm(ks[3], pool),
        'cache_cmp_v': nrm(ks[4], pool),
        'cache_slc_k': nrm(ks[5], pool),
        'cache_slc_v': nrm(ks[6], pool),
        'cache_win_k': nrm(ks[7], win),
        'cache_win_v': nrm(ks[8], win),
        'state_hgrn': nrm(ks[9], (DEPTH, DEC_BATCH, HG_HEADS, HG_FDIM, HG_IDIM), 0.5),
        'state_ffn_conv': nrm(ks[10], (DEPTH, DEC_BATCH, CONV_W - 1, D_FF)),
        'page_table': page_table,
        'c_prompt': nrm(ks[11], (BATCH, D_MODEL)),
        'c_sample': nrm(ks[12], (DEC_BATCH, D_MODEL)),
        'ada_w': nrm(ks[13], (DEPTH, D_MODEL, 6 * D_MODEL), 0.5 * D_MODEL ** -0.5),
        'ada_b': nrm(ks[14], (DEPTH, 6 * D_MODEL), 0.01),
        'norm_attn': 1.0 + nrm(ks[15], (DEPTH, D_MODEL), 0.1),
        'norm_ffn': 1.0 + nrm(ks[16], (DEPTH, D_MODEL), 0.1),
        'w_in': nrm(ks[17], (DEPTH, D_MODEL, N_IN), D_MODEL ** -0.5),
        'q_gain': 1.0 + nrm(ks[18], (DEPTH, HEAD_DIM), 0.1),
        'k_gain': 1.0 + nrm(ks[19], (DEPTH, 3, HEAD_DIM), 0.1),
        'cmp_w1': nrm(ks[20], (DEPTH, 2, CMP_BLK * HEAD_DIM, CMP_HID), (CMP_BLK * HEAD_DIM) ** -0.5),
        'cmp_w2': nrm(ks[21], (DEPTH, 2, CMP_HID, HEAD_DIM), CMP_HID ** -0.5),
        'cmp_pe': nrm(ks[22], (DEPTH, 2, CMP_BLK, HEAD_DIM), 0.1),
        'hg_lb': nrm(ks[23], (DEPTH + 1, HG_FWIDTH)),
        'hg_norm': 1.0 + nrm(ks[24], (DEPTH, HG_IDIM), 0.1),
        'w_br_nsa': nrm(ks[25], (DEPTH, NSA_WIDTH, D_MODEL), NSA_WIDTH ** -0.5),
        'w_br_hg': nrm(ks[26], (DEPTH, HG_IWIDTH, D_MODEL), HG_IWIDTH ** -0.5),
        'w_out': nrm(ks[27], (DEPTH, D_MODEL, D_MODEL), D_MODEL ** -0.5),
        'w_up': nrm(ks[28], (DEPTH, D_MODEL, 2 * D_FF), D_MODEL ** -0.5),
        'conv_w': nrm(ks[29], (DEPTH, CONV_W, D_FF), CONV_W ** -0.5),
        'conv_b': nrm(ks[30], (DEPTH, D_FF), 0.01),
        'w_down': nrm(ks[31], (DEPTH, D_FF, D_MODEL), D_FF ** -0.5),
    }


def reference(x_prompt, x_sample, cache_cmp_k, cache_cmp_v, cache_slc_k, cache_slc_v, cache_win_k, cache_win_v,
              state_hgrn, state_ffn_conv, page_table, c_prompt, c_sample, ada_w, ada_b, norm_attn, norm_ffn, w_in,
              q_gain, k_gain, cmp_w1, cmp_w2, cmp_pe, hg_lb, hg_norm, w_br_nsa, w_br_hg, w_out, w_up, conv_w,
              conv_b, w_down):
    lb_all = jnp.cumsum(jax.nn.softmax(hg_lb.astype(jnp.float32), axis=0), axis=0)
    b, t = x_prompt.shape[:2]
    db, tn = x_sample.shape[:2]
    pos_p = jnp.arange(t)
    pos_s = PAST_LEN + jnp.arange(tn)
    xp, xs = x_prompt, x_sample
    p_states, s_states = [], []
    for l in range(DEPTH):
        lw = dict(ada_w=ada_w[l], ada_b=ada_b[l], norm_attn=norm_attn[l], norm_ffn=norm_ffn[l], w_in=w_in[l],
                  q_gain=q_gain[l], k_gain=k_gain[l], cmp_w1=cmp_w1[l], cmp_w2=cmp_w2[l], cmp_pe=cmp_pe[l],
                  hg_norm=hg_norm[l], w_br_nsa=w_br_nsa[l], w_br_hg=w_br_hg[l], w_out=w_out[l], w_up=w_up[l],
                  conv_w=conv_w[l], conv_b=conv_b[l], w_down=w_down[l])
        hg0 = jnp.zeros((b, HG_HEADS, HG_FDIM, HG_IDIM), xp.dtype)
        cb0 = jnp.zeros((b, CONV_W - 1, D_FF), xp.dtype)
        xp, st_p = layer_step(xp, c_prompt, pos_p, lw, lb_all[l], hg0, cb0)
        nsa_cache = (cache_cmp_k[l], cache_cmp_v[l], cache_slc_k[l], cache_slc_v[l], cache_win_k[l], cache_win_v[l])
        xs, st_s = layer_step(xs, c_sample, pos_s, lw, lb_all[l], state_hgrn[l], state_ffn_conv[l],
                              nsa_cache, page_table)
        p_states.append(st_p)
        s_states.append(st_s)
    p_ck, p_cv, p_sk, p_sv, p_wk, p_wv, p_hg, p_cf = [jnp.stack(a) for a in zip(*p_states)]
    s_ck, s_cv, s_sk, s_sv, s_wk, s_wv, s_hg, s_cf = [jnp.stack(a) for a in zip(*s_states)]
    return (xp, xs, p_ck, p_cv, p_sk, p_sv, p_wk, p_wv, p_hg, p_cf,
            s_ck, s_cv, s_sk, s_sv, s_wk, s_wv, s_hg, s_cf)
```

```python
import functools

import numpy as np
import jax
import jax.numpy as jnp
from jax import lax
from jax.experimental import pallas as pl
from jax.experimental.pallas import tpu as pltpu

F32 = jnp.float32
BF16 = jnp.bfloat16

HEAD_DIM = 64
NSA_HEADS = 8
NSA_GROUPS = 2
HPG = NSA_HEADS // NSA_GROUPS
ROT_DIM = HEAD_DIM // 4
ROPE_THETA = 500000.0
CMP_BLK = 32
CMP_STRIDE = 16
SLC_BLK = 64
N_SEL = 16
WINDOW = 512
PAGE_SIZE = 128
HG_HEADS = 4
HG_DIM = 128
HG_SUB = 16
CONV_W = 3
EPS = 1e-6
NEG = -1e30
SEL_BIAS = -1e9
LANES = 128
VMEM_LIMIT = 56 * 1024 * 1024

NSA_WIDTH = NSA_HEADS * HEAD_DIM
KV_WIDTH = NSA_GROUPS * HEAD_DIM
HG_WIDTH = HG_HEADS * HG_DIM
GRP_Q = HPG * HEAD_DIM

SEG_Q = 0
SEG_KV = SEG_Q + NSA_WIDTH
SEG_G = SEG_KV + 6 * KV_WIDTH
SEG_H = SEG_G + NSA_GROUPS * LANES
SEG_M = SEG_H + 4 * HG_WIDTH


def _cparams(sem):
    return pltpu.CompilerParams(dimension_semantics=sem, vmem_limit_bytes=VMEM_LIMIT)


def _dot(a, b):
    return jnp.dot(a.astype(BF16), b.astype(BF16), preferred_element_type=F32)


def _dot_nt(a, b):
    return lax.dot_general(a.astype(BF16), b.astype(BF16), (((1,), (1,)), ((), ())), preferred_element_type=F32)


def _split2(x):
    hi = x.astype(BF16)
    lo = (x - hi.astype(F32)).astype(BF16)
    return hi, lo


def _split3(x):
    hi = x.astype(BF16)
    r = x - hi.astype(F32)
    mid = r.astype(BF16)
    lo = (r - mid.astype(F32)).astype(BF16)
    return hi, mid, lo


def _dot_x3(a, b):
    ah, al = _split2(a)
    bh, bl = _split2(b)
    return (jnp.dot(ah, bh, preferred_element_type=F32) + jnp.dot(al, bh, preferred_element_type=F32)
            + jnp.dot(ah, bl, preferred_element_type=F32))


def _sigmoid(x):
    return 1.0 / (1.0 + jnp.exp(-x))


def _const_spec(shape):
    nd = len(shape)
    return pl.BlockSpec(shape, lambda *_: (0,) * nd)


def _resident_spec(shape):
    nd = len(shape)
    return pl.BlockSpec(shape, lambda *_: (0,) * nd, pipeline_mode=pl.Buffered(1))


def _ada_kernel(c_ref, w_ref, b_ref, o_ref):
    o_ref[...] = _dot_x3(c_ref[...], w_ref[...]) + b_ref[...]


def _ada(c_all, w, b):
    r, d = c_all.shape
    n = w.shape[1]
    tn = 1536
    return pl.pallas_call(
        _ada_kernel,
        grid=(n // tn,),
        in_specs=[_const_spec((r, d)), pl.BlockSpec((d, tn), lambda j: (0, j)), pl.BlockSpec((1, tn), lambda j: (0, j))],
        out_specs=pl.BlockSpec((r, tn), lambda j: (0, j)),
        out_shape=jax.ShapeDtypeStruct((r, n), F32),
        compiler_params=_cparams(("arbitrary",)),
        name="ada",
    )(c_all, w, b.reshape(1, n))


def _rope(x, c, s_lo, s_hi):
    width = x.shape[1]
    reps = width // LANES
    if reps > 1:
        c = jnp.concatenate([c] * reps, axis=1)
        s_lo = jnp.concatenate([s_lo] * reps, axis=1)
        s_hi = jnp.concatenate([s_hi] * reps, axis=1)
    half = ROT_DIM // 2
    return x * c + pltpu.roll(x, half, 1) * s_hi + pltpu.roll(x, width - half, 1) * s_lo


def _head_rms(z, bd):
    hi, lo = _split2(z * z)
    ss = jnp.dot(hi, bd, preferred_element_type=F32) + jnp.dot(lo, bd, preferred_element_type=F32)
    return z * lax.rsqrt(ss * (1.0 / HEAD_DIM) + EPS)


def _inproj_kernel(x_ref, sh_ref, sc_ref, ng_ref, w_ref, qg_ref, kg_ref, rc_ref, rlo_ref, rhi_ref, bd_ref,
                   q_ref, kc_ref, vc_ref, ks_ref, vs_ref, kw_ref, vw_ref, kvb_ref, gt_ref, hz_ref, sm_ref):
    x = x_ref[...]
    xn = x * lax.rsqrt(jnp.mean(x * x, axis=-1, keepdims=True) + EPS) * ng_ref[...]
    h = (xn * (1.0 + sc_ref[...]) + sh_ref[...]).astype(BF16)
    rc, rlo, rhi = rc_ref[...], rlo_ref[...], rhi_ref[...]
    bd = bd_ref[...]

    zq = jnp.dot(h, w_ref[:, SEG_Q:SEG_KV], preferred_element_type=F32)
    q = _rope(_head_rms(zq, bd) * qg_ref[...], rc, rlo, rhi)
    q_ref[...] = (q * (HEAD_DIM ** -0.5)).astype(BF16)

    zkv = jnp.dot(h, w_ref[:, SEG_KV:SEG_G], preferred_element_type=F32)
    bd1 = bd[:LANES, :LANES]
    k_refs = (kc_ref, ks_ref, kw_ref)
    v_refs = (vc_ref, vs_ref, vw_ref)
    rows = []
    for j in range(3):
        zk = zkv[:, 2 * j * KV_WIDTH:(2 * j + 1) * KV_WIDTH]
        k = _rope(_head_rms(zk, bd1) * kg_ref[j:j + 1, :], rc, rlo, rhi)
        v = zkv[:, (2 * j + 1) * KV_WIDTH:(2 * j + 2) * KV_WIDTH]
        k_refs[j][...] = k
        v_refs[j][...] = v
        rows.append((k, v))
    parts = []
    for g in range(NSA_GROUPS):
        for k, v in rows[1:]:
            parts.append(k[:, g * HEAD_DIM:(g + 1) * HEAD_DIM])
            parts.append(v[:, g * HEAD_DIM:(g + 1) * HEAD_DIM])
    kvb_ref[...] = jnp.concatenate(parts, axis=1).astype(BF16)

    gt_ref[...] = _sigmoid(jnp.dot(h, w_ref[:, SEG_G:SEG_H], preferred_element_type=F32))
    hz_ref[...] = jnp.dot(h, w_ref[:, SEG_H:SEG_M], preferred_element_type=F32)
    sm_ref[...] = _sigmoid(jnp.dot(h, w_ref[:, SEG_M:], preferred_element_type=F32))


def _inproj(x2d, shift, scale, norm_g, w_pack, qg, kg, rope_tabs, bd, *, tm, tiles_per_mod, rope_tiles):
    r, d = x2d.shape
    n_tiles = r // tm
    rows_mod = shift.shape[1]
    mod_spec = pl.BlockSpec((None, rows_mod, d), lambda i: (i // tiles_per_mod, 0, 0))
    rope_spec = pl.BlockSpec((tm, LANES), lambda i: (i % rope_tiles, 0))

    def row_spec(width):
        return pl.BlockSpec((tm, width), lambda i: (i, 0))

    widths = (NSA_WIDTH,) + (KV_WIDTH,) * 6 + (4 * KV_WIDTH, NSA_GROUPS * LANES, 4 * HG_WIDTH, w_pack.shape[1] - SEG_M)
    dtypes = (BF16,) + (F32,) * 6 + (BF16, F32, F32, F32)
    return pl.pallas_call(
        _inproj_kernel,
        grid=(n_tiles,),
        in_specs=[row_spec(d), mod_spec, mod_spec, _const_spec((1, d)), _resident_spec(w_pack.shape),
                  _const_spec(qg.shape), _const_spec(kg.shape), rope_spec, rope_spec, rope_spec, _const_spec(bd.shape)],
        out_specs=[row_spec(w) for w in widths],
        out_shape=[jax.ShapeDtypeStruct((r, w), dt) for w, dt in zip(widths, dtypes)],
        compiler_params=_cparams(("arbitrary",)),
        name="inproj",
    )(x2d, shift, scale, norm_g, w_pack, qg, kg, *rope_tabs, bd)


def _mm_kernel(a_ref, w_ref, o_ref):
    o_ref[...] = _dot(a_ref[...], w_ref[...])


def _mm(a, w, tm, name):
    m, k = a.shape
    n = w.shape[1]
    tm = min(tm, m)
    return pl.pallas_call(
        _mm_kernel,
        grid=(m // tm,),
        in_specs=[pl.BlockSpec((tm, k), lambda i: (i, 0)), _const_spec((k, n))],
        out_specs=pl.BlockSpec((tm, n), lambda i: (i, 0)),
        out_shape=jax.ShapeDtypeStruct((m, n), F32),
        compiler_params=_cparams(("arbitrary",)),
        name=name,
    )(a, w)


PAGES_PER_STEP = 16


def _cmp_proj_paged_kernel(pt_ref, *refs):
    page_refs = refs[:PAGES_PER_STEP]
    w_ref, o_ref, rows_ref = refs[PAGES_PER_STEP:]
    sub = page_refs[0].shape[0]
    for j, pr in enumerate(page_refs):
        rows_ref[j * sub:(j + 1) * sub, :] = pr[...].astype(BF16)
    o_ref[...] = jnp.dot(rows_ref[...], w_ref[...], preferred_element_type=F32)


def _cmp_proj_paged(pool3, page_table, w, name):
    db, n_pages = page_table.shape
    _, sub, kdim = pool3.shape
    n = w.shape[1]
    steps = n_pages // PAGES_PER_STEP
    page_specs = [pl.BlockSpec((None, sub, kdim), lambda b, c, pt, j=j: (pt[b, c * PAGES_PER_STEP + j], 0, 0))
                  for j in range(PAGES_PER_STEP)]
    grid_spec = pltpu.PrefetchScalarGridSpec(
        num_scalar_prefetch=1,
        grid=(db, steps),
        in_specs=page_specs + [pl.BlockSpec((kdim, n), lambda b, c, pt: (0, 0))],
        out_specs=pl.BlockSpec((None, PAGES_PER_STEP * sub, n), lambda b, c, pt: (b, c, 0)),
        scratch_shapes=[pltpu.VMEM((PAGES_PER_STEP * sub, kdim), BF16)],
    )
    return pl.pallas_call(
        _cmp_proj_paged_kernel,
        grid_spec=grid_spec,
        out_shape=jax.ShapeDtypeStruct((db, n_pages * sub, n), F32),
        compiler_params=_cparams(("arbitrary", "arbitrary")),
        name=name,
    )(page_table, *([pool3] * PAGES_PER_STEP), w)


def _cmp_combine_kernel(pk_ref, pv_ref, pe_ref, w1_ref, w2_ref, o_ref):
    n = pk_ref.shape[0]
    outs = []
    for j, p_ref in enumerate((pk_ref, pv_ref)):
        pe = jnp.dot(pe_ref[j], w1_ref[j], preferred_element_type=F32)
        pe_hid = pe[0:1, :KV_WIDTH] + pe[1:2, KV_WIDTH:]
        p = p_ref[...]
        nxt = pltpu.roll(p[:, KV_WIDTH:], n - 1, 0)
        hid = pe_hid + p[:, :KV_WIDTH] + nxt
        outs.append(_dot(jax.nn.gelu(hid), w2_ref[j]))
    for g in range(NSA_GROUPS):
        o_ref[g] = jnp.concatenate([o[:, g * HEAD_DIM:(g + 1) * HEAD_DIM] for o in outs], axis=1).astype(BF16)


def _cmp_combine(pk, pv, pe_x, w1, w2bd):
    bx, n, c = pk.shape
    return pl.pallas_call(
        _cmp_combine_kernel,
        grid=(bx,),
        in_specs=[pl.BlockSpec((None, n, c), lambda b: (b, 0, 0)), pl.BlockSpec((None, n, c), lambda b: (b, 0, 0)),
                  _const_spec(pe_x.shape), _const_spec(w1.shape), _const_spec(w2bd.shape)],
        out_specs=pl.BlockSpec((None, NSA_GROUPS, n, 2 * HEAD_DIM), lambda b: (b, 0, 0, 0)),
        out_shape=jax.ShapeDtypeStruct((bx, NSA_GROUPS, n, 2 * HEAD_DIM), BF16),
        compiler_params=_cparams(("arbitrary",)),
        name="cmp_combine",
    )(pk, pv, pe_x, w1, w2bd)


def _masked_softmax(s, mask):
    s = jnp.where(mask, s, NEG)
    m = jnp.max(s, axis=-1, keepdims=True)
    e = jnp.where(mask, jnp.exp(s - m), 0.0)
    return e / jnp.maximum(jnp.sum(e, axis=-1, keepdims=True), 1e-30)


def _top_k_mask(score, k, idx_out=False):
    r, l = score.shape
    lane = lax.broadcasted_iota(jnp.int32, (r, l), 1)
    sel = jnp.zeros((r, l), jnp.bool_)
    idx = jnp.zeros((r, LANES), jnp.int32)
    lane_out = lax.broadcasted_iota(jnp.int32, (r, LANES), 1)
    work = score
    for it in range(k):
        mx = jnp.max(work, axis=-1, keepdims=True)
        first = jnp.min(jnp.where(work == mx, lane, l), axis=-1, keepdims=True)
        pick = lane == first
        sel = jnp.logical_or(sel, pick)
        work = jnp.where(pick, -jnp.inf, work)
        if idx_out:
            idx = jnp.where(lane_out == it, first, idx)
    return (sel, idx) if idx_out else sel


def _block_scores(imp, qpos):
    blk = lax.broadcasted_iota(jnp.int32, imp.shape, 1)
    cur = qpos >> 6
    valid = (blk << 6) <= qpos
    forced = (blk == 0) | (blk == cur) | (blk == cur - 1)
    return jnp.where(forced, 1e9, jnp.where(valid, imp, -1e9))


def _nsa_prompt_kernel(q_ref, kcv_ref, kvb_ref, gt_ref, ov_ref, o_ref, *, tq):
    qi = pl.program_id(2)
    q0 = qi * tq
    tk = tq
    nc = kcv_ref.shape[0]
    rows = HPG * tq

    q = q_ref[...]
    qs = jnp.concatenate([q[:, h * HEAD_DIM:(h + 1) * HEAD_DIM] for h in range(HPG)], axis=0)
    qpos = q0 + lax.broadcasted_iota(jnp.int32, (tq, 1), 0)
    qpos4 = jnp.concatenate([qpos] * HPG, axis=0)

    kcv = kcv_ref[...]
    kc, vc = kcv[:, :HEAD_DIM], kcv[:, HEAD_DIM:]
    cmp_end = lax.broadcasted_iota(jnp.int32, (1, nc), 1) * CMP_STRIDE + (CMP_BLK - 1)
    p = _masked_softmax(_dot_nt(qs, kc), cmp_end <= qpos4)
    o_cmp = _dot(p, vc)
    psum = p[0:tq]
    for h in range(1, HPG):
        psum = psum + p[h * tq:(h + 1) * tq]
    p_hi, p_lo = _split2(psum)
    ov = ov_ref[...]
    imp = jnp.dot(p_hi, ov, preferred_element_type=F32) + jnp.dot(p_lo, ov, preferred_element_type=F32)

    sel = _top_k_mask(_block_scores(imp, qpos), N_SEL)
    bias = jnp.where(sel, 0.0, SEL_BIAS).astype(BF16)
    q_aug = jnp.concatenate([jnp.concatenate([bias] * HPG, axis=0), qs], axis=1)

    lane_blk = lax.broadcasted_iota(jnp.int32, (1, LANES), 1)
    kcol = lax.broadcasted_iota(jnp.int32, (tk, 1), 0)
    krow = lax.broadcasted_iota(jnp.int32, (1, tk), 1)

    def flash(lo, hi, step_scores, v_lane0):
        def body(j, carry):
            m, l, acc = carry
            k0 = j * tk
            blk = kvb_ref[pl.ds(pl.multiple_of(k0, tk), tk), :]
            s = step_scores(blk, k0)
            m_new = jnp.maximum(m, jnp.max(s, axis=-1, keepdims=True))
            alpha = jnp.exp(m - m_new)
            e = jnp.exp(s - m_new)
            l = alpha * l + jnp.sum(e, axis=-1, keepdims=True)
            acc = alpha * acc + _dot(e, blk[:, v_lane0:v_lane0 + HEAD_DIM])
            return m_new, l, acc

        init = (jnp.full((rows, 1), NEG, F32), jnp.zeros((rows, 1), F32), jnp.zeros((rows, HEAD_DIM), F32))
        _, l, acc = lax.fori_loop(lo, hi, body, init)
        return acc / jnp.maximum(l, 1e-30)

    def slc_scores(blk, k0):
        onehot = jnp.where(((k0 + kcol) >> 6) == lane_blk, 1.0, 0.0).astype(BF16)
        k_aug = jnp.concatenate([onehot, blk[:, 0:HEAD_DIM]], axis=1)
        s = _dot_nt(q_aug, k_aug)
        return jnp.where((k0 + krow) <= qpos4, s, NEG)

    def win_scores(blk, k0):
        s = _dot_nt(qs, blk[:, 2 * HEAD_DIM:3 * HEAD_DIM])
        dist = qpos4 - (k0 + krow)
        return jnp.where((dist >= 0) & (dist < WINDOW), s, NEG)

    o_slc = flash(0, qi + 1, slc_scores, HEAD_DIM)
    o_win = flash(jnp.maximum(qi - WINDOW // tk, 0), qi + 1, win_scores, 3 * HEAD_DIM)

    gt = gt_ref[...]
    outs = []
    for h in range(HPG):
        r0 = h * tq
        outs.append(gt[:, 3 * h:3 * h + 1] * o_cmp[r0:r0 + tq] + gt[:, 3 * h + 1:3 * h + 2] * o_slc[r0:r0 + tq]
                    + gt[:, 3 * h + 2:3 * h + 3] * o_win[r0:r0 + tq])
    o_ref[...] = jnp.concatenate(outs, axis=1).astype(BF16)


def _nsa_prompt(q, kcv, kvb, gt, ov, *, tq):
    b, t, _ = q.shape
    nc = kcv.shape[2]
    return pl.pallas_call(
        functools.partial(_nsa_prompt_kernel, tq=tq),
        grid=(b, NSA_GROUPS, t // tq),
        in_specs=[pl.BlockSpec((None, tq, GRP_Q), lambda bi, g, i: (bi, i, g)),
                  pl.BlockSpec((None, None, nc, 2 * HEAD_DIM), lambda bi, g, i: (bi, g, 0, 0)),
                  pl.BlockSpec((None, t, 4 * HEAD_DIM), lambda bi, g, i: (bi, 0, g)),
                  pl.BlockSpec((None, tq, LANES), lambda bi, g, i: (bi, i, g)),
                  _const_spec(ov.shape)],
        out_specs=pl.BlockSpec((None, tq, GRP_Q), lambda bi, g, i: (bi, i, g)),
        out_shape=jax.ShapeDtypeStruct((b, t, NSA_WIDTH), BF16),
        compiler_params=_cparams(("arbitrary", "arbitrary", "arbitrary")),
        name="nsa_prompt",
    )(q, kcv, kvb, gt, ov)


def _nsa_dec_select_kernel(q_ref, kcv_ref, ov_ref, ocmp_ref, idx_ref, *, qpos):
    nc = kcv_ref.shape[1]
    q = q_ref[...]
    cmp_end = lax.broadcasted_iota(jnp.int32, (1, nc), 1) * CMP_STRIDE + (CMP_BLK - 1)
    mask = cmp_end <= qpos
    ov = ov_ref[...]
    imps = []
    for g in range(NSA_GROUPS):
        kcv = kcv_ref[g]
        p = _masked_softmax(_dot_nt(q[g * HPG:(g + 1) * HPG], kcv[:, :HEAD_DIM]), mask)
        ocmp_ref[g * HPG:(g + 1) * HPG, :] = _dot(p, kcv[:, HEAD_DIM:])
        p_hi, p_lo = _split2(jnp.sum(p, axis=0, keepdims=True))
        imps.append(jnp.dot(p_hi, ov, preferred_element_type=F32) + jnp.dot(p_lo, ov, preferred_element_type=F32))
    imp = jnp.concatenate(imps + [jnp.zeros((8 - NSA_GROUPS, ov.shape[1]), F32)], axis=0)
    score = _block_scores(imp, jnp.full((8, 1), qpos, jnp.int32))
    _, idx = _top_k_mask(score, N_SEL, idx_out=True)
    idx_ref[...] = idx


def _nsa_dec_select(q3, kcv, ov, qpos):
    db, nh, _ = q3.shape
    nc = kcv.shape[2]
    return pl.pallas_call(
        functools.partial(_nsa_dec_select_kernel, qpos=qpos),
        grid=(db,),
        in_specs=[pl.BlockSpec((None, nh, HEAD_DIM), lambda b: (b, 0, 0)),
                  pl.BlockSpec((None, NSA_GROUPS, nc, 2 * HEAD_DIM), lambda b: (b, 0, 0, 0)),
                  _const_spec(ov.shape)],
        out_specs=[pl.BlockSpec((None, nh, HEAD_DIM), lambda b: (b, 0, 0)),
                   pl.BlockSpec((None, 8, LANES), lambda b: (b, 0, 0))],
        out_shape=[jax.ShapeDtypeStruct((db, nh, HEAD_DIM), F32), jax.ShapeDtypeStruct((db, 8, LANES), jnp.int32)],
        compiler_params=_cparams(("arbitrary",)),
        name="nsa_dec_select",
    )(q3, kcv, ov)


def _nsa_dec_attend_kernel(idx_ref, pt_ref, *refs, past):
    k_refs = refs[:N_SEL]
    v_refs = refs[N_SEL:2 * N_SEL]
    (q_ref, new_ref, wk_ref, wv_ref, ocmp_ref, gt_ref, o_ref, wko_ref, wvo_ref) = refs[2 * N_SEL:]
    b = pl.program_id(0)
    g = pl.program_id(1)
    win = wk_ref.shape[0]
    new = new_ref[...]
    q = q_ref[...]
    row = lax.broadcasted_iota(jnp.int32, (SLC_BLK, 1), 0)

    def group_lanes(x):
        return jnp.where(g == 0, x[:, :HEAD_DIM], x[:, HEAD_DIM:])

    ks_blocks, vs_blocks, pos_blocks = [], [], []
    for i in range(N_SEL):
        blk = idx_ref[b, g * N_SEL + i]
        from_new = (blk * SLC_BLK + row) >= past
        ks_blocks.append(jnp.where(from_new, new[0:1, :], k_refs[i][...]))
        vs_blocks.append(jnp.where(from_new, new[1:2, :], v_refs[i][...]))
        pos_blocks.append(blk * SLC_BLK + lax.broadcasted_iota(jnp.int32, (1, SLC_BLK), 1))
    ks = group_lanes(jnp.concatenate(ks_blocks, axis=0))
    vs = group_lanes(jnp.concatenate(vs_blocks, axis=0))
    kpos = jnp.concatenate(pos_blocks, axis=1)
    p = _masked_softmax(_dot_nt(q, ks), kpos <= past)
    o_slc = _dot(p, vs)

    wk_new = jnp.concatenate([wk_ref[1:, :], new[2:3, :]], axis=0)
    wv_new = jnp.concatenate([wv_ref[1:, :], new[3:4, :]], axis=0)
    dist = win - 1 - lax.broadcasted_iota(jnp.int32, (1, win), 1)
    p = _masked_softmax(_dot_nt(q, group_lanes(wk_new)), dist < WINDOW)
    o_win = _dot(p, group_lanes(wv_new))

    @pl.when(g == 0)
    def _():
        wko_ref[...] = wk_new
        wvo_ref[...] = wv_new

    gt = gt_ref[...]
    o_ref[...] = gt[:, 0:1] * ocmp_ref[...] + gt[:, 1:2] * o_slc + gt[:, 2:3] * o_win


def _nsa_dec_attend(idx, page_table, pool_k, pool_v, q4, new_rows, win_k, win_v, o_cmp4, gt4, past):
    db, n_pages = page_table.shape
    win = win_k.shape[1]
    blocks_per_page = PAGE_SIZE // SLC_BLK

    def blk_map(i):
        def index(b, g, idx_s, pt_s):
            blk = idx_s[b, g * N_SEL + i]
            page = jnp.minimum(blk // blocks_per_page, n_pages - 1)
            return pt_s[b, page], blk % blocks_per_page, 0
        return index

    sel_specs = [pl.BlockSpec((None, SLC_BLK, KV_WIDTH), blk_map(i)) for i in range(N_SEL)]

    def per_bg(shape):
        return pl.BlockSpec((None, None) + shape, lambda b, g, *_: (b, g, 0, 0))

    def per_b(shape):
        return pl.BlockSpec((None,) + shape, lambda b, g, *_: (b, 0, 0))

    grid_spec = pltpu.PrefetchScalarGridSpec(
        num_scalar_prefetch=2,
        grid=(db, NSA_GROUPS),
        in_specs=sel_specs + sel_specs + [per_bg((HPG, HEAD_DIM)), per_b((4, KV_WIDTH)), per_b((win, KV_WIDTH)),
                                          per_b((win, KV_WIDTH)), per_bg((HPG, HEAD_DIM)), per_bg((HPG, 3))],
        out_specs=[per_bg((HPG, HEAD_DIM)), per_b((win, KV_WIDTH)), per_b((win, KV_WIDTH))],
    )
    return pl.pallas_call(
        functools.partial(_nsa_dec_attend_kernel, past=past),
        grid_spec=grid_spec,
        out_shape=[jax.ShapeDtypeStruct((db, NSA_GROUPS, HPG, HEAD_DIM), F32),
                   jax.ShapeDtypeStruct(win_k.shape, F32), jax.ShapeDtypeStruct(win_v.shape, F32)],
        compiler_params=_cparams(("arbitrary", "arbitrary")),
        name="nsa_dec_attend",
    )(idx, page_table, *([pool_k] * N_SEL), *([pool_v] * N_SEL), q4, new_rows, win_k, win_v, o_cmp4, gt4)


def _forget_lower_bound(lb_raw):
    m = jnp.max(lb_raw, axis=0, keepdims=True)
    e = jnp.exp(lb_raw - m)
    return e[0:1] / jnp.sum(e, axis=0, keepdims=True)


def _hgrn_out(o, hg, norm_g):
    o = o * lax.rsqrt(jnp.mean(o * o, axis=-1, keepdims=True) + EPS) * norm_g
    return o * (hg * _sigmoid(hg))


def _hgrn_prompt_kernel(hq_ref, hf_ref, hi_ref, hg_ref, lb_ref, ng_ref, tri_ref, o_ref, st_ref, state_t,
                        b_s, q_s, k_s, v_s):
    c = pl.program_id(2)
    n_c = pl.num_programs(2)
    lc = hq_ref.shape[0]

    @pl.when(c == 0)
    def _():
        state_t[...] = jnp.zeros_like(state_t)

    lb = _forget_lower_bound(lb_ref[...])
    f = lb + (1.0 - lb) * _sigmoid(hf_ref[...])
    k = 1.0 - f
    g_hi, g_mid, g_lo = _split3(jnp.log(f))
    tri = tri_ref[...]
    bcum = (jnp.dot(tri, g_hi, preferred_element_type=F32) + jnp.dot(tri, g_mid, preferred_element_type=F32)
            + jnp.dot(tri, g_lo, preferred_element_type=F32))
    q = hq_ref[...]
    v = hi_ref[...]
    b_s[...] = bcum
    q_s[...] = q
    k_s[...] = k
    v_s[...] = v
    st = state_t[...]

    o_inter = _dot_nt(q * jnp.exp(bcum), st)
    t_idx = lax.broadcasted_iota(jnp.int32, (HG_SUB, 1), 0)
    outs = []
    for i in range(lc // HG_SUB):
        r0 = i * HG_SUB
        b_i = bcum[r0:r0 + HG_SUB]
        q_i = q[r0:r0 + HG_SUB]
        o_i = o_inter[r0:r0 + HG_SUB]
        if i > 0:
            ref_b = bcum[r0:r0 + 1]
            a = _dot_nt(q_i * jnp.exp(b_i - ref_b), k[:r0] * jnp.exp(ref_b - bcum[:r0]))
            o_i = o_i + _dot(a, v[:r0])

        def diag(s, acc):
            b_row = b_s[pl.ds(r0 + s, 1), :]
            causal = t_idx >= s
            w = jnp.exp(jnp.where(causal, b_i - b_row, NEG)) * q_i * k_s[pl.ds(r0 + s, 1), :]
            return acc + jnp.sum(w, axis=-1, keepdims=True) * v_s[pl.ds(r0 + s, 1), :]

        outs.append(lax.fori_loop(0, HG_SUB, diag, o_i))
    o = jnp.concatenate(outs, axis=0)
    o_ref[...] = _hgrn_out(o, hg_ref[...], ng_ref[...]).astype(BF16)

    b_last = bcum[lc - 1:lc]
    kd = k * jnp.exp(b_last - bcum)
    st_new = st * jnp.exp(b_last) + _dot(v.T, kd)
    state_t[...] = st_new

    @pl.when(c == n_c - 1)
    def _():
        st_ref[...] = st_new.T


def _hgrn_prompt(hz3, hg_lb, norm_g, *, lc):
    b, t, _ = hz3.shape
    tri = jnp.tril(jnp.ones((lc, lc), F32)).astype(BF16)

    def seg_spec(seg):
        return pl.BlockSpec((None, lc, HG_DIM), lambda bi, h, c: (bi, c, seg * HG_HEADS + h))

    return pl.pallas_call(
        _hgrn_prompt_kernel,
        grid=(b, HG_HEADS, t // lc),
        in_specs=[seg_spec(0), seg_spec(1), seg_spec(2), seg_spec(3),
                  pl.BlockSpec((hg_lb.shape[0], HG_DIM), lambda bi, h, c: (0, h)),
                  _const_spec((1, HG_DIM)), _const_spec((lc, lc))],
        out_specs=[pl.BlockSpec((None, lc, HG_DIM), lambda bi, h, c: (bi, c, h)),
                   pl.BlockSpec((None, None, HG_DIM, HG_DIM), lambda bi, h, c: (bi, h, 0, 0))],
        out_shape=[jax.ShapeDtypeStruct((b, t, HG_WIDTH), BF16),
                   jax.ShapeDtypeStruct((b, HG_HEADS, HG_DIM, HG_DIM), F32)],
        scratch_shapes=[pltpu.VMEM((HG_DIM, HG_DIM), F32)] + [pltpu.VMEM((lc, HG_DIM), F32)] * 4,
        compiler_params=_cparams(("arbitrary", "arbitrary", "arbitrary")),
        name="hgrn_prompt",
    )(hz3, hz3, hz3, hz3, hg_lb, norm_g, tri)


def _hgrn_dec_kernel(hz_ref, s0_ref, lb_ref, ng_ref, o_ref, s_ref):
    hz = hz_ref[...]
    lb_all = _forget_lower_bound(lb_ref[...])
    eye = lax.broadcasted_iota(jnp.int32, (HG_DIM, HG_DIM), 0) == lax.broadcasted_iota(jnp.int32, (HG_DIM, HG_DIM), 1)

    def column(row):
        return jnp.sum(jnp.where(eye, row, 0.0), axis=1, keepdims=True)

    outs = []
    for h in range(HG_HEADS):
        def seg(j):
            return hz[:, j * HG_WIDTH + h * HG_DIM:j * HG_WIDTH + (h + 1) * HG_DIM]

        lb = lb_all[:, h * HG_DIM:(h + 1) * HG_DIM]
        f = lb + (1.0 - lb) * _sigmoid(seg(1))
        q, k, v = seg(0), 1.0 - f, seg(2)
        s0 = s0_ref[h]
        s_ref[h] = column(f) * s0 + column(k) * v
        o = jnp.sum(column(q * f) * s0, axis=0, keepdims=True) + jnp.sum(q * k, axis=-1, keepdims=True) * v
        outs.append(_hgrn_out(o, seg(3), ng_ref[...]))
    o_ref[...] = jnp.concatenate(outs, axis=1)


def _hgrn_dec(hz, s0, hg_lb, norm_g):
    db = hz.shape[0]
    return pl.pallas_call(
        _hgrn_dec_kernel,
        grid=(db,),
        in_specs=[pl.BlockSpec((None, 1, hz.shape[1]), lambda b: (b, 0, 0)),
                  pl.BlockSpec((None, HG_HEADS, HG_DIM, HG_DIM), lambda b: (b, 0, 0, 0)),
                  _const_spec(hg_lb.shape), _const_spec((1, HG_DIM))],
        out_specs=[pl.BlockSpec((None, 1, HG_WIDTH), lambda b: (b, 0, 0)),
                   pl.BlockSpec((None, HG_HEADS, HG_DIM, HG_DIM), lambda b: (b, 0, 0, 0))],
        out_shape=[jax.ShapeDtypeStruct((db, 1, HG_WIDTH), F32), jax.ShapeDtypeStruct(s0.shape, F32)],
        compiler_params=_cparams(("arbitrary",)),
        name="hgrn_dec",
    )(hz.reshape(db, 1, -1), s0, hg_lb, norm_g)


def _merge_kernel(on_ref, oh_ref, sm_ref, x_ref, g1_ref, sh_ref, sc_ref, ng_ref, wn_ref, wh_ref, wo_ref,
                  x1_ref, h2_ref):
    d = x_ref.shape[1]
    sm = sm_ref[...]
    mix = sm[:, :d] * _dot(on_ref[...], wn_ref[...]) + sm[:, d:] * _dot(oh_ref[...], wh_ref[...])
    x1 = x_ref[...] + g1_ref[...] * _dot(mix, wo_ref[...])
    x1_ref[...] = x1
    xn = x1 * lax.rsqrt(jnp.mean(x1 * x1, axis=-1, keepdims=True) + EPS) * ng_ref[...]
    h2_ref[...] = (xn * (1.0 + sc_ref[...]) + sh_ref[...]).astype(BF16)


def _merge(o_nsa, o_hg, sm, x2d, g1, sh2, sc2, norm_g, w_br_nsa, w_br_hg, w_out, *, tm, tiles_per_mod):
    r, d = x2d.shape
    rows_mod = g1.shape[1]
    mod_spec = pl.BlockSpec((None, rows_mod, d), lambda i: (i // tiles_per_mod, 0, 0))

    def row_spec(width):
        return pl.BlockSpec((tm, width), lambda i: (i, 0))

    return pl.pallas_call(
        _merge_kernel,
        grid=(r // tm,),
        in_specs=[row_spec(o_nsa.shape[1]), row_spec(o_hg.shape[1]), row_spec(sm.shape[1]), row_spec(d),
                  mod_spec, mod_spec, mod_spec, _const_spec((1, d)),
                  _const_spec(w_br_nsa.shape), _const_spec(w_br_hg.shape), _const_spec(w_out.shape)],
        out_specs=[row_spec(d), row_spec(d)],
        out_shape=[jax.ShapeDtypeStruct((r, d), F32), jax.ShapeDtypeStruct((r, d), BF16)],
        compiler_params=_cparams(("arbitrary",)),
        name="merge",
    )(o_nsa, o_hg, sm, x2d, g1, sh2, sc2, norm_g, w_br_nsa, w_br_hg, w_out)


FF_CHUNKS = 2


def _ffn_core(h2, x1, g2, wu_ref, cw_ref, cb_ref, wd_ref, prev_rows):
    d_ff = wd_ref.shape[0]
    ch = d_ff // FF_CHUNKS
    acc = None
    us = []
    for ci in range(FF_CHUNKS):
        c0, c1 = ci * ch, (ci + 1) * ch
        u = jnp.dot(h2, wu_ref[:, c0:c1], preferred_element_type=F32)
        v = jnp.dot(h2, wu_ref[:, d_ff + c0:d_ff + c1], preferred_element_type=F32)
        u_m1, u_m2 = prev_rows(u, c0, c1)
        y = cb_ref[:, c0:c1] + cw_ref[0:1, c0:c1] * u_m2 + cw_ref[1:2, c0:c1] * u_m1 + cw_ref[2:3, c0:c1] * u
        part = _dot(y * _sigmoid(y) * v, wd_ref[c0:c1, :])
        acc = part if acc is None else acc + part
        us.append(u)
    return x1 + g2 * acc, us


def _ffn_prompt_kernel(h2_ref, x1_ref, g2_ref, wu_ref, cw_ref, cb_ref, wd_ref, y_ref, tail_ref, carry):
    ti = pl.program_id(1)
    tm = h2_ref.shape[0]

    @pl.when(ti == 0)
    def _():
        carry[...] = jnp.zeros_like(carry)

    row = lax.broadcasted_iota(jnp.int32, (tm, 1), 0)

    def prev_rows(u, c0, c1):
        last = carry[:, c0:c1]
        m1 = jnp.where(row == 0, last[7:8], pltpu.roll(u, 1, 0))
        m2 = jnp.where(row == 0, last[6:7], jnp.where(row == 1, last[7:8], pltpu.roll(u, 2, 0)))
        return m1, m2

    out, us = _ffn_core(h2_ref[...], x1_ref[...], g2_ref[...], wu_ref, cw_ref, cb_ref, wd_ref, prev_rows)
    y_ref[...] = out
    ch = us[0].shape[1]
    for ci, u in enumerate(us):
        carry[:, ci * ch:(ci + 1) * ch] = u[tm - 8:, :]
    tail_ref[...] = carry[...]


def _ffn_prompt(h2, x1, g2, w_up, conv_w8, conv_b, w_down, *, b, tm):
    r, d = x1.shape
    d_ff = w_down.shape[0]
    tpb = r // b // tm

    def row_spec(width):
        return pl.BlockSpec((tm, width), lambda bi, i: (bi * tpb + i, 0))

    return pl.pallas_call(
        _ffn_prompt_kernel,
        grid=(b, tpb),
        in_specs=[row_spec(d), row_spec(d), pl.BlockSpec((None, 1, d), lambda bi, i: (bi, 0, 0)),
                  _resident_spec(w_up.shape), _const_spec(conv_w8.shape), _const_spec(conv_b.shape),
                  _resident_spec(w_down.shape)],
        out_specs=[row_spec(d), pl.BlockSpec((None, 8, d_ff), lambda bi, i: (bi, 0, 0))],
        out_shape=[jax.ShapeDtypeStruct((r, d), F32), jax.ShapeDtypeStruct((b, 8, d_ff), F32)],
        scratch_shapes=[pltpu.VMEM((8, d_ff), F32)],
        compiler_params=_cparams(("arbitrary", "arbitrary")),
        name="ffn_prompt",
    )(h2, x1, g2, w_up, conv_w8, conv_b, w_down)


def _ffn_dec_kernel(h2_ref, x1_ref, g2_ref, wu_ref, cw_ref, cb_ref, wd_ref, m1_ref, m2_ref, y_ref, u_ref):
    def prev_rows(u, c0, c1):
        return m1_ref[:, c0:c1], m2_ref[:, c0:c1]

    out, us = _ffn_core(h2_ref[...], x1_ref[...], g2_ref[...], wu_ref, cw_ref, cb_ref, wd_ref, prev_rows)
    y_ref[...] = out
    u_ref[...] = jnp.concatenate(us, axis=1)


def _ffn_dec(h2, x1, g2, w_up, conv_w8, conv_b, w_down, u_m1, u_m2):
    r, d = x1.shape
    d_ff = w_down.shape[0]
    args = (h2, x1, g2, w_up, conv_w8, conv_b, w_down, u_m1, u_m2)
    return pl.pallas_call(
        _ffn_dec_kernel,
        grid=(1,),
        in_specs=[_const_spec(a.shape) for a in args],
        out_specs=[_const_spec((r, d)), _const_spec((r, d_ff))],
        out_shape=[jax.ShapeDtypeStruct((r, d), F32), jax.ShapeDtypeStruct((r, d_ff), F32)],
        compiler_params=_cparams(("arbitrary",)),
        name="ffn_dec",
    )(*args)


def _pack_w_in(w_in):
    cuts = np.cumsum([NSA_WIDTH] + [KV_WIDTH] * 6 + [3 * NSA_HEADS] + [HG_WIDTH] * 4)
    q, kv, gates, rest = w_in[:, :cuts[0]], w_in[:, cuts[0]:cuts[6]], w_in[:, cuts[6]:cuts[7]], w_in[:, cuts[7]:]
    hz, m = rest[:, :4 * HG_WIDTH], rest[:, 4 * HG_WIDTH:]
    per_group = 3 * HPG
    pad = jnp.zeros((w_in.shape[0], LANES - per_group), w_in.dtype)
    gate_cols = []
    for g in range(NSA_GROUPS):
        gate_cols += [gates[:, g * per_group:(g + 1) * per_group], pad]
    return jnp.concatenate([q, kv] + gate_cols + [hz, m], axis=1).astype(BF16)


def _rope_tables(pos):
    half = ROT_DIM // 2
    inv = ROPE_THETA ** (-jnp.arange(0, ROT_DIM, 2, dtype=F32) / ROT_DIM)
    ang = pos.astype(F32)[:, None] * inv[None, :]
    cos, sin = jnp.cos(ang), jnp.sin(ang)
    n = pos.shape[0]
    ones = jnp.ones((n, HEAD_DIM - ROT_DIM), F32)
    zeros = jnp.zeros((n, HEAD_DIM - ROT_DIM), F32)
    z8 = jnp.zeros((n, half), F32)
    c = jnp.concatenate([cos, cos, ones], axis=1)
    s_lo = jnp.concatenate([-sin, z8, zeros], axis=1)
    s_hi = jnp.concatenate([z8, sin, zeros], axis=1)
    return tuple(jnp.concatenate([t, t], axis=1) for t in (c, s_lo, s_hi))


def _cmp_weights(cmp_w1, cmp_w2, cmp_pe):
    n_half = CMP_BLK // CMP_STRIDE
    eye = jnp.eye(NSA_GROUPS, dtype=F32)
    w1s, w2s, pes = [], [], []
    for j in range(2):
        w = cmp_w1[j].reshape(n_half, CMP_STRIDE, HEAD_DIM, HEAD_DIM)
        big = jnp.einsum('rsdh,ge->sgdreh', w, eye)
        w1s.append(big.reshape(CMP_STRIDE * KV_WIDTH, n_half * KV_WIDTH))
        w2s.append(jnp.einsum('dh,ge->gdeh', cmp_w2[j], eye).reshape(KV_WIDTH, KV_WIDTH))
        pe = cmp_pe[j].reshape(n_half, CMP_STRIDE, 1, HEAD_DIM)
        pe = jnp.broadcast_to(pe, (n_half, CMP_STRIDE, NSA_GROUPS, HEAD_DIM)).reshape(n_half, -1)
        pes.append(jnp.concatenate([pe, jnp.zeros((8 - n_half, pe.shape[1]), F32)], axis=0))
    return jnp.stack(w1s).astype(BF16), jnp.stack(w2s).astype(BF16), jnp.stack(pes).astype(BF16)


def _overlap(nc_pad, nc, ns, ns_pad):
    start = np.arange(nc_pad) * CMP_STRIDE
    end = start + CMP_BLK - 1
    s0 = np.arange(ns_pad) * SLC_BLK
    s1 = s0 + SLC_BLK - 1
    m = (start[:, None] <= s1[None, :]) & (end[:, None] >= s0[None, :])
    m &= (np.arange(nc_pad) < nc)[:, None] & (np.arange(ns_pad) < ns)[None, :]
    return jnp.asarray(m, BF16)


def kernel(x_prompt, x_sample, cache_cmp_k, cache_cmp_v, cache_slc_k, cache_slc_v, cache_win_k, cache_win_v,
           state_hgrn, state_ffn_conv, page_table, c_prompt, c_sample, ada_w, ada_b, norm_attn, norm_ffn, w_in,
           q_gain, k_gain, cmp_w1, cmp_w2, cmp_pe, hg_lb, hg_norm, w_br_nsa, w_br_hg, w_out, w_up, conv_w,
           conv_b, w_down):
    depth = ada_w.shape[0]
    assert depth == 1, "single-layer trunk only"
    b, t, d = x_prompt.shape
    db, tn, _ = x_sample.shape
    assert tn == 1
    n_pages = page_table.shape[1]
    past = n_pages * PAGE_SIZE
    d_ff = w_down.shape[1]
    tq = 256
    tm = 256
    lc = 128
    assert t % tq == 0 and t >= WINDOW and WINDOW % tq == 0 and n_pages % PAGES_PER_STEP == 0
    assert cache_win_k.shape[2] == WINDOW

    w_pack = _pack_w_in(w_in[0])
    qg = jnp.tile(q_gain[0], NSA_HEADS).reshape(1, NSA_WIDTH)
    kg = jnp.concatenate([jnp.tile(k_gain[0], (1, NSA_GROUPS)), jnp.zeros((5, KV_WIDTH), F32)], axis=0)
    hd = np.arange(NSA_WIDTH) // HEAD_DIM
    bd = jnp.asarray(hd[:, None] == hd[None, :], BF16)
    w1c, w2c, pe_x = _cmp_weights(cmp_w1[0], cmp_w2[0], cmp_pe[0])
    conv_w8 = jnp.concatenate([conv_w[0], jnp.zeros((8 - CONV_W, d_ff), F32)], axis=0)
    conv_b2 = conv_b[0].reshape(1, d_ff)
    wbn, wbh, wo = w_br_nsa[0].astype(BF16), w_br_hg[0].astype(BF16), w_out[0].astype(BF16)
    wu, wd = w_up[0].astype(BF16), w_down[0].astype(BF16)
    norm_a, norm_f, hg_n = norm_attn[0].reshape(1, d), norm_ffn[0].reshape(1, d), hg_norm[0].reshape(1, HG_DIM)

    n_c = b + db
    c_all = jnp.concatenate([c_prompt, c_sample, jnp.zeros((-n_c % 8, d), F32)], axis=0)
    ada = _ada(c_all, ada_w[0], ada_b[0])
    mods_p = [ada[:b, i * d:(i + 1) * d].reshape(b, 1, d) for i in range(6)]
    mods_s = [ada[b:n_c, i * d:(i + 1) * d].reshape(1, db, d) for i in range(6)]

    tpb = t // tm
    (q, kc_r, vc_r, ks_r, vs_r, kw_r, vw_r, kvb, gt, hz, sm) = _inproj(
        x_prompt.reshape(b * t, d), mods_p[0], mods_p[1], norm_a, w_pack, qg, kg, _rope_tables(jnp.arange(t)), bd,
        tm=tm, tiles_per_mod=tpb, rope_tiles=tpb)

    nsub = t // CMP_STRIDE
    pk = _mm(kc_r.reshape(b * nsub, CMP_STRIDE * KV_WIDTH), w1c[0], 512, "cmp_proj_k").reshape(b, nsub, -1)
    pv = _mm(vc_r.reshape(b * nsub, CMP_STRIDE * KV_WIDTH), w1c[1], 512, "cmp_proj_v").reshape(b, nsub, -1)
    kcv = _cmp_combine(pk, pv, pe_x, w1c, w2c)
    ov = _overlap(nsub, nsub - 1, t // SLC_BLK, LANES)
    o_nsa = _nsa_prompt(q.reshape(b, t, -1), kcv, kvb.reshape(b, t, -1), gt.reshape(b, t, -1), ov, tq=tq)

    o_hg, st_p = _hgrn_prompt(hz.reshape(b, t, -1), hg_lb, hg_n, lc=lc)

    x1, h2 = _merge(o_nsa.reshape(b * t, -1), o_hg.reshape(b * t, -1), sm, x_prompt.reshape(b * t, d),
                    mods_p[2], mods_p[3], mods_p[4], norm_f, wbn, wbh, wo, tm=tm, tiles_per_mod=tpb)
    y_p, tail = _ffn_prompt(h2, x1, mods_p[5], wu, conv_w8, conv_b2, wd, b=b, tm=tm)

    def rows5(a):
        return a.reshape(1, b, t, NSA_GROUPS, HEAD_DIM)

    w = min(WINDOW, t)
    out_p = (rows5(kc_r), rows5(vc_r), rows5(ks_r), rows5(vs_r), rows5(kw_r)[:, :, -w:], rows5(vw_r)[:, :, -w:],
             st_p[None], tail[None, :, 8 - (CONV_W - 1):, :])

    pos_s = jnp.full((db,), past, jnp.int32)
    (q_s, kc_s, vc_s, ks_s, vs_s, kw_s, vw_s, _, gt_s, hz_s, sm_s) = _inproj(
        x_sample.reshape(db, d), mods_s[0], mods_s[1], norm_a, w_pack, qg, kg, _rope_tables(pos_s), bd,
        tm=db, tiles_per_mod=1, rope_tiles=1)

    sub = PAGE_SIZE // CMP_STRIDE
    assert (past + tn) // CMP_STRIDE == past // CMP_STRIDE
    pk_s = _cmp_proj_paged(cache_cmp_k[0].reshape(-1, sub, CMP_STRIDE * KV_WIDTH), page_table, w1c[0], "cmp_proj_pk")
    pv_s = _cmp_proj_paged(cache_cmp_v[0].reshape(-1, sub, CMP_STRIDE * KV_WIDTH), page_table, w1c[1], "cmp_proj_pv")
    kcv_s = _cmp_combine(pk_s, pv_s, pe_x, w1c, w2c)
    nsub_s = past // CMP_STRIDE
    ns_s = -(-(past + tn) // SLC_BLK)
    ns_pad = -(-ns_s // LANES) * LANES
    ov_s = _overlap(nsub_s, nsub_s - 1, ns_s, ns_pad)
    o_cmp_s, idx8 = _nsa_dec_select(q_s.reshape(db, NSA_HEADS, HEAD_DIM), kcv_s, ov_s, past)
    idx = idx8[:, :NSA_GROUPS, :N_SEL].reshape(db, NSA_GROUPS * N_SEL)
    new_rows = jnp.stack([ks_s, vs_s, kw_s, vw_s], axis=1)
    per_group = 3 * HPG
    gt4 = jnp.stack([gt_s[:, g * LANES:g * LANES + per_group] for g in range(NSA_GROUPS)], axis=1)
    o_nsa_s, wk_new, wv_new = _nsa_dec_attend(
        idx, page_table, cache_slc_k[0].reshape(-1, PAGE_SIZE, KV_WIDTH), cache_slc_v[0].reshape(-1, PAGE_SIZE, KV_WIDTH),
        q_s.astype(F32).reshape(db, NSA_GROUPS, HPG, HEAD_DIM), new_rows,
        cache_win_k[0].reshape(db, WINDOW, KV_WIDTH), cache_win_v[0].reshape(db, WINDOW, KV_WIDTH),
        o_cmp_s.reshape(db, NSA_GROUPS, HPG, HEAD_DIM), gt4.reshape(db, NSA_GROUPS, HPG, 3), past)

    o_hg_s, st_s = _hgrn_dec(hz_s, state_hgrn[0], hg_lb, hg_n)
    x1_s, h2_s = _merge(o_nsa_s.reshape(db, NSA_WIDTH), o_hg_s.reshape(db, HG_WIDTH), sm_s, x_sample.reshape(db, d),
                        mods_s[2], mods_s[3], mods_s[4], norm_f, wbn, wbh, wo, tm=db, tiles_per_mod=1)
    buf = state_ffn_conv[0]
    y_s, u_s = _ffn_dec(h2_s, x1_s, mods_s[5].reshape(db, d), wu, conv_w8, conv_b2, wd, buf[:, 1], buf[:, 0])
    conv_s = jnp.stack([buf[:, 1], u_s], axis=1)

    def rows5s(a):
        return a.reshape(1, db, tn, NSA_GROUPS, HEAD_DIM)

    out_s = (rows5s(kc_s), rows5s(vc_s), rows5s(ks_s), rows5s(vs_s),
             wk_new.reshape(1, db, WINDOW, NSA_GROUPS, HEAD_DIM), wv_new.reshape(1, db, WINDOW, NSA_GROUPS, HEAD_DIM),
             st_s[None], conv_s[None])
    return (y_p.reshape(b, t, d), y_s.reshape(db, tn, d)) + out_p + out_s
```

```python
import functools

import numpy as np
import jax
import jax.numpy as jnp
from jax import lax
from jax.experimental import pallas as pl
from jax.experimental.pallas import tpu as pltpu

F32 = jnp.float32
BF16 = jnp.bfloat16

HEAD_DIM = 64
NSA_HEADS = 8
NSA_GROUPS = 2
HPG = NSA_HEADS // NSA_GROUPS
ROT_DIM = HEAD_DIM // 4
ROPE_THETA = 500000.0
CMP_BLK = 32
CMP_STRIDE = 16
SLC_BLK = 64
N_SEL = 16
WINDOW = 512
PAGE_SIZE = 128
HG_HEADS = 4
HG_DIM = 128
HG_SUB = 16
ROW_BLK = 128
CONV_W = 3
EPS = 1e-6
NEG = -1e30
SEL_BIAS = -1e9
LANES = 128
VMEM_LIMIT = 56 * 1024 * 1024

NSA_WIDTH = NSA_HEADS * HEAD_DIM
KV_WIDTH = NSA_GROUPS * HEAD_DIM
HG_WIDTH = HG_HEADS * HG_DIM
GRP_Q = HPG * HEAD_DIM

SEG_Q = 0
SEG_KV = SEG_Q + NSA_WIDTH
SEG_G = SEG_KV + 6 * KV_WIDTH
SEG_H = SEG_G + NSA_GROUPS * LANES
SEG_M = SEG_H + 4 * HG_WIDTH


def _cparams(sem):
    return pltpu.CompilerParams(dimension_semantics=sem, vmem_limit_bytes=VMEM_LIMIT)


def _dot(a, b):
    return jnp.dot(a.astype(BF16), b.astype(BF16), preferred_element_type=F32)


def _dot_nt(a, b):
    return lax.dot_general(a.astype(BF16), b.astype(BF16), (((1,), (1,)), ((), ())), preferred_element_type=F32)


def _split2(x):
    hi = x.astype(BF16)
    lo = (x - hi.astype(F32)).astype(BF16)
    return hi, lo


def _split3(x):
    hi = x.astype(BF16)
    r = x - hi.astype(F32)
    mid = r.astype(BF16)
    lo = (r - mid.astype(F32)).astype(BF16)
    return hi, mid, lo


def _dot_x3(a, b):
    ah, al = _split2(a)
    bh, bl = _split2(b)
    return (jnp.dot(ah, bh, preferred_element_type=F32) + jnp.dot(al, bh, preferred_element_type=F32)
            + jnp.dot(ah, bl, preferred_element_type=F32))


def _sigmoid(x):
    return 1.0 / (1.0 + jnp.exp(-x))


def _const_spec(shape):
    nd = len(shape)
    return pl.BlockSpec(shape, lambda *_: (0,) * nd)


def _resident_spec(shape):
    nd = len(shape)
    return pl.BlockSpec(shape, lambda *_: (0,) * nd, pipeline_mode=pl.Buffered(1))


def _ada_kernel(c_ref, w_ref, b_ref, o_ref):
    o_ref[...] = _dot_x3(c_ref[...], w_ref[...]) + b_ref[...]


def _ada(c_all, w, b):
    r, d = c_all.shape
    n = w.shape[1]
    tn = 1536
    return pl.pallas_call(
        _ada_kernel,
        grid=(n // tn,),
        in_specs=[_const_spec((r, d)), pl.BlockSpec((d, tn), lambda j: (0, j)), pl.BlockSpec((1, tn), lambda j: (0, j))],
        out_specs=pl.BlockSpec((r, tn), lambda j: (0, j)),
        out_shape=jax.ShapeDtypeStruct((r, n), F32),
        compiler_params=_cparams(("arbitrary",)),
        name="ada",
    )(c_all, w, b.reshape(1, n))


def _rope(x, c, s_lo, s_hi):
    width = x.shape[1]
    reps = width // LANES
    if reps > 1:
        c = jnp.concatenate([c] * reps, axis=1)
        s_lo = jnp.concatenate([s_lo] * reps, axis=1)
        s_hi = jnp.concatenate([s_hi] * reps, axis=1)
    half = ROT_DIM // 2
    return x * c + pltpu.roll(x, half, 1) * s_hi + pltpu.roll(x, width - half, 1) * s_lo


def _head_rms(z, bd):
    hi, lo = _split2(z * z)
    ss = jnp.dot(hi, bd, preferred_element_type=F32) + jnp.dot(lo, bd, preferred_element_type=F32)
    return z * lax.rsqrt(ss * (1.0 / HEAD_DIM) + EPS)


def _inproj_kernel(x_ref, sh_ref, sc_ref, ng_ref, w_ref, qg_ref, kg_ref, rc_ref, rlo_ref, rhi_ref, bd_ref,
                   q_ref, kc_ref, vc_ref, ks_ref, vs_ref, kw_ref, vw_ref, kvb_ref, gt_ref, hz_ref, sm_ref):
    x = x_ref[...]
    xn = x * lax.rsqrt(jnp.mean(x * x, axis=-1, keepdims=True) + EPS) * ng_ref[...]
    h = (xn * (1.0 + sc_ref[...]) + sh_ref[...]).astype(BF16)
    rc, rlo, rhi = rc_ref[...], rlo_ref[...], rhi_ref[...]
    bd = bd_ref[...]

    zq = jnp.dot(h, w_ref[:, SEG_Q:SEG_KV], preferred_element_type=F32)
    q = _rope(_head_rms(zq, bd) * qg_ref[...], rc, rlo, rhi)
    q_ref[...] = (q * (HEAD_DIM ** -0.5)).astype(BF16)

    zkv = jnp.dot(h, w_ref[:, SEG_KV:SEG_G], preferred_element_type=F32)
    bd1 = bd[:LANES, :LANES]
    k_refs = (kc_ref, ks_ref, kw_ref)
    v_refs = (vc_ref, vs_ref, vw_ref)
    rows = []
    for j in range(3):
        zk = zkv[:, 2 * j * KV_WIDTH:(2 * j + 1) * KV_WIDTH]
        k = _rope(_head_rms(zk, bd1) * kg_ref[j:j + 1, :], rc, rlo, rhi)
        v = zkv[:, (2 * j + 1) * KV_WIDTH:(2 * j + 2) * KV_WIDTH]
        k_refs[j][...] = k
        v_refs[j][...] = v
        rows.append((k, v))
    parts = []
    for g in range(NSA_GROUPS):
        for k, v in rows[1:]:
            parts.append(k[:, g * HEAD_DIM:(g + 1) * HEAD_DIM])
            parts.append(v[:, g * HEAD_DIM:(g + 1) * HEAD_DIM])
    kvb_ref[...] = jnp.concatenate(parts, axis=1).astype(BF16)

    gt_ref[...] = _sigmoid(jnp.dot(h, w_ref[:, SEG_G:SEG_H], preferred_element_type=F32))
    hz_ref[...] = jnp.dot(h, w_ref[:, SEG_H:SEG_M], preferred_element_type=F32)
    sm_ref[...] = _sigmoid(jnp.dot(h, w_ref[:, SEG_M:], preferred_element_type=F32))


def _inproj(x2d, shift, scale, norm_g, w_pack, qg, kg, rope_tabs, bd, *, tm, tiles_per_mod, rope_tiles):
    r, d = x2d.shape
    n_tiles = r // tm
    rows_mod = shift.shape[1]
    mod_spec = pl.BlockSpec((None, rows_mod, d), lambda i: (i // tiles_per_mod, 0, 0))
    rope_spec = pl.BlockSpec((tm, LANES), lambda i: (i % rope_tiles, 0))

    def row_spec(width):
        return pl.BlockSpec((tm, width), lambda i: (i, 0))

    widths = (NSA_WIDTH,) + (KV_WIDTH,) * 6 + (4 * KV_WIDTH, NSA_GROUPS * LANES, 4 * HG_WIDTH, w_pack.shape[1] - SEG_M)
    dtypes = (BF16,) + (F32,) * 6 + (BF16, F32, F32, F32)
    return pl.pallas_call(
        _inproj_kernel,
        grid=(n_tiles,),
        in_specs=[row_spec(d), mod_spec, mod_spec, _const_spec((1, d)), _resident_spec(w_pack.shape),
                  _const_spec(qg.shape), _const_spec(kg.shape), rope_spec, rope_spec, rope_spec, _const_spec(bd.shape)],
        out_specs=[row_spec(w) for w in widths],
        out_shape=[jax.ShapeDtypeStruct((r, w), dt) for w, dt in zip(widths, dtypes)],
        compiler_params=_cparams(("arbitrary",)),
        name="inproj",
    )(x2d, shift, scale, norm_g, w_pack, qg, kg, *rope_tabs, bd)


def _mm_kernel(a_ref, w_ref, o_ref):
    o_ref[...] = _dot(a_ref[...], w_ref[...])


def _mm(a, w, tm, name):
    m, k = a.shape
    n = w.shape[1]
    tm = min(tm, m)
    return pl.pallas_call(
        _mm_kernel,
        grid=(m // tm,),
        in_specs=[pl.BlockSpec((tm, k), lambda i: (i, 0)), _const_spec((k, n))],
        out_specs=pl.BlockSpec((tm, n), lambda i: (i, 0)),
        out_shape=jax.ShapeDtypeStruct((m, n), F32),
        compiler_params=_cparams(("arbitrary",)),
        name=name,
    )(a, w)


PAGES_PER_STEP = 16


def _cmp_proj_paged_kernel(pt_ref, *refs):
    page_refs = refs[:PAGES_PER_STEP]
    w_ref, o_ref, rows_ref = refs[PAGES_PER_STEP:]
    sub = page_refs[0].shape[0]
    for j, pr in enumerate(page_refs):
        rows_ref[j * sub:(j + 1) * sub, :] = pr[...].astype(BF16)
    o_ref[...] = jnp.dot(rows_ref[...], w_ref[...], preferred_element_type=F32)


def _cmp_proj_paged(pool3, page_table, w, name):
    db, n_pages = page_table.shape
    _, sub, kdim = pool3.shape
    n = w.shape[1]
    steps = n_pages // PAGES_PER_STEP
    page_specs = [pl.BlockSpec((None, sub, kdim), lambda b, c, pt, j=j: (pt[b, c * PAGES_PER_STEP + j], 0, 0))
                  for j in range(PAGES_PER_STEP)]
    grid_spec = pltpu.PrefetchScalarGridSpec(
        num_scalar_prefetch=1,
        grid=(db, steps),
        in_specs=page_specs + [pl.BlockSpec((kdim, n), lambda b, c, pt: (0, 0))],
        out_specs=pl.BlockSpec((None, PAGES_PER_STEP * sub, n), lambda b, c, pt: (b, c, 0)),
        scratch_shapes=[pltpu.VMEM((PAGES_PER_STEP * sub, kdim), BF16)],
    )
    return pl.pallas_call(
        _cmp_proj_paged_kernel,
        grid_spec=grid_spec,
        out_shape=jax.ShapeDtypeStruct((db, n_pages * sub, n), F32),
        compiler_params=_cparams(("arbitrary", "arbitrary")),
        name=name,
    )(page_table, *([pool3] * PAGES_PER_STEP), w)


def _cmp_combine_kernel(pk_ref, pv_ref, pe_ref, w1_ref, w2_ref, o_ref):
    n = pk_ref.shape[0]
    outs = []
    for j, p_ref in enumerate((pk_ref, pv_ref)):
        pe = jnp.dot(pe_ref[j], w1_ref[j], preferred_element_type=F32)
        pe_hid = pe[0:1, :KV_WIDTH] + pe[1:2, KV_WIDTH:]
        p = p_ref[...]
        nxt = pltpu.roll(p[:, KV_WIDTH:], n - 1, 0)
        hid = pe_hid + p[:, :KV_WIDTH] + nxt
        outs.append(_dot(jax.nn.gelu(hid), w2_ref[j]))
    for g in range(NSA_GROUPS):
        o_ref[g] = jnp.concatenate([o[:, g * HEAD_DIM:(g + 1) * HEAD_DIM] for o in outs], axis=1).astype(BF16)


def _cmp_combine(pk, pv, pe_x, w1, w2bd):
    bx, n, c = pk.shape
    return pl.pallas_call(
        _cmp_combine_kernel,
        grid=(bx,),
        in_specs=[pl.BlockSpec((None, n, c), lambda b: (b, 0, 0)), pl.BlockSpec((None, n, c), lambda b: (b, 0, 0)),
                  _const_spec(pe_x.shape), _const_spec(w1.shape), _const_spec(w2bd.shape)],
        out_specs=pl.BlockSpec((None, NSA_GROUPS, n, 2 * HEAD_DIM), lambda b: (b, 0, 0, 0)),
        out_shape=jax.ShapeDtypeStruct((bx, NSA_GROUPS, n, 2 * HEAD_DIM), BF16),
        compiler_params=_cparams(("arbitrary",)),
        name="cmp_combine",
    )(pk, pv, pe_x, w1, w2bd)


def _masked_softmax(s, mask):
    s = jnp.where(mask, s, NEG)
    m = jnp.max(s, axis=-1, keepdims=True)
    e = jnp.where(mask, jnp.exp(s - m), 0.0)
    return e / jnp.maximum(jnp.sum(e, axis=-1, keepdims=True), 1e-30)


def _top_k_mask(score, k, idx_out=False):
    r, l = score.shape
    lane = lax.broadcasted_iota(jnp.int32, (r, l), 1)
    sel = jnp.zeros((r, l), jnp.bool_)
    idx = jnp.zeros((r, LANES), jnp.int32)
    lane_out = lax.broadcasted_iota(jnp.int32, (r, LANES), 1)
    work = score
    for it in range(k):
        first = jnp.argmax(work, axis=-1, keepdims=True).astype(jnp.int32)
        pick = lane == first
        sel = jnp.logical_or(sel, pick)
        work = jnp.where(pick, -jnp.inf, work)
        if idx_out:
            idx = jnp.where(lane_out == it, first, idx)
    return (sel, idx) if idx_out else sel


def _block_scores(imp, qpos):
    blk = lax.broadcasted_iota(jnp.int32, imp.shape, 1)
    cur = qpos >> 6
    valid = (blk << 6) <= qpos
    forced = (blk == 0) | (blk == cur) | (blk == cur - 1)
    return jnp.where(forced, 1e9, jnp.where(valid, imp, -1e9))


def _nsa_prompt_kernel(q_ref, kcv_ref, kvb_ref, gt_ref, ov_ref, o_ref, s_scr, p_scr, m_scr, a_scr, acc_scr, *, tq):
    qi = pl.program_id(2)
    q0 = qi * tq
    tk = tq
    nc = kcv_ref.shape[0]
    rows = HPG * tq

    q = q_ref[...]
    qs = jnp.concatenate([q[:, h * HEAD_DIM:(h + 1) * HEAD_DIM] for h in range(HPG)], axis=0)
    qpos = q0 + lax.broadcasted_iota(jnp.int32, (tq, 1), 0)
    qpos4 = jnp.concatenate([qpos] * HPG, axis=0)

    kcv = kcv_ref[...]
    kc, vc = kcv[:, :HEAD_DIM], kcv[:, HEAD_DIM:]
    cmp_end = lax.broadcasted_iota(jnp.int32, (1, nc), 1) * CMP_STRIDE + (CMP_BLK - 1)
    p = _masked_softmax(_dot_nt(qs, kc), cmp_end <= qpos4)
    o_cmp = _dot(p, vc)
    psum = p[0:tq]
    for h in range(1, HPG):
        psum = psum + p[h * tq:(h + 1) * tq]
    p_hi, p_lo = _split2(psum)
    ov = ov_ref[...]
    imp = jnp.dot(p_hi, ov, preferred_element_type=F32) + jnp.dot(p_lo, ov, preferred_element_type=F32)

    sel = _top_k_mask(_block_scores(imp, qpos), N_SEL)
    bias = jnp.where(sel, 0.0, SEL_BIAS).astype(BF16)
    q_aug = jnp.concatenate([jnp.concatenate([bias] * HPG, axis=0), qs], axis=1)

    lane_blk = lax.broadcasted_iota(jnp.int32, (1, LANES), 1)
    kcol = lax.broadcasted_iota(jnp.int32, (tk, 1), 0)
    krow = lax.broadcasted_iota(jnp.int32, (1, tk), 1)
    ones_v = jnp.ones((tk, LANES - HEAD_DIM), BF16)
    rb = ROW_BLK
    qpos_rb = [q0 + (r % (tq // rb)) * rb + lax.broadcasted_iota(jnp.int32, (rb, 1), 0) for r in range(rows // rb)]

    def attend_tile(q_lhs, k_rhs, v, k0, mask_kind):
        s_scr[...] = _dot_nt(q_lhs, k_rhs)
        for r in range(rows // rb):
            sl = slice(r * rb, (r + 1) * rb)
            s = s_scr[sl, :]
            if mask_kind == "causal":
                s = jnp.where((k0 + krow) <= qpos_rb[r], s, NEG)
            elif mask_kind == "window":
                s = jnp.where(qpos_rb[r] - (k0 + krow) < WINDOW, s, NEG)
            m_old = m_scr[sl, :]
            m_new = jnp.maximum(m_old, jnp.max(s, axis=-1, keepdims=True))
            a_scr[sl, :] = jnp.exp(m_old - m_new)
            m_scr[sl, :] = m_new
            p_scr[sl, :] = jnp.exp(s - jnp.concatenate([m_new] * (tk // LANES), axis=1)).astype(BF16)
        v_aug = jnp.concatenate([v, ones_v], axis=1)
        acc_scr[...] = a_scr[...] * acc_scr[...] + jnp.dot(p_scr[...], v_aug, preferred_element_type=F32)

    def reset():
        m_scr[...] = jnp.full(m_scr.shape, NEG, F32)
        acc_scr[...] = jnp.zeros(acc_scr.shape, F32)

    def result():
        acc = acc_scr[...]
        return acc[:, :HEAD_DIM] / jnp.maximum(acc[:, HEAD_DIM:HEAD_DIM + 1], 1e-30)

    def key_tile(j):
        k0 = j * tk
        return kvb_ref[pl.ds(pl.multiple_of(k0, tk), tk), :], k0

    def slc_tile(j, mask_kind):
        blk, k0 = key_tile(j)
        onehot = jnp.where(((k0 + kcol) >> 6) == lane_blk, 1.0, 0.0).astype(BF16)
        k_aug = jnp.concatenate([onehot, blk[:, 0:HEAD_DIM]], axis=1)
        attend_tile(q_aug, k_aug, blk[:, HEAD_DIM:2 * HEAD_DIM], k0, mask_kind)

    def win_tile(j, mask_kind):
        blk, k0 = key_tile(j)
        attend_tile(qs, blk[:, 2 * HEAD_DIM:3 * HEAD_DIM], blk[:, 3 * HEAD_DIM:], k0, mask_kind)

    reset()
    slc_tile(qi, "causal")

    def past_tile(j, carry):
        slc_tile(j, None)
        return carry

    lax.fori_loop(0, qi, past_tile, 0)
    o_slc = result()

    reset()
    win_tile(qi, "causal")
    for back in range(1, WINDOW // tk + 1):
        @pl.when(qi >= back)
        def _():
            win_tile(qi - back, "window" if back == WINDOW // tk else None)
    o_win = result()

    gt = gt_ref[...]
    outs = []
    for h in range(HPG):
        r0 = h * tq
        outs.append(gt[:, 3 * h:3 * h + 1] * o_cmp[r0:r0 + tq] + gt[:, 3 * h + 1:3 * h + 2] * o_slc[r0:r0 + tq]
                    + gt[:, 3 * h + 2:3 * h + 3] * o_win[r0:r0 + tq])
    o_ref[...] = jnp.concatenate(outs, axis=1).astype(BF16)


def _nsa_prompt(q, kcv, kvb, gt, ov, *, tq):
    b, t, _ = q.shape
    nc = kcv.shape[2]
    return pl.pallas_call(
        functools.partial(_nsa_prompt_kernel, tq=tq),
        grid=(b, NSA_GROUPS, t // tq),
        in_specs=[pl.BlockSpec((None, tq, GRP_Q), lambda bi, g, i: (bi, i, g)),
                  pl.BlockSpec((None, None, nc, 2 * HEAD_DIM), lambda bi, g, i: (bi, g, 0, 0)),
                  pl.BlockSpec((None, t, 4 * HEAD_DIM), lambda bi, g, i: (bi, 0, g)),
                  pl.BlockSpec((None, tq, LANES), lambda bi, g, i: (bi, i, g)),
                  _const_spec(ov.shape)],
        out_specs=pl.BlockSpec((None, tq, GRP_Q), lambda bi, g, i: (bi, i, g)),
        out_shape=jax.ShapeDtypeStruct((b, t, NSA_WIDTH), BF16),
        scratch_shapes=[pltpu.VMEM((HPG * tq, tq), F32), pltpu.VMEM((HPG * tq, tq), BF16),
                        pltpu.VMEM((HPG * tq, LANES), F32), pltpu.VMEM((HPG * tq, LANES), F32),
                        pltpu.VMEM((HPG * tq, LANES), F32)],
        compiler_params=_cparams(("arbitrary", "arbitrary", "arbitrary")),
        name="nsa_prompt",
    )(q, kcv, kvb, gt, ov)


def _nsa_dec_select_kernel(q_ref, kcv_ref, ov_ref, ocmp_ref, idx_ref, *, qpos):
    nc = kcv_ref.shape[1]
    q = q_ref[...]
    cmp_end = lax.broadcasted_iota(jnp.int32, (1, nc), 1) * CMP_STRIDE + (CMP_BLK - 1)
    mask = cmp_end <= qpos
    ov = ov_ref[...]
    imps = []
    for g in range(NSA_GROUPS):
        kcv = kcv_ref[g]
        p = _masked_softmax(_dot_nt(q[g * HPG:(g + 1) * HPG], kcv[:, :HEAD_DIM]), mask)
        ocmp_ref[g * HPG:(g + 1) * HPG, :] = _dot(p, kcv[:, HEAD_DIM:])
        p_hi, p_lo = _split2(jnp.sum(p, axis=0, keepdims=True))
        imps.append(jnp.dot(p_hi, ov, preferred_element_type=F32) + jnp.dot(p_lo, ov, preferred_element_type=F32))
    imp = jnp.concatenate(imps + [jnp.zeros((8 - NSA_GROUPS, ov.shape[1]), F32)], axis=0)
    score = _block_scores(imp, jnp.full((8, 1), qpos, jnp.int32))
    _, idx = _top_k_mask(score, N_SEL, idx_out=True)
    idx_ref[...] = idx


def _nsa_dec_select(q3, kcv, ov, qpos):
    db, nh, _ = q3.shape
    nc = kcv.shape[2]
    return pl.pallas_call(
        functools.partial(_nsa_dec_select_kernel, qpos=qpos),
        grid=(db,),
        in_specs=[pl.BlockSpec((None, nh, HEAD_DIM), lambda b: (b, 0, 0)),
                  pl.BlockSpec((None, NSA_GROUPS, nc, 2 * HEAD_DIM), lambda b: (b, 0, 0, 0)),
                  _const_spec(ov.shape)],
        out_specs=[pl.BlockSpec((None, nh, HEAD_DIM), lambda b: (b, 0, 0)),
                   pl.BlockSpec((None, 8, LANES), lambda b: (b, 0, 0))],
        out_shape=[jax.ShapeDtypeStruct((db, nh, HEAD_DIM), F32), jax.ShapeDtypeStruct((db, 8, LANES), jnp.int32)],
        compiler_params=_cparams(("arbitrary",)),
        name="nsa_dec_select",
    )(q3, kcv, ov)


def _nsa_dec_attend_kernel(idx_ref, pt_ref, *refs, past):
    k_refs = refs[:N_SEL]
    v_refs = refs[N_SEL:2 * N_SEL]
    (q_ref, nk_ref, nv_ref, nwk_ref, nwv_ref, wk_ref, wv_ref, ocmp_ref, gt_ref, o_ref, wko_ref, wvo_ref) = refs[2 * N_SEL:]
    b = pl.program_id(0)
    g = pl.program_id(1)
    win = wk_ref.shape[1]
    q = q_ref[...]
    lane_blk = lax.broadcasted_iota(jnp.int32, (1, SLC_BLK), 1)

    s_parts, v_parts, pos_parts = [], [], []
    for i in range(N_SEL):
        blk = idx_ref[b, g * N_SEL + i]
        first_half = blk % (PAGE_SIZE // SLC_BLK) == 0
        kpos = blk * SLC_BLK + lane_blk
        from_new = kpos >= past
        k_t, v_t = k_refs[i][...], v_refs[i][...]
        k_t = jnp.where(from_new, nk_ref[...], jnp.where(first_half, k_t[:, :SLC_BLK], k_t[:, SLC_BLK:]))
        v_t = jnp.where(from_new, nv_ref[...], jnp.where(first_half, v_t[:, :SLC_BLK], v_t[:, SLC_BLK:]))
        s_parts.append(_dot(q, k_t))
        v_parts.append(v_t)
        pos_parts.append(kpos)
    p = _masked_softmax(jnp.concatenate(s_parts, axis=1), jnp.concatenate(pos_parts, axis=1) <= past)
    o_slc = _dot_nt(p, jnp.concatenate(v_parts, axis=1))

    lane = lax.broadcasted_iota(jnp.int32, (1, win), 1)
    wk_new = jnp.where(lane == win - 1, nwk_ref[...], pltpu.roll(wk_ref[...], win - 1, 1))
    wv_new = jnp.where(lane == win - 1, nwv_ref[...], pltpu.roll(wv_ref[...], win - 1, 1))
    wko_ref[...] = wk_new
    wvo_ref[...] = wv_new
    p = _masked_softmax(_dot(q, wk_new), (win - 1 - lane) < WINDOW)
    o_win = _dot_nt(p, wv_new)

    gt = gt_ref[...]
    o_ref[...] = gt[:, 0:1] * ocmp_ref[...] + gt[:, 1:2] * o_slc + gt[:, 2:3] * o_win


def _nsa_dec_attend(idx, page_table, pool_k, pool_v, q4, new_cols, win_k, win_v, o_cmp4, gt4, past):
    db, n_pages = page_table.shape
    win = win_k.shape[3]
    blocks_per_page = PAGE_SIZE // SLC_BLK
    assert blocks_per_page == 2

    def blk_map(i):
        def index(b, g, idx_s, pt_s):
            page = jnp.minimum(idx_s[b, g * N_SEL + i] // blocks_per_page, n_pages - 1)
            return pt_s[b, page], g, 0, 0
        return index

    sel_specs = [pl.BlockSpec((None, None, HEAD_DIM, PAGE_SIZE), blk_map(i)) for i in range(N_SEL)]

    def per_bg(shape):
        return pl.BlockSpec((None, None) + shape, lambda b, g, *_: (b, g, 0, 0))

    grid_spec = pltpu.PrefetchScalarGridSpec(
        num_scalar_prefetch=2,
        grid=(db, NSA_GROUPS),
        in_specs=sel_specs + sel_specs + [per_bg((HPG, HEAD_DIM))] + [per_bg((HEAD_DIM, 1))] * 4
        + [per_bg((HEAD_DIM, win))] * 2 + [per_bg((HPG, HEAD_DIM)), per_bg((HPG, 3))],
        out_specs=[per_bg((HPG, HEAD_DIM)), per_bg((HEAD_DIM, win)), per_bg((HEAD_DIM, win))],
    )
    return pl.pallas_call(
        functools.partial(_nsa_dec_attend_kernel, past=past),
        grid_spec=grid_spec,
        out_shape=[jax.ShapeDtypeStruct((db, NSA_GROUPS, HPG, HEAD_DIM), F32),
                   jax.ShapeDtypeStruct(win_k.shape, F32), jax.ShapeDtypeStruct(win_v.shape, F32)],
        compiler_params=_cparams(("arbitrary", "arbitrary")),
        name="nsa_dec_attend",
    )(idx, page_table, *([pool_k] * N_SEL), *([pool_v] * N_SEL), q4, *new_cols, win_k, win_v, o_cmp4, gt4)


def _forget_lower_bound(lb_raw):
    m = jnp.max(lb_raw, axis=0, keepdims=True)
    e = jnp.exp(lb_raw - m)
    return e[0:1] / jnp.sum(e, axis=0, keepdims=True)


def _hgrn_out(o, hg, norm_g):
    o = o * lax.rsqrt(jnp.mean(o * o, axis=-1, keepdims=True) + EPS) * norm_g
    return o * (hg * _sigmoid(hg))


def _hgrn_head_chunk(q, hf, v, hg, lb, norm_g, tri, st, b_s, k_s, v_s):
    lc = q.shape[0]
    f = lb + (1.0 - lb) * _sigmoid(hf)
    k = 1.0 - f
    g_hi, g_mid, g_lo = _split3(jnp.log(f))
    bcum = (jnp.dot(tri, g_hi, preferred_element_type=F32) + jnp.dot(tri, g_mid, preferred_element_type=F32)
            + jnp.dot(tri, g_lo, preferred_element_type=F32))
    b_s[...] = bcum
    k_s[...] = k
    v_s[...] = v

    o_inter = _dot_nt(q * jnp.exp(bcum), st)
    t_idx = lax.broadcasted_iota(jnp.int32, (8, 1), 0)
    outs = []
    for i in range(lc // HG_SUB):
        r0 = i * HG_SUB
        o_i = o_inter[r0:r0 + HG_SUB]
        if i > 0:
            ref_b = bcum[r0:r0 + 1]
            a = _dot_nt(q[r0:r0 + HG_SUB] * jnp.exp(bcum[r0:r0 + HG_SUB] - ref_b), k[:r0] * jnp.exp(ref_b - bcum[:r0]))
            o_i = o_i + _dot(a, v[:r0])
        for half in range(HG_SUB // 8):
            t0 = r0 + 8 * half
            b_t, q_t = bcum[t0:t0 + 8], q[t0:t0 + 8]
            acc = o_i[8 * half:8 * half + 8]
            for s in range(r0, t0 + 8):
                d = b_t - b_s[s:s + 1, :]
                if s >= t0:
                    d = jnp.where(t_idx >= s - t0, d, NEG)
                w = jnp.exp(d) * q_t * k_s[s:s + 1, :]
                acc = acc + jnp.sum(w, axis=-1, keepdims=True) * v_s[s:s + 1, :]
            outs.append(acc)
    o = jnp.concatenate(outs, axis=0)
    b_last = bcum[lc - 1:lc]
    st_new = st * jnp.exp(b_last) + _dot(v.T, k * jnp.exp(b_last - bcum))
    return _hgrn_out(o, hg, norm_g), st_new


def _hgrn_prompt_kernel(hz_ref, lb_ref, ng_ref, tri_ref, o_ref, st_ref, state_t, b_s, k_s, v_s):
    c = pl.program_id(1)
    n_c = pl.num_programs(1)

    @pl.when(c == 0)
    def _():
        state_t[...] = jnp.zeros_like(state_t)

    lb_all = _forget_lower_bound(lb_ref[...])
    tri = tri_ref[...]
    for h in range(HG_HEADS):
        def seg(j):
            return hz_ref[:, j * HG_WIDTH + h * HG_DIM:j * HG_WIDTH + (h + 1) * HG_DIM]

        o, st_new = _hgrn_head_chunk(seg(0), seg(1), seg(2), seg(3), lb_all[:, h * HG_DIM:(h + 1) * HG_DIM],
                                     ng_ref[...], tri, state_t[h], b_s.at[h], k_s.at[h], v_s.at[h])
        o_ref[:, h * HG_DIM:(h + 1) * HG_DIM] = o.astype(BF16)
        state_t[h] = st_new

    @pl.when(c == n_c - 1)
    def _():
        for h in range(HG_HEADS):
            st_ref[h] = state_t[h].T


def _hgrn_prompt(hz3, hg_lb, norm_g, *, lc):
    b, t, w = hz3.shape
    tri = jnp.tril(jnp.ones((lc, lc), F32)).astype(BF16)
    return pl.pallas_call(
        _hgrn_prompt_kernel,
        grid=(b, t // lc),
        in_specs=[pl.BlockSpec((None, lc, w), lambda bi, c: (bi, c, 0)), _const_spec(hg_lb.shape),
                  _const_spec((1, HG_DIM)), _const_spec((lc, lc))],
        out_specs=[pl.BlockSpec((None, lc, HG_WIDTH), lambda bi, c: (bi, c, 0)),
                   pl.BlockSpec((None, HG_HEADS, HG_DIM, HG_DIM), lambda bi, c: (bi, 0, 0, 0))],
        out_shape=[jax.ShapeDtypeStruct((b, t, HG_WIDTH), BF16),
                   jax.ShapeDtypeStruct((b, HG_HEADS, HG_DIM, HG_DIM), F32)],
        scratch_shapes=[pltpu.VMEM((HG_HEADS, HG_DIM, HG_DIM), F32)] + [pltpu.VMEM((HG_HEADS, lc, HG_DIM), F32)] * 3,
        compiler_params=_cparams(("arbitrary", "arbitrary")),
        name="hgrn_prompt",
    )(hz3, hg_lb, norm_g, tri)


def _hgrn_dec_kernel(hz_ref, s0_ref, lb_ref, ng_ref, o_ref, s_ref):
    hz = hz_ref[...]
    lb_all = _forget_lower_bound(lb_ref[...])
    eye = lax.broadcasted_iota(jnp.int32, (HG_DIM, HG_DIM), 0) == lax.broadcasted_iota(jnp.int32, (HG_DIM, HG_DIM), 1)

    def column(row):
        return jnp.sum(jnp.where(eye, row, 0.0), axis=1, keepdims=True)

    outs = []
    for h in range(HG_HEADS):
        def seg(j):
            return hz[:, j * HG_WIDTH + h * HG_DIM:j * HG_WIDTH + (h + 1) * HG_DIM]

        lb = lb_all[:, h * HG_DIM:(h + 1) * HG_DIM]
        f = lb + (1.0 - lb) * _sigmoid(seg(1))
        q, k, v = seg(0), 1.0 - f, seg(2)
        s0 = s0_ref[h]
        s_ref[h] = column(f) * s0 + column(k) * v
        o = jnp.sum(column(q * f) * s0, axis=0, keepdims=True) + jnp.sum(q * k, axis=-1, keepdims=True) * v
        outs.append(_hgrn_out(o, seg(3), ng_ref[...]))
    o_ref[...] = jnp.concatenate(outs, axis=1)


def _hgrn_dec(hz, s0, hg_lb, norm_g):
    db = hz.shape[0]
    return pl.pallas_call(
        _hgrn_dec_kernel,
        grid=(db,),
        in_specs=[pl.BlockSpec((None, 1, hz.shape[1]), lambda b: (b, 0, 0)),
                  pl.BlockSpec((None, HG_HEADS, HG_DIM, HG_DIM), lambda b: (b, 0, 0, 0)),
                  _const_spec(hg_lb.shape), _const_spec((1, HG_DIM))],
        out_specs=[pl.BlockSpec((None, 1, HG_WIDTH), lambda b: (b, 0, 0)),
                   pl.BlockSpec((None, HG_HEADS, HG_DIM, HG_DIM), lambda b: (b, 0, 0, 0))],
        out_shape=[jax.ShapeDtypeStruct((db, 1, HG_WIDTH), F32), jax.ShapeDtypeStruct(s0.shape, F32)],
        compiler_params=_cparams(("arbitrary",)),
        name="hgrn_dec",
    )(hz.reshape(db, 1, -1), s0, hg_lb, norm_g)


def _merge_kernel(on_ref, oh_ref, sm_ref, x_ref, g1_ref, sh_ref, sc_ref, ng_ref, wn_ref, wh_ref, wo_ref,
                  x1_ref, h2_ref):
    d = x_ref.shape[1]
    sm = sm_ref[...]
    mix = sm[:, :d] * _dot(on_ref[...], wn_ref[...]) + sm[:, d:] * _dot(oh_ref[...], wh_ref[...])
    x1 = x_ref[...] + g1_ref[...] * _dot(mix, wo_ref[...])
    x1_ref[...] = x1
    xn = x1 * lax.rsqrt(jnp.mean(x1 * x1, axis=-1, keepdims=True) + EPS) * ng_ref[...]
    h2_ref[...] = (xn * (1.0 + sc_ref[...]) + sh_ref[...]).astype(BF16)


def _merge(o_nsa, o_hg, sm, x2d, g1, sh2, sc2, norm_g, w_br_nsa, w_br_hg, w_out, *, tm, tiles_per_mod):
    r, d = x2d.shape
    rows_mod = g1.shape[1]
    mod_spec = pl.BlockSpec((None, rows_mod, d), lambda i: (i // tiles_per_mod, 0, 0))

    def row_spec(width):
        return pl.BlockSpec((tm, width), lambda i: (i, 0))

    return pl.pallas_call(
        _merge_kernel,
        grid=(r // tm,),
        in_specs=[row_spec(o_nsa.shape[1]), row_spec(o_hg.shape[1]), row_spec(sm.shape[1]), row_spec(d),
                  mod_spec, mod_spec, mod_spec, _const_spec((1, d)),
                  _const_spec(w_br_nsa.shape), _const_spec(w_br_hg.shape), _const_spec(w_out.shape)],
        out_specs=[row_spec(d), row_spec(d)],
        out_shape=[jax.ShapeDtypeStruct((r, d), F32), jax.ShapeDtypeStruct((r, d), BF16)],
        compiler_params=_cparams(("arbitrary",)),
        name="merge",
    )(o_nsa, o_hg, sm, x2d, g1, sh2, sc2, norm_g, w_br_nsa, w_br_hg, w_out)


FF_CHUNKS = 2


def _ffn_core(h2, x1, g2, wu_ref, cw_ref, cb_ref, wd_ref, prev_rows):
    d_ff = wd_ref.shape[0]
    ch = d_ff // FF_CHUNKS
    acc = None
    us = []
    for ci in range(FF_CHUNKS):
        c0, c1 = ci * ch, (ci + 1) * ch
        u = jnp.dot(h2, wu_ref[:, c0:c1], preferred_element_type=F32)
        v = jnp.dot(h2, wu_ref[:, d_ff + c0:d_ff + c1], preferred_element_type=F32)
        u_m1, u_m2 = prev_rows(u, c0, c1)
        y = cb_ref[:, c0:c1] + cw_ref[0:1, c0:c1] * u_m2 + cw_ref[1:2, c0:c1] * u_m1 + cw_ref[2:3, c0:c1] * u
        part = _dot(y * _sigmoid(y) * v, wd_ref[c0:c1, :])
        acc = part if acc is None else acc + part
        us.append(u)
    return x1 + g2 * acc, us


def _ffn_prompt_kernel(h2_ref, x1_ref, g2_ref, wu_ref, cw_ref, cb_ref, wd_ref, y_ref, tail_ref, carry):
    ti = pl.program_id(1)
    tm = h2_ref.shape[0]

    @pl.when(ti == 0)
    def _():
        carry[...] = jnp.zeros_like(carry)

    row = lax.broadcasted_iota(jnp.int32, (tm, 1), 0)

    def prev_rows(u, c0, c1):
        last = carry[:, c0:c1]
        m1 = jnp.where(row == 0, last[7:8], pltpu.roll(u, 1, 0))
        m2 = jnp.where(row == 0, last[6:7], jnp.where(row == 1, last[7:8], pltpu.roll(u, 2, 0)))
        return m1, m2

    out, us = _ffn_core(h2_ref[...], x1_ref[...], g2_ref[...], wu_ref, cw_ref, cb_ref, wd_ref, prev_rows)
    y_ref[...] = out
    ch = us[0].shape[1]
    for ci, u in enumerate(us):
        carry[:, ci * ch:(ci + 1) * ch] = u[tm - 8:, :]
    tail_ref[...] = carry[...]


def _ffn_prompt(h2, x1, g2, w_up, conv_w8, conv_b, w_down, *, b, tm):
    r, d = x1.shape
    d_ff = w_down.shape[0]
    tpb = r // b // tm

    def row_spec(width):
        return pl.BlockSpec((tm, width), lambda bi, i: (bi * tpb + i, 0))

    return pl.pallas_call(
        _ffn_prompt_kernel,
        grid=(b, tpb),
        in_specs=[row_spec(d), row_spec(d), pl.BlockSpec((None, 1, d), lambda bi, i: (bi, 0, 0)),
                  _resident_spec(w_up.shape), _const_spec(conv_w8.shape), _const_spec(conv_b.shape),
                  _resident_spec(w_down.shape)],
        out_specs=[row_spec(d), pl.BlockSpec((None, 8, d_ff), lambda bi, i: (bi, 0, 0))],
        out_shape=[jax.ShapeDtypeStruct((r, d), F32), jax.ShapeDtypeStruct((b, 8, d_ff), F32)],
        scratch_shapes=[pltpu.VMEM((8, d_ff), F32)],
        compiler_params=_cparams(("arbitrary", "arbitrary")),
        name="ffn_prompt",
    )(h2, x1, g2, w_up, conv_w8, conv_b, w_down)


def _ffn_dec_kernel(h2_ref, x1_ref, g2_ref, wu_ref, cw_ref, cb_ref, wd_ref, m1_ref, m2_ref, y_ref, u_ref):
    def prev_rows(u, c0, c1):
        return m1_ref[:, c0:c1], m2_ref[:, c0:c1]

    out, us = _ffn_core(h2_ref[...], x1_ref[...], g2_ref[...], wu_ref, cw_ref, cb_ref, wd_ref, prev_rows)
    y_ref[...] = out
    u_ref[...] = jnp.concatenate(us, axis=1)


def _ffn_dec(h2, x1, g2, w_up, conv_w8, conv_b, w_down, u_m1, u_m2):
    r, d = x1.shape
    d_ff = w_down.shape[0]
    args = (h2, x1, g2, w_up, conv_w8, conv_b, w_down, u_m1, u_m2)
    return pl.pallas_call(
        _ffn_dec_kernel,
        grid=(1,),
        in_specs=[_const_spec(a.shape) for a in args],
        out_specs=[_const_spec((r, d)), _const_spec((r, d_ff))],
        out_shape=[jax.ShapeDtypeStruct((r, d), F32), jax.ShapeDtypeStruct((r, d_ff), F32)],
        compiler_params=_cparams(("arbitrary",)),
        name="ffn_dec",
    )(*args)


def _pack_w_in(w_in):
    cuts = np.cumsum([NSA_WIDTH] + [KV_WIDTH] * 6 + [3 * NSA_HEADS] + [HG_WIDTH] * 4)
    q, kv, gates, rest = w_in[:, :cuts[0]], w_in[:, cuts[0]:cuts[6]], w_in[:, cuts[6]:cuts[7]], w_in[:, cuts[7]:]
    hz, m = rest[:, :4 * HG_WIDTH], rest[:, 4 * HG_WIDTH:]
    per_group = 3 * HPG
    pad = jnp.zeros((w_in.shape[0], LANES - per_group), w_in.dtype)
    gate_cols = []
    for g in range(NSA_GROUPS):
        gate_cols += [gates[:, g * per_group:(g + 1) * per_group], pad]
    return jnp.concatenate([q, kv] + gate_cols + [hz, m], axis=1).astype(BF16)


def _rope_tables(pos):
    half = ROT_DIM // 2
    inv = ROPE_THETA ** (-jnp.arange(0, ROT_DIM, 2, dtype=F32) / ROT_DIM)
    ang = pos.astype(F32)[:, None] * inv[None, :]
    cos, sin = jnp.cos(ang), jnp.sin(ang)
    n = pos.shape[0]
    ones = jnp.ones((n, HEAD_DIM - ROT_DIM), F32)
    zeros = jnp.zeros((n, HEAD_DIM - ROT_DIM), F32)
    z8 = jnp.zeros((n, half), F32)
    c = jnp.concatenate([cos, cos, ones], axis=1)
    s_lo = jnp.concatenate([-sin, z8, zeros], axis=1)
    s_hi = jnp.concatenate([z8, sin, zeros], axis=1)
    return tuple(jnp.concatenate([t, t], axis=1) for t in (c, s_lo, s_hi))


def _cmp_weights(cmp_w1, cmp_w2, cmp_pe):
    n_half = CMP_BLK // CMP_STRIDE
    eye = jnp.eye(NSA_GROUPS, dtype=F32)
    w1s, w2s, pes = [], [], []
    for j in range(2):
        w = cmp_w1[j].reshape(n_half, CMP_STRIDE, HEAD_DIM, HEAD_DIM)
        big = jnp.einsum('rsdh,ge->sgdreh', w, eye)
        w1s.append(big.reshape(CMP_STRIDE * KV_WIDTH, n_half * KV_WIDTH))
        w2s.append(jnp.einsum('dh,ge->gdeh', cmp_w2[j], eye).reshape(KV_WIDTH, KV_WIDTH))
        pe = cmp_pe[j].reshape(n_half, CMP_STRIDE, 1, HEAD_DIM)
        pe = jnp.broadcast_to(pe, (n_half, CMP_STRIDE, NSA_GROUPS, HEAD_DIM)).reshape(n_half, -1)
        pes.append(jnp.concatenate([pe, jnp.zeros((8 - n_half, pe.shape[1]), F32)], axis=0))
    return jnp.stack(w1s).astype(BF16), jnp.stack(w2s).astype(BF16), jnp.stack(pes).astype(BF16)


def _overlap(nc_pad, nc, ns, ns_pad):
    start = np.arange(nc_pad) * CMP_STRIDE
    end = start + CMP_BLK - 1
    s0 = np.arange(ns_pad) * SLC_BLK
    s1 = s0 + SLC_BLK - 1
    m = (start[:, None] <= s1[None, :]) & (end[:, None] >= s0[None, :])
    m &= (np.arange(nc_pad) < nc)[:, None] & (np.arange(ns_pad) < ns)[None, :]
    return jnp.asarray(m, BF16)


def kernel(x_prompt, x_sample, cache_cmp_k, cache_cmp_v, cache_slc_k, cache_slc_v, cache_win_k, cache_win_v,
           state_hgrn, state_ffn_conv, page_table, c_prompt, c_sample, ada_w, ada_b, norm_attn, norm_ffn, w_in,
           q_gain, k_gain, cmp_w1, cmp_w2, cmp_pe, hg_lb, hg_norm, w_br_nsa, w_br_hg, w_out, w_up, conv_w,
           conv_b, w_down):
    depth = ada_w.shape[0]
    assert depth == 1, "single-layer trunk only"
    b, t, d = x_prompt.shape
    db, tn, _ = x_sample.shape
    assert tn == 1
    n_pages = page_table.shape[1]
    past = n_pages * PAGE_SIZE
    d_ff = w_down.shape[1]
    tq = 256
    tm = 256
    lc = 128
    assert t % tq == 0 and t >= WINDOW and WINDOW % tq == 0 and n_pages % PAGES_PER_STEP == 0
    assert cache_win_k.shape[2] == WINDOW

    w_pack = _pack_w_in(w_in[0])
    qg = jnp.tile(q_gain[0], NSA_HEADS).reshape(1, NSA_WIDTH)
    kg = jnp.concatenate([jnp.tile(k_gain[0], (1, NSA_GROUPS)), jnp.zeros((5, KV_WIDTH), F32)], axis=0)
    hd = np.arange(NSA_WIDTH) // HEAD_DIM
    bd = jnp.asarray(hd[:, None] == hd[None, :], BF16)
    w1c, w2c, pe_x = _cmp_weights(cmp_w1[0], cmp_w2[0], cmp_pe[0])
    conv_w8 = jnp.concatenate([conv_w[0], jnp.zeros((8 - CONV_W, d_ff), F32)], axis=0)
    conv_b2 = conv_b[0].reshape(1, d_ff)
    wbn, wbh, wo = w_br_nsa[0].astype(BF16), w_br_hg[0].astype(BF16), w_out[0].astype(BF16)
    wu, wd = w_up[0].astype(BF16), w_down[0].astype(BF16)
    norm_a, norm_f, hg_n = norm_attn[0].reshape(1, d), norm_ffn[0].reshape(1, d), hg_norm[0].reshape(1, HG_DIM)

    n_c = b + db
    c_all = jnp.concatenate([c_prompt, c_sample, jnp.zeros((-n_c % 8, d), F32)], axis=0)
    ada = _ada(c_all, ada_w[0], ada_b[0])
    mods_p = [ada[:b, i * d:(i + 1) * d].reshape(b, 1, d) for i in range(6)]
    mods_s = [ada[b:n_c, i * d:(i + 1) * d].reshape(1, db, d) for i in range(6)]

    tpb = t // tm
    (q, kc_r, vc_r, ks_r, vs_r, kw_r, vw_r, kvb, gt, hz, sm) = _inproj(
        x_prompt.reshape(b * t, d), mods_p[0], mods_p[1], norm_a, w_pack, qg, kg, _rope_tables(jnp.arange(t)), bd,
        tm=tm, tiles_per_mod=tpb, rope_tiles=tpb)

    nsub = t // CMP_STRIDE
    pk = _mm(kc_r.reshape(b * nsub, CMP_STRIDE * KV_WIDTH), w1c[0], 512, "cmp_proj_k").reshape(b, nsub, -1)
    pv = _mm(vc_r.reshape(b * nsub, CMP_STRIDE * KV_WIDTH), w1c[1], 512, "cmp_proj_v").reshape(b, nsub, -1)
    kcv = _cmp_combine(pk, pv, pe_x, w1c, w2c)
    ov = _overlap(nsub, nsub - 1, t // SLC_BLK, LANES)
    o_nsa = _nsa_prompt(q.reshape(b, t, -1), kcv, kvb.reshape(b, t, -1), gt.reshape(b, t, -1), ov, tq=tq)

    o_hg, st_p = _hgrn_prompt(hz.reshape(b, t, -1), hg_lb, hg_n, lc=lc)

    x1, h2 = _merge(o_nsa.reshape(b * t, -1), o_hg.reshape(b * t, -1), sm, x_prompt.reshape(b * t, d),
                    mods_p[2], mods_p[3], mods_p[4], norm_f, wbn, wbh, wo, tm=tm, tiles_per_mod=tpb)
    y_p, tail = _ffn_prompt(h2, x1, mods_p[5], wu, conv_w8, conv_b2, wd, b=b, tm=tm)

    def rows5(a):
        return a.reshape(1, b, t, NSA_GROUPS, HEAD_DIM)

    w = min(WINDOW, t)
    out_p = (rows5(kc_r), rows5(vc_r), rows5(ks_r), rows5(vs_r), rows5(kw_r)[:, :, -w:], rows5(vw_r)[:, :, -w:],
             st_p[None], tail[None, :, 8 - (CONV_W - 1):, :])

    pos_s = jnp.full((db,), past, jnp.int32)
    (q_s, kc_s, vc_s, ks_s, vs_s, kw_s, vw_s, _, gt_s, hz_s, sm_s) = _inproj(
        x_sample.reshape(db, d), mods_s[0], mods_s[1], norm_a, w_pack, qg, kg, _rope_tables(pos_s), bd,
        tm=db, tiles_per_mod=1, rope_tiles=1)

    sub = PAGE_SIZE // CMP_STRIDE
    assert (past + tn) // CMP_STRIDE == past // CMP_STRIDE
    pk_s = _cmp_proj_paged(cache_cmp_k[0].reshape(-1, sub, CMP_STRIDE * KV_WIDTH), page_table, w1c[0], "cmp_proj_pk")
    pv_s = _cmp_proj_paged(cache_cmp_v[0].reshape(-1, sub, CMP_STRIDE * KV_WIDTH), page_table, w1c[1], "cmp_proj_pv")
    kcv_s = _cmp_combine(pk_s, pv_s, pe_x, w1c, w2c)
    nsub_s = past // CMP_STRIDE
    ns_s = -(-(past + tn) // SLC_BLK)
    ns_pad = -(-ns_s // LANES) * LANES
    ov_s = _overlap(nsub_s, nsub_s - 1, ns_s, ns_pad)
    o_cmp_s, idx8 = _nsa_dec_select(q_s.reshape(db, NSA_HEADS, HEAD_DIM), kcv_s, ov_s, past)
    idx = idx8[:, :NSA_GROUPS, :N_SEL].reshape(db, NSA_GROUPS * N_SEL)
    new_cols = [a.reshape(db, NSA_GROUPS, HEAD_DIM, 1) for a in (ks_s, vs_s, kw_s, vw_s)]
    per_group = 3 * HPG
    gt4 = jnp.stack([gt_s[:, g * LANES:g * LANES + per_group] for g in range(NSA_GROUPS)], axis=1)

    def feature_major(a):
        return jnp.transpose(a, (0, 2, 3, 1))

    o_nsa_s, wk_new, wv_new = _nsa_dec_attend(
        idx, page_table, feature_major(cache_slc_k[0]), feature_major(cache_slc_v[0]),
        q_s.astype(F32).reshape(db, NSA_GROUPS, HPG, HEAD_DIM), new_cols,
        feature_major(cache_win_k[0]), feature_major(cache_win_v[0]),
        o_cmp_s.reshape(db, NSA_GROUPS, HPG, HEAD_DIM), gt4.reshape(db, NSA_GROUPS, HPG, 3), past)
    wk_new, wv_new = (jnp.transpose(a, (0, 3, 1, 2)) for a in (wk_new, wv_new))

    o_hg_s, st_s = _hgrn_dec(hz_s, state_hgrn[0], hg_lb, hg_n)
    x1_s, h2_s = _merge(o_nsa_s.reshape(db, NSA_WIDTH), o_hg_s.reshape(db, HG_WIDTH), sm_s, x_sample.reshape(db, d),
                        mods_s[2], mods_s[3], mods_s[4], norm_f, wbn, wbh, wo, tm=db, tiles_per_mod=1)
    buf = state_ffn_conv[0]
    y_s, u_s = _ffn_dec(h2_s, x1_s, mods_s[5].reshape(db, d), wu, conv_w8, conv_b2, wd, buf[:, 1], buf[:, 0])
    conv_s = jnp.stack([buf[:, 1], u_s], axis=1)

    def rows5s(a):
        return a.reshape(1, db, tn, NSA_GROUPS, HEAD_DIM)

    out_s = (rows5s(kc_s), rows5s(vc_s), rows5s(ks_s), rows5s(vs_s),
             wk_new.reshape(1, db, WINDOW, NSA_GROUPS, HEAD_DIM), wv_new.reshape(1, db, WINDOW, NSA_GROUPS, HEAD_DIM),
             st_s[None], conv_s[None])
    return (y_p.reshape(b, t, d), y_s.reshape(db, tn, d)) + out_p + out_s
```

```python
import functools

import numpy as np
import jax
import jax.numpy as jnp
from jax import lax
from jax.experimental import pallas as pl
from jax.experimental.pallas import tpu as pltpu

F32 = jnp.float32
BF16 = jnp.bfloat16

HEAD_DIM = 64
NSA_HEADS = 8
NSA_GROUPS = 2
HPG = NSA_HEADS // NSA_GROUPS
ROT_DIM = HEAD_DIM // 4
ROPE_THETA = 500000.0
CMP_BLK = 32
CMP_STRIDE = 16
SLC_BLK = 64
N_SEL = 16
WINDOW = 512
PAGE_SIZE = 128
HG_HEADS = 4
HG_DIM = 128
HG_SUB = 16
ROW_BLK = 128
CONV_W = 3
EPS = 1e-6
NEG = -1e30
SEL_BIAS = -1e9
LANES = 128
VMEM_LIMIT = 56 * 1024 * 1024

NSA_WIDTH = NSA_HEADS * HEAD_DIM
KV_WIDTH = NSA_GROUPS * HEAD_DIM
HG_WIDTH = HG_HEADS * HG_DIM
GRP_Q = HPG * HEAD_DIM

SEG_Q = 0
SEG_KV = SEG_Q + NSA_WIDTH
SEG_G = SEG_KV + 6 * KV_WIDTH
SEG_H = SEG_G + NSA_GROUPS * LANES
SEG_M = SEG_H + 4 * HG_WIDTH


def _cparams(sem):
    return pltpu.CompilerParams(dimension_semantics=sem, vmem_limit_bytes=VMEM_LIMIT)


def _dot(a, b):
    return jnp.dot(a.astype(BF16), b.astype(BF16), preferred_element_type=F32)


def _dot_nt(a, b):
    return lax.dot_general(a.astype(BF16), b.astype(BF16), (((1,), (1,)), ((), ())), preferred_element_type=F32)


def _split2(x):
    hi = x.astype(BF16)
    lo = (x - hi.astype(F32)).astype(BF16)
    return hi, lo


def _split3(x):
    hi = x.astype(BF16)
    r = x - hi.astype(F32)
    mid = r.astype(BF16)
    lo = (r - mid.astype(F32)).astype(BF16)
    return hi, mid, lo


def _dot_x3(a, b):
    ah, al = _split2(a)
    bh, bl = _split2(b)
    return (jnp.dot(ah, bh, preferred_element_type=F32) + jnp.dot(al, bh, preferred_element_type=F32)
            + jnp.dot(ah, bl, preferred_element_type=F32))


def _sigmoid(x):
    return 1.0 / (1.0 + jnp.exp(-x))


def _const_spec(shape):
    nd = len(shape)
    return pl.BlockSpec(shape, lambda *_: (0,) * nd)


def _resident_spec(shape):
    nd = len(shape)
    return pl.BlockSpec(shape, lambda *_: (0,) * nd, pipeline_mode=pl.Buffered(1))


def _ada_kernel(c_ref, w_ref, b_ref, o_ref):
    o_ref[...] = _dot_x3(c_ref[...], w_ref[...]) + b_ref[...]


def _ada(c_all, w, b):
    r, d = c_all.shape
    n = w.shape[1]
    tn = 1536
    return pl.pallas_call(
        _ada_kernel,
        grid=(n // tn,),
        in_specs=[_const_spec((r, d)), pl.BlockSpec((d, tn), lambda j: (0, j)), pl.BlockSpec((1, tn), lambda j: (0, j))],
        out_specs=pl.BlockSpec((r, tn), lambda j: (0, j)),
        out_shape=jax.ShapeDtypeStruct((r, n), F32),
        compiler_params=_cparams(("arbitrary",)),
        name="ada",
    )(c_all, w, b.reshape(1, n))


def _rope(x, c, s_lo, s_hi):
    width = x.shape[1]
    reps = width // LANES
    if reps > 1:
        c = jnp.concatenate([c] * reps, axis=1)
        s_lo = jnp.concatenate([s_lo] * reps, axis=1)
        s_hi = jnp.concatenate([s_hi] * reps, axis=1)
    half = ROT_DIM // 2
    return x * c + pltpu.roll(x, half, 1) * s_hi + pltpu.roll(x, width - half, 1) * s_lo


def _head_rms(z, bd):
    hi, lo = _split2(z * z)
    ss = jnp.dot(hi, bd, preferred_element_type=F32) + jnp.dot(lo, bd, preferred_element_type=F32)
    return z * lax.rsqrt(ss * (1.0 / HEAD_DIM) + EPS)


def _inproj_kernel(x_ref, sh_ref, sc_ref, ng_ref, w_ref, qg_ref, kg_ref, rc_ref, rlo_ref, rhi_ref, bd_ref,
                   q_ref, kc_ref, vc_ref, ks_ref, vs_ref, kw_ref, vw_ref, kcb_ref, vcb_ref, kvb_ref, gt_ref, hz_ref,
                   sm_ref):
    x = x_ref[...]
    xn = x * lax.rsqrt(jnp.mean(x * x, axis=-1, keepdims=True) + EPS) * ng_ref[...]
    h = (xn * (1.0 + sc_ref[...]) + sh_ref[...]).astype(BF16)
    rc, rlo, rhi = rc_ref[...], rlo_ref[...], rhi_ref[...]
    bd = bd_ref[...]

    zq = jnp.dot(h, w_ref[:, SEG_Q:SEG_KV], preferred_element_type=F32)
    q = _rope(_head_rms(zq, bd) * qg_ref[...], rc, rlo, rhi)
    q_ref[...] = (q * (HEAD_DIM ** -0.5)).astype(BF16)

    zkv = jnp.dot(h, w_ref[:, SEG_KV:SEG_G], preferred_element_type=F32)
    bd1 = bd[:LANES, :LANES]
    k_refs = (kc_ref, ks_ref, kw_ref)
    v_refs = (vc_ref, vs_ref, vw_ref)
    rows = []
    for j in range(3):
        zk = zkv[:, 2 * j * KV_WIDTH:(2 * j + 1) * KV_WIDTH]
        k = _rope(_head_rms(zk, bd1) * kg_ref[j:j + 1, :], rc, rlo, rhi)
        v = zkv[:, (2 * j + 1) * KV_WIDTH:(2 * j + 2) * KV_WIDTH]
        for g in range(NSA_GROUPS):
            k_refs[j][:, g, :] = k[:, g * HEAD_DIM:(g + 1) * HEAD_DIM]
            v_refs[j][:, g, :] = v[:, g * HEAD_DIM:(g + 1) * HEAD_DIM]
        rows.append((k, v))
    kcb_ref[...] = rows[0][0].astype(BF16)
    vcb_ref[...] = rows[0][1].astype(BF16)
    parts = []
    for g in range(NSA_GROUPS):
        for k, v in rows[1:]:
            parts.append(k[:, g * HEAD_DIM:(g + 1) * HEAD_DIM])
            parts.append(v[:, g * HEAD_DIM:(g + 1) * HEAD_DIM])
    kvb_ref[...] = jnp.concatenate(parts, axis=1).astype(BF16)

    gt_ref[...] = _sigmoid(jnp.dot(h, w_ref[:, SEG_G:SEG_H], preferred_element_type=F32))
    hz_ref[...] = jnp.dot(h, w_ref[:, SEG_H:SEG_M], preferred_element_type=F32)
    sm_ref[...] = _sigmoid(jnp.dot(h, w_ref[:, SEG_M:], preferred_element_type=F32))


def _inproj(x2d, shift, scale, norm_g, w_pack, qg, kg, rope_tabs, bd, *, tm, tiles_per_mod, rope_tiles):
    r, d = x2d.shape
    n_tiles = r // tm
    rows_mod = shift.shape[1]
    mod_spec = pl.BlockSpec((None, rows_mod, d), lambda i: (i // tiles_per_mod, 0, 0))
    rope_spec = pl.BlockSpec((tm, LANES), lambda i: (i % rope_tiles, 0))

    def row_spec(width):
        return pl.BlockSpec((tm, width), lambda i: (i, 0))

    widths = (KV_WIDTH, KV_WIDTH, 4 * KV_WIDTH, NSA_GROUPS * LANES, 4 * HG_WIDTH, w_pack.shape[1] - SEG_M)
    dtypes = (BF16, BF16, BF16, F32, F32, F32)
    cache_spec = pl.BlockSpec((tm, NSA_GROUPS, HEAD_DIM), lambda i: (i, 0, 0))
    cache_shape = jax.ShapeDtypeStruct((r, NSA_GROUPS, HEAD_DIM), F32)
    return pl.pallas_call(
        _inproj_kernel,
        grid=(n_tiles,),
        in_specs=[row_spec(d), mod_spec, mod_spec, _const_spec((1, d)), _resident_spec(w_pack.shape),
                  _const_spec(qg.shape), _const_spec(kg.shape), rope_spec, rope_spec, rope_spec, _const_spec(bd.shape)],
        out_specs=[row_spec(NSA_WIDTH)] + [cache_spec] * 6 + [row_spec(w) for w in widths],
        out_shape=[jax.ShapeDtypeStruct((r, NSA_WIDTH), BF16)] + [cache_shape] * 6
        + [jax.ShapeDtypeStruct((r, w), dt) for w, dt in zip(widths, dtypes)],
        compiler_params=_cparams(("arbitrary",)),
        name="inproj",
    )(x2d, shift, scale, norm_g, w_pack, qg, kg, *rope_tabs, bd)


def _mm_kernel(a_ref, w_ref, o_ref):
    o_ref[...] = _dot(a_ref[...], w_ref[...])


def _mm(a, w, tm, name):
    m, k = a.shape
    n = w.shape[1]
    tm = min(tm, m)
    return pl.pallas_call(
        _mm_kernel,
        grid=(m // tm,),
        in_specs=[pl.BlockSpec((tm, k), lambda i: (i, 0)), _const_spec((k, n))],
        out_specs=pl.BlockSpec((tm, n), lambda i: (i, 0)),
        out_shape=jax.ShapeDtypeStruct((m, n), F32),
        compiler_params=_cparams(("arbitrary",)),
        name=name,
    )(a, w)


PAGES_PER_STEP = 16


def _cmp_proj_paged_kernel(pt_ref, *refs):
    page_refs = refs[:PAGES_PER_STEP]
    perm_ref, w_ref, o_ref, rows_ref = refs[PAGES_PER_STEP:]
    sub = PAGE_SIZE // CMP_STRIDE
    perm = perm_ref[...]
    for j, pr in enumerate(page_refs):
        rows = _dot_nt(perm, pr[...])
        for s in range(CMP_STRIDE):
            rows_ref[s, j * sub:(j + 1) * sub, :] = rows[s * sub:(s + 1) * sub, :]
    acc = None
    for s in range(CMP_STRIDE):
        part = _dot(rows_ref[s], w_ref[s * KV_WIDTH:(s + 1) * KV_WIDTH, :])
        acc = part if acc is None else acc + part
    o_ref[...] = acc


def _cmp_proj_paged(pool_fm, page_table, w, name):
    db, n_pages = page_table.shape
    sub = PAGE_SIZE // CMP_STRIDE
    n = w.shape[1]
    steps = n_pages // PAGES_PER_STEP
    pos = np.arange(PAGE_SIZE)
    perm = jnp.asarray((pos[:, None] % sub) * CMP_STRIDE + pos[:, None] // sub == pos[None, :], BF16)
    page_specs = [pl.BlockSpec((None, KV_WIDTH, PAGE_SIZE), lambda b, c, pt, j=j: (pt[b, c * PAGES_PER_STEP + j], 0, 0))
                  for j in range(PAGES_PER_STEP)]
    grid_spec = pltpu.PrefetchScalarGridSpec(
        num_scalar_prefetch=1,
        grid=(db, steps),
        in_specs=page_specs + [pl.BlockSpec(perm.shape, lambda b, c, pt: (0, 0)),
                               pl.BlockSpec(w.shape, lambda b, c, pt: (0, 0))],
        out_specs=pl.BlockSpec((None, PAGES_PER_STEP * sub, n), lambda b, c, pt: (b, c, 0)),
        scratch_shapes=[pltpu.VMEM((CMP_STRIDE, PAGES_PER_STEP * sub, KV_WIDTH), F32)],
    )
    return pl.pallas_call(
        _cmp_proj_paged_kernel,
        grid_spec=grid_spec,
        out_shape=jax.ShapeDtypeStruct((db, n_pages * sub, n), F32),
        compiler_params=_cparams(("arbitrary", "arbitrary")),
        name=name,
    )(page_table, *([pool_fm] * PAGES_PER_STEP), perm, w)


def _cmp_combine_kernel(pk_ref, pv_ref, pe_ref, w1_ref, w2_ref, o_ref):
    n = pk_ref.shape[0]
    outs = []
    for j, p_ref in enumerate((pk_ref, pv_ref)):
        pe = jnp.dot(pe_ref[j], w1_ref[j], preferred_element_type=F32)
        pe_hid = pe[0:1, :KV_WIDTH] + pe[1:2, KV_WIDTH:]
        p = p_ref[...]
        nxt = pltpu.roll(p[:, KV_WIDTH:], n - 1, 0)
        hid = pe_hid + p[:, :KV_WIDTH] + nxt
        outs.append(_dot(jax.nn.gelu(hid), w2_ref[j]))
    for g in range(NSA_GROUPS):
        o_ref[g] = jnp.concatenate([o[:, g * HEAD_DIM:(g + 1) * HEAD_DIM] for o in outs], axis=1).astype(BF16)


def _cmp_combine(pk, pv, pe_x, w1, w2bd):
    bx, n, c = pk.shape
    return pl.pallas_call(
        _cmp_combine_kernel,
        grid=(bx,),
        in_specs=[pl.BlockSpec((None, n, c), lambda b: (b, 0, 0)), pl.BlockSpec((None, n, c), lambda b: (b, 0, 0)),
                  _const_spec(pe_x.shape), _const_spec(w1.shape), _const_spec(w2bd.shape)],
        out_specs=pl.BlockSpec((None, NSA_GROUPS, n, 2 * HEAD_DIM), lambda b: (b, 0, 0, 0)),
        out_shape=jax.ShapeDtypeStruct((bx, NSA_GROUPS, n, 2 * HEAD_DIM), BF16),
        compiler_params=_cparams(("arbitrary",)),
        name="cmp_combine",
    )(pk, pv, pe_x, w1, w2bd)


def _masked_softmax(s, mask):
    s = jnp.where(mask, s, NEG)
    m = jnp.max(s, axis=-1, keepdims=True)
    e = jnp.where(mask, jnp.exp(s - m), 0.0)
    return e / jnp.maximum(jnp.sum(e, axis=-1, keepdims=True), 1e-30)


def _top_k_mask(score, k, axis, idx_out=False):
    n = score.shape[axis]
    pos = lax.broadcasted_iota(jnp.int32, score.shape, axis).astype(F32)
    sel = jnp.zeros(score.shape, jnp.bool_)
    idx = jnp.zeros((score.shape[0], LANES), jnp.int32)
    lane_out = lax.broadcasted_iota(jnp.int32, (score.shape[0], LANES), 1)
    work = score
    for it in range(k):
        mx = jnp.max(work, axis=axis, keepdims=True)
        first = jnp.min(jnp.where(work == mx, pos, float(n)), axis=axis, keepdims=True)
        pick = pos == first
        sel = jnp.logical_or(sel, pick)
        work = jnp.where(pick, -jnp.inf, work)
        if idx_out:
            idx = jnp.where(lane_out == it, first.astype(jnp.int32), idx)
    return (sel, idx) if idx_out else sel


def _block_scores(imp, qpos, axis):
    blk = lax.broadcasted_iota(jnp.int32, imp.shape, axis)
    cur = qpos >> 6
    valid = (blk << 6) <= qpos
    forced = (blk == 0) | (blk == cur) | (blk == cur - 1)
    return jnp.where(forced, 1e9, jnp.where(valid, imp, -1e9))


def _nsa_prompt_kernel(q_ref, kcv_ref, kvb_ref, gt_ref, ov_ref, o_ref, s0_scr, s1_scr, p0_scr, p1_scr, a0_scr, a1_scr,
                       m_scr, acc_scr, *, tq):
    qi = pl.program_id(2)
    q0 = qi * tq
    tk = tq
    nc = kcv_ref.shape[0]
    rows = HPG * tq

    q = q_ref[...]
    qs = jnp.concatenate([q[:, h * HEAD_DIM:(h + 1) * HEAD_DIM] for h in range(HPG)], axis=0)
    qpos = q0 + lax.broadcasted_iota(jnp.int32, (tq, 1), 0)
    qpos4 = jnp.concatenate([qpos] * HPG, axis=0)

    kcv = kcv_ref[...]
    kc, vc = kcv[:, :HEAD_DIM], kcv[:, HEAD_DIM:]
    cmp_end = lax.broadcasted_iota(jnp.int32, (1, nc), 1) * CMP_STRIDE + (CMP_BLK - 1)
    p = _masked_softmax(_dot_nt(qs, kc), cmp_end <= qpos4)
    o_cmp = _dot(p, vc)
    psum = p[0:tq]
    for h in range(1, HPG):
        psum = psum + p[h * tq:(h + 1) * tq]
    p_hi, p_lo = _split2(psum)
    ov_t = ov_ref[...]
    imp_t = _dot_nt(ov_t, p_hi) + _dot_nt(ov_t, p_lo)

    qpos_row = q0 + lax.broadcasted_iota(jnp.int32, (1, tq), 1)
    sel_t = _top_k_mask(_block_scores(imp_t, qpos_row, 0), N_SEL, 0)
    bias = jnp.where(sel_t, 0.0, SEL_BIAS).T.astype(BF16)
    q_aug = jnp.concatenate([jnp.concatenate([bias] * HPG, axis=0), qs], axis=1)

    lane_blk = lax.broadcasted_iota(jnp.int32, (1, LANES), 1)
    kcol = lax.broadcasted_iota(jnp.int32, (tk, 1), 0)
    krow = lax.broadcasted_iota(jnp.int32, (1, tk), 1)
    ones_v = jnp.ones((tk, LANES - HEAD_DIM), BF16)
    rb = ROW_BLK
    qpos_rb = [q0 + (r % (tq // rb)) * rb + lax.broadcasted_iota(jnp.int32, (rb, 1), 0) for r in range(rows // rb)]

    half = rows // 2
    s_bufs, p_bufs, a_bufs = (s0_scr, s1_scr), (p0_scr, p1_scr), (a0_scr, a1_scr)

    def key_tile(j):
        return kvb_ref[pl.ds(pl.multiple_of(j * tk, tk), tk), :]

    def scores(slot, j, slc):
        blk = key_tile(j)
        if slc:
            onehot = jnp.where(((j * tk + kcol) >> 6) == lane_blk, 1.0, 0.0).astype(BF16)
            q_lhs, k_rhs = q_aug, jnp.concatenate([onehot, blk[:, 0:HEAD_DIM]], axis=1)
        else:
            q_lhs, k_rhs = qs, blk[:, 2 * HEAD_DIM:3 * HEAD_DIM]
        for h0 in (0, half):
            s_bufs[slot][h0:h0 + half, :] = _dot_nt(q_lhs[h0:h0 + half], k_rhs)

    def softmax(slot, j, mask_kind, live=None):
        k0 = j * tk
        for r in range(rows // rb):
            sl = slice(r * rb, (r + 1) * rb)
            s = s_bufs[slot][sl, :]
            keep = None
            if mask_kind == "causal":
                keep = (k0 + krow) <= qpos_rb[r]
            elif mask_kind == "window":
                keep = qpos_rb[r] - (k0 + krow) < WINDOW
            if live is not None:
                keep = live if keep is None else jnp.logical_and(keep, live)
            if keep is not None:
                s = jnp.where(keep, s, NEG)
            m_old = m_scr[sl, :]
            m_new = jnp.maximum(m_old, jnp.max(s, axis=-1, keepdims=True))
            a_bufs[slot][sl, :] = jnp.exp(m_old - m_new)
            m_scr[sl, :] = m_new
            p_bufs[slot][sl, :] = jnp.exp(s - jnp.concatenate([m_new] * (tk // LANES), axis=1)).astype(BF16)

    def accumulate(slot, j, v_lane0):
        v_aug = jnp.concatenate([key_tile(j)[:, v_lane0:v_lane0 + HEAD_DIM], ones_v], axis=1)
        for h0 in (0, half):
            sl = slice(h0, h0 + half)
            acc_scr[sl, :] = a_bufs[slot][sl, :] * acc_scr[sl, :] + jnp.dot(p_bufs[slot][sl, :], v_aug,
                                                                           preferred_element_type=F32)

    def reset():
        m_scr[...] = jnp.full(m_scr.shape, NEG, F32)
        acc_scr[...] = jnp.zeros(acc_scr.shape, F32)

    def result():
        acc = acc_scr[...]
        return acc[:, :HEAD_DIM] / jnp.maximum(acc[:, HEAD_DIM:HEAD_DIM + 1], 1e-30)

    reset()
    scores(0, qi, True)
    scores(1, 0, True)
    softmax(0, qi, "causal")

    def past_pair(u, carry):
        ta, tb = 2 * u, 2 * u + 1
        scores(0, jnp.minimum(tb, qi), True)
        softmax(1, ta, None)
        accumulate(0, jnp.where(u == 0, qi, ta - 1), HEAD_DIM)
        scores(1, jnp.minimum(tb + 1, qi), True)
        softmax(0, tb, None, live=tb < qi)
        accumulate(1, ta, HEAD_DIM)
        return carry

    n_pairs = (qi + 1) // 2
    lax.fori_loop(0, n_pairs, past_pair, 0)
    accumulate(0, jnp.where(n_pairs == 0, qi, jnp.minimum(2 * n_pairs - 1, qi)), HEAD_DIM)
    o_slc = result()

    reset()
    n_back = WINDOW // tk
    tiles = [(qi, "causal", None)] + [(jnp.maximum(qi - back, 0), "window" if back == n_back else None, qi >= back)
                                      for back in range(1, n_back + 1)]
    scores(0, tiles[0][0], False)
    for i, (j, mask_kind, live) in enumerate(tiles):
        if i + 1 < len(tiles):
            scores((i + 1) % 2, tiles[i + 1][0], False)
        softmax(i % 2, j, mask_kind, live)
        accumulate(i % 2, j, 3 * HEAD_DIM)
    o_win = result()

    gt = gt_ref[...]
    outs = []
    for h in range(HPG):
        r0 = h * tq
        outs.append(gt[:, 3 * h:3 * h + 1] * o_cmp[r0:r0 + tq] + gt[:, 3 * h + 1:3 * h + 2] * o_slc[r0:r0 + tq]
                    + gt[:, 3 * h + 2:3 * h + 3] * o_win[r0:r0 + tq])
    o_ref[...] = jnp.concatenate(outs, axis=1).astype(BF16)


def _nsa_prompt(q, kcv, kvb, gt, ov, *, tq):
    b, t, _ = q.shape
    nc = kcv.shape[2]
    return pl.pallas_call(
        functools.partial(_nsa_prompt_kernel, tq=tq),
        grid=(b, NSA_GROUPS, t // tq),
        in_specs=[pl.BlockSpec((None, tq, GRP_Q), lambda bi, g, i: (bi, i, g)),
                  pl.BlockSpec((None, None, nc, 2 * HEAD_DIM), lambda bi, g, i: (bi, g, 0, 0)),
                  pl.BlockSpec((None, t, 4 * HEAD_DIM), lambda bi, g, i: (bi, 0, g)),
                  pl.BlockSpec((None, tq, LANES), lambda bi, g, i: (bi, i, g)),
                  _const_spec(ov.shape)],
        out_specs=pl.BlockSpec((None, tq, GRP_Q), lambda bi, g, i: (bi, i, g)),
        out_shape=jax.ShapeDtypeStruct((b, t, NSA_WIDTH), BF16),
        scratch_shapes=[pltpu.VMEM((HPG * tq, tq), F32)] * 2 + [pltpu.VMEM((HPG * tq, tq), BF16)] * 2
        + [pltpu.VMEM((HPG * tq, LANES), F32)] * 4,
        compiler_params=_cparams(("arbitrary", "arbitrary", "arbitrary")),
        name="nsa_prompt",
    )(q, kcv, kvb, gt, ov)


def _nsa_dec_select_kernel(q_ref, kcv_ref, ov_ref, ocmp_ref, idx_ref, *, qpos):
    nc = kcv_ref.shape[1]
    q = q_ref[...]
    cmp_end = lax.broadcasted_iota(jnp.int32, (1, nc), 1) * CMP_STRIDE + (CMP_BLK - 1)
    mask = cmp_end <= qpos
    ov = ov_ref[...]
    imps = []
    for g in range(NSA_GROUPS):
        kcv = kcv_ref[g]
        p = _masked_softmax(_dot_nt(q[g * HPG:(g + 1) * HPG], kcv[:, :HEAD_DIM]), mask)
        ocmp_ref[g * HPG:(g + 1) * HPG, :] = _dot(p, kcv[:, HEAD_DIM:])
        p_hi, p_lo = _split2(jnp.sum(p, axis=0, keepdims=True))
        imps.append(jnp.dot(p_hi, ov, preferred_element_type=F32) + jnp.dot(p_lo, ov, preferred_element_type=F32))
    imp = jnp.concatenate(imps + [jnp.zeros((8 - NSA_GROUPS, ov.shape[1]), F32)], axis=0)
    score = _block_scores(imp, jnp.full((8, 1), qpos, jnp.int32), 1)
    _, idx = _top_k_mask(score, N_SEL, 1, idx_out=True)
    idx_ref[...] = idx


def _nsa_dec_select(q3, kcv, ov, qpos):
    db, nh, _ = q3.shape
    nc = kcv.shape[2]
    return pl.pallas_call(
        functools.partial(_nsa_dec_select_kernel, qpos=qpos),
        grid=(db,),
        in_specs=[pl.BlockSpec((None, nh, HEAD_DIM), lambda b: (b, 0, 0)),
                  pl.BlockSpec((None, NSA_GROUPS, nc, 2 * HEAD_DIM), lambda b: (b, 0, 0, 0)),
                  _const_spec(ov.shape)],
        out_specs=[pl.BlockSpec((None, nh, HEAD_DIM), lambda b: (b, 0, 0)),
                   pl.BlockSpec((None, 8, LANES), lambda b: (b, 0, 0))],
        out_shape=[jax.ShapeDtypeStruct((db, nh, HEAD_DIM), F32), jax.ShapeDtypeStruct((db, 8, LANES), jnp.int32)],
        compiler_params=_cparams(("arbitrary",)),
        name="nsa_dec_select",
    )(q3, kcv, ov)


def _nsa_dec_attend_kernel(idx_ref, pt_ref, *refs, past):
    k_refs = refs[:N_SEL]
    v_refs = refs[N_SEL:2 * N_SEL]
    (q_ref, nk_ref, nv_ref, nwk_ref, nwv_ref, wk_ref, wv_ref, ocmp_ref, gt_ref, o_ref, wko_ref, wvo_ref) = refs[2 * N_SEL:]
    b = pl.program_id(0)
    g = pl.program_id(1)
    win = wk_ref.shape[1]
    q = q_ref[...]
    lane_blk = lax.broadcasted_iota(jnp.int32, (1, SLC_BLK), 1)

    s_parts, v_parts, pos_parts = [], [], []
    for i in range(N_SEL):
        blk = idx_ref[b, g * N_SEL + i]
        first_half = blk % (PAGE_SIZE // SLC_BLK) == 0
        kpos = blk * SLC_BLK + lane_blk
        from_new = kpos >= past
        k_t, v_t = k_refs[i][...], v_refs[i][...]
        k_t = jnp.where(from_new, nk_ref[...], jnp.where(first_half, k_t[:, :SLC_BLK], k_t[:, SLC_BLK:]))
        v_t = jnp.where(from_new, nv_ref[...], jnp.where(first_half, v_t[:, :SLC_BLK], v_t[:, SLC_BLK:]))
        s_parts.append(_dot(q, k_t))
        v_parts.append(v_t)
        pos_parts.append(kpos)
    p = _masked_softmax(jnp.concatenate(s_parts, axis=1), jnp.concatenate(pos_parts, axis=1) <= past)
    o_slc = _dot_nt(p, jnp.concatenate(v_parts, axis=1))

    lane = lax.broadcasted_iota(jnp.int32, (1, win), 1)
    wk_new = jnp.where(lane == win - 1, nwk_ref[...], pltpu.roll(wk_ref[...], win - 1, 1))
    wv_new = jnp.where(lane == win - 1, nwv_ref[...], pltpu.roll(wv_ref[...], win - 1, 1))
    wko_ref[...] = wk_new
    wvo_ref[...] = wv_new
    p = _masked_softmax(_dot(q, wk_new), (win - 1 - lane) < WINDOW)
    o_win = _dot_nt(p, wv_new)

    gt = gt_ref[...]
    o_ref[...] = gt[:, 0:1] * ocmp_ref[...] + gt[:, 1:2] * o_slc + gt[:, 2:3] * o_win


def _nsa_dec_attend(idx, page_table, pool_k, pool_v, q4, new_cols, win_k, win_v, o_cmp4, gt4, past):
    db, n_pages = page_table.shape
    win = win_k.shape[3]
    blocks_per_page = PAGE_SIZE // SLC_BLK
    assert blocks_per_page == 2

    def blk_map(i):
        def index(b, g, idx_s, pt_s):
            page = jnp.minimum(idx_s[b, g * N_SEL + i] // blocks_per_page, n_pages - 1)
            return pt_s[b, page], g, 0, 0
        return index

    sel_specs = [pl.BlockSpec((None, None, HEAD_DIM, PAGE_SIZE), blk_map(i)) for i in range(N_SEL)]

    def per_bg(shape):
        return pl.BlockSpec((None, None) + shape, lambda b, g, *_: (b, g, 0, 0))

    grid_spec = pltpu.PrefetchScalarGridSpec(
        num_scalar_prefetch=2,
        grid=(db, NSA_GROUPS),
        in_specs=sel_specs + sel_specs + [per_bg((HPG, HEAD_DIM))] + [per_bg((HEAD_DIM, 1))] * 4
        + [per_bg((HEAD_DIM, win))] * 2 + [per_bg((HPG, HEAD_DIM)), per_bg((HPG, 3))],
        out_specs=[per_bg((HPG, HEAD_DIM)), per_bg((HEAD_DIM, win)), per_bg((HEAD_DIM, win))],
    )
    return pl.pallas_call(
        functools.partial(_nsa_dec_attend_kernel, past=past),
        grid_spec=grid_spec,
        out_shape=[jax.ShapeDtypeStruct((db, NSA_GROUPS, HPG, HEAD_DIM), F32),
                   jax.ShapeDtypeStruct(win_k.shape, F32), jax.ShapeDtypeStruct(win_v.shape, F32)],
        compiler_params=_cparams(("arbitrary", "arbitrary")),
        name="nsa_dec_attend",
    )(idx, page_table, *([pool_k] * N_SEL), *([pool_v] * N_SEL), q4, *new_cols, win_k, win_v, o_cmp4, gt4)


def _forget_lower_bound(lb_raw):
    m = jnp.max(lb_raw, axis=0, keepdims=True)
    e = jnp.exp(lb_raw - m)
    return e[0:1] / jnp.sum(e, axis=0, keepdims=True)


def _hgrn_out(o, hg, norm_g):
    o = o * lax.rsqrt(jnp.mean(o * o, axis=-1, keepdims=True) + EPS) * norm_g
    return o * (hg * _sigmoid(hg))


def _hgrn_head_chunk(q, hf, v, hg, lb, norm_g, tri, st, b_s, k_s, v_s):
    lc = q.shape[0]
    f = lb + (1.0 - lb) * _sigmoid(hf)
    k = 1.0 - f
    g_hi, g_mid, g_lo = _split3(jnp.log(f))
    bcum = (jnp.dot(tri, g_hi, preferred_element_type=F32) + jnp.dot(tri, g_mid, preferred_element_type=F32)
            + jnp.dot(tri, g_lo, preferred_element_type=F32))
    b_s[...] = bcum
    k_s[...] = k
    v_s[...] = v

    o_inter = _dot_nt(q * jnp.exp(bcum), st)
    t_idx = lax.broadcasted_iota(jnp.int32, (8, 1), 0)
    outs = []
    for i in range(lc // HG_SUB):
        r0 = i * HG_SUB
        o_i = o_inter[r0:r0 + HG_SUB]
        if i > 0:
            ref_b = bcum[r0:r0 + 1]
            a = _dot_nt(q[r0:r0 + HG_SUB] * jnp.exp(bcum[r0:r0 + HG_SUB] - ref_b), k[:r0] * jnp.exp(ref_b - bcum[:r0]))
            o_i = o_i + _dot(a, v[:r0])
        for half in range(HG_SUB // 8):
            t0 = r0 + 8 * half
            b_t, q_t = bcum[t0:t0 + 8], q[t0:t0 + 8]
            acc = o_i[8 * half:8 * half + 8]
            for s in range(r0, t0 + 8):
                d = b_t - b_s[s:s + 1, :]
                if s >= t0:
                    d = jnp.where(t_idx >= s - t0, d, NEG)
                w = jnp.exp(d) * q_t * k_s[s:s + 1, :]
                acc = acc + jnp.sum(w, axis=-1, keepdims=True) * v_s[s:s + 1, :]
            outs.append(acc)
    o = jnp.concatenate(outs, axis=0)
    b_last = bcum[lc - 1:lc]
    st_new = st * jnp.exp(b_last) + _dot(v.T, k * jnp.exp(b_last - bcum))
    return _hgrn_out(o, hg, norm_g), st_new


def _hgrn_prompt_kernel(hz_ref, lb_ref, ng_ref, tri_ref, o_ref, st_ref, state_t, b_s, k_s, v_s):
    c = pl.program_id(1)
    n_c = pl.num_programs(1)

    @pl.when(c == 0)
    def _():
        state_t[...] = jnp.zeros_like(state_t)

    lb_all = _forget_lower_bound(lb_ref[...])
    tri = tri_ref[...]
    for h in range(HG_HEADS):
        def seg(j):
            return hz_ref[:, j * HG_WIDTH + h * HG_DIM:j * HG_WIDTH + (h + 1) * HG_DIM]

        o, st_new = _hgrn_head_chunk(seg(0), seg(1), seg(2), seg(3), lb_all[:, h * HG_DIM:(h + 1) * HG_DIM],
                                     ng_ref[...], tri, state_t[h], b_s.at[h], k_s.at[h], v_s.at[h])
        o_ref[:, h * HG_DIM:(h + 1) * HG_DIM] = o.astype(BF16)
        state_t[h] = st_new

    @pl.when(c == n_c - 1)
    def _():
        for h in range(HG_HEADS):
            st_ref[h] = state_t[h].T


def _hgrn_prompt(hz3, hg_lb, norm_g, *, lc):
    b, t, w = hz3.shape
    tri = jnp.tril(jnp.ones((lc, lc), F32)).astype(BF16)
    return pl.pallas_call(
        _hgrn_prompt_kernel,
        grid=(b, t // lc),
        in_specs=[pl.BlockSpec((None, lc, w), lambda bi, c: (bi, c, 0)), _const_spec(hg_lb.shape),
                  _const_spec((1, HG_DIM)), _const_spec((lc, lc))],
        out_specs=[pl.BlockSpec((None, lc, HG_WIDTH), lambda bi, c: (bi, c, 0)),
                   pl.BlockSpec((None, HG_HEADS, HG_DIM, HG_DIM), lambda bi, c: (bi, 0, 0, 0))],
        out_shape=[jax.ShapeDtypeStruct((b, t, HG_WIDTH), BF16),
                   jax.ShapeDtypeStruct((b, HG_HEADS, HG_DIM, HG_DIM), F32)],
        scratch_shapes=[pltpu.VMEM((HG_HEADS, HG_DIM, HG_DIM), F32)] + [pltpu.VMEM((HG_HEADS, lc, HG_DIM), F32)] * 3,
        compiler_params=_cparams(("arbitrary", "arbitrary")),
        name="hgrn_prompt",
    )(hz3, hg_lb, norm_g, tri)


def _hgrn_dec_kernel(hz_ref, s0_ref, lb_ref, ng_ref, o_ref, s_ref):
    hz = hz_ref[...]
    lb_all = _forget_lower_bound(lb_ref[...])
    eye = lax.broadcasted_iota(jnp.int32, (HG_DIM, HG_DIM), 0) == lax.broadcasted_iota(jnp.int32, (HG_DIM, HG_DIM), 1)

    def column(row):
        return jnp.sum(jnp.where(eye, row, 0.0), axis=1, keepdims=True)

    outs = []
    for h in range(HG_HEADS):
        def seg(j):
            return hz[:, j * HG_WIDTH + h * HG_DIM:j * HG_WIDTH + (h + 1) * HG_DIM]

        lb = lb_all[:, h * HG_DIM:(h + 1) * HG_DIM]
        f = lb + (1.0 - lb) * _sigmoid(seg(1))
        q, k, v = seg(0), 1.0 - f, seg(2)
        s0 = s0_ref[h]
        s_ref[h] = column(f) * s0 + column(k) * v
        o = jnp.sum(column(q * f) * s0, axis=0, keepdims=True) + jnp.sum(q * k, axis=-1, keepdims=True) * v
        outs.append(_hgrn_out(o, seg(3), ng_ref[...]))
    o_ref[...] = jnp.concatenate(outs, axis=1)


def _hgrn_dec(hz, s0, hg_lb, norm_g):
    db = hz.shape[0]
    return pl.pallas_call(
        _hgrn_dec_kernel,
        grid=(db,),
        in_specs=[pl.BlockSpec((None, 1, hz.shape[1]), lambda b: (b, 0, 0)),
                  pl.BlockSpec((None, HG_HEADS, HG_DIM, HG_DIM), lambda b: (b, 0, 0, 0)),
                  _const_spec(hg_lb.shape), _const_spec((1, HG_DIM))],
        out_specs=[pl.BlockSpec((None, 1, HG_WIDTH), lambda b: (b, 0, 0)),
                   pl.BlockSpec((None, HG_HEADS, HG_DIM, HG_DIM), lambda b: (b, 0, 0, 0))],
        out_shape=[jax.ShapeDtypeStruct((db, 1, HG_WIDTH), F32), jax.ShapeDtypeStruct(s0.shape, F32)],
        compiler_params=_cparams(("arbitrary",)),
        name="hgrn_dec",
    )(hz.reshape(db, 1, -1), s0, hg_lb, norm_g)


def _merge_kernel(on_ref, oh_ref, sm_ref, x_ref, g1_ref, sh_ref, sc_ref, ng_ref, wn_ref, wh_ref, wo_ref,
                  x1_ref, h2_ref):
    d = x_ref.shape[1]
    sm = sm_ref[...]
    mix = sm[:, :d] * _dot(on_ref[...], wn_ref[...]) + sm[:, d:] * _dot(oh_ref[...], wh_ref[...])
    x1 = x_ref[...] + g1_ref[...] * _dot(mix, wo_ref[...])
    x1_ref[...] = x1
    xn = x1 * lax.rsqrt(jnp.mean(x1 * x1, axis=-1, keepdims=True) + EPS) * ng_ref[...]
    h2_ref[...] = (xn * (1.0 + sc_ref[...]) + sh_ref[...]).astype(BF16)


def _merge(o_nsa, o_hg, sm, x2d, g1, sh2, sc2, norm_g, w_br_nsa, w_br_hg, w_out, *, tm, tiles_per_mod):
    r, d = x2d.shape
    rows_mod = g1.shape[1]
    mod_spec = pl.BlockSpec((None, rows_mod, d), lambda i: (i // tiles_per_mod, 0, 0))

    def row_spec(width):
        return pl.BlockSpec((tm, width), lambda i: (i, 0))

    return pl.pallas_call(
        _merge_kernel,
        grid=(r // tm,),
        in_specs=[row_spec(o_nsa.shape[1]), row_spec(o_hg.shape[1]), row_spec(sm.shape[1]), row_spec(d),
                  mod_spec, mod_spec, mod_spec, _const_spec((1, d)),
                  _const_spec(w_br_nsa.shape), _const_spec(w_br_hg.shape), _const_spec(w_out.shape)],
        out_specs=[row_spec(d), row_spec(d)],
        out_shape=[jax.ShapeDtypeStruct((r, d), F32), jax.ShapeDtypeStruct((r, d), BF16)],
        compiler_params=_cparams(("arbitrary",)),
        name="merge",
    )(o_nsa, o_hg, sm, x2d, g1, sh2, sc2, norm_g, w_br_nsa, w_br_hg, w_out)


FF_CHUNKS = 2


def _ffn_core(h2, x1, g2, wu_ref, cw_ref, cb_ref, wd_ref, prev_rows):
    d_ff = wd_ref.shape[0]
    ch = d_ff // FF_CHUNKS
    acc = None
    us = []
    for ci in range(FF_CHUNKS):
        c0, c1 = ci * ch, (ci + 1) * ch
        u = jnp.dot(h2, wu_ref[:, c0:c1], preferred_element_type=F32)
        v = jnp.dot(h2, wu_ref[:, d_ff + c0:d_ff + c1], preferred_element_type=F32)
        u_m1, u_m2 = prev_rows(u, c0, c1)
        y = cb_ref[:, c0:c1] + cw_ref[0:1, c0:c1] * u_m2 + cw_ref[1:2, c0:c1] * u_m1 + cw_ref[2:3, c0:c1] * u
        part = _dot(y * _sigmoid(y) * v, wd_ref[c0:c1, :])
        acc = part if acc is None else acc + part
        us.append(u)
    return x1 + g2 * acc, us


def _ffn_prompt_kernel(h2_ref, x1_ref, g2_ref, wu_ref, cw_ref, cb_ref, wd_ref, y_ref, tail_ref, carry):
    ti = pl.program_id(1)
    tm = h2_ref.shape[0]

    @pl.when(ti == 0)
    def _():
        carry[...] = jnp.zeros_like(carry)

    row = lax.broadcasted_iota(jnp.int32, (tm, 1), 0)

    def prev_rows(u, c0, c1):
        last = carry[:, c0:c1]
        m1 = jnp.where(row == 0, last[7:8], pltpu.roll(u, 1, 0))
        m2 = jnp.where(row == 0, last[6:7], jnp.where(row == 1, last[7:8], pltpu.roll(u, 2, 0)))
        return m1, m2

    out, us = _ffn_core(h2_ref[...], x1_ref[...], g2_ref[...], wu_ref, cw_ref, cb_ref, wd_ref, prev_rows)
    y_ref[...] = out
    ch = us[0].shape[1]
    for ci, u in enumerate(us):
        carry[:, ci * ch:(ci + 1) * ch] = u[tm - 8:, :]
    tail_ref[...] = carry[...]


def _ffn_prompt(h2, x1, g2, w_up, conv_w8, conv_b, w_down, *, b, tm):
    r, d = x1.shape
    d_ff = w_down.shape[0]
    tpb = r // b // tm

    def row_spec(width):
        return pl.BlockSpec((tm, width), lambda bi, i: (bi * tpb + i, 0))

    return pl.pallas_call(
        _ffn_prompt_kernel,
        grid=(b, tpb),
        in_specs=[row_spec(d), row_spec(d), pl.BlockSpec((None, 1, d), lambda bi, i: (bi, 0, 0)),
                  _resident_spec(w_up.shape), _const_spec(conv_w8.shape), _const_spec(conv_b.shape),
                  _resident_spec(w_down.shape)],
        out_specs=[row_spec(d), pl.BlockSpec((None, 8, d_ff), lambda bi, i: (bi, 0, 0))],
        out_shape=[jax.ShapeDtypeStruct((r, d), F32), jax.ShapeDtypeStruct((b, 8, d_ff), F32)],
        scratch_shapes=[pltpu.VMEM((8, d_ff), F32)],
        compiler_params=_cparams(("arbitrary", "arbitrary")),
        name="ffn_prompt",
    )(h2, x1, g2, w_up, conv_w8, conv_b, w_down)


def _ffn_dec_kernel(h2_ref, x1_ref, g2_ref, wu_ref, cw_ref, cb_ref, wd_ref, m1_ref, m2_ref, y_ref, u_ref):
    def prev_rows(u, c0, c1):
        return m1_ref[:, c0:c1], m2_ref[:, c0:c1]

    out, us = _ffn_core(h2_ref[...], x1_ref[...], g2_ref[...], wu_ref, cw_ref, cb_ref, wd_ref, prev_rows)
    y_ref[...] = out
    u_ref[...] = jnp.concatenate(us, axis=1)


def _ffn_dec(h2, x1, g2, w_up, conv_w8, conv_b, w_down, u_m1, u_m2):
    r, d = x1.shape
    d_ff = w_down.shape[0]
    args = (h2, x1, g2, w_up, conv_w8, conv_b, w_down, u_m1, u_m2)
    return pl.pallas_call(
        _ffn_dec_kernel,
        grid=(1,),
        in_specs=[_const_spec(a.shape) for a in args],
        out_specs=[_const_spec((r, d)), _const_spec((r, d_ff))],
        out_shape=[jax.ShapeDtypeStruct((r, d), F32), jax.ShapeDtypeStruct((r, d_ff), F32)],
        compiler_params=_cparams(("arbitrary",)),
        name="ffn_dec",
    )(*args)


def _pack_w_in(w_in):
    cuts = np.cumsum([NSA_WIDTH] + [KV_WIDTH] * 6 + [3 * NSA_HEADS] + [HG_WIDTH] * 4)
    q, kv, gates, rest = w_in[:, :cuts[0]], w_in[:, cuts[0]:cuts[6]], w_in[:, cuts[6]:cuts[7]], w_in[:, cuts[7]:]
    hz, m = rest[:, :4 * HG_WIDTH], rest[:, 4 * HG_WIDTH:]
    per_group = 3 * HPG
    pad = jnp.zeros((w_in.shape[0], LANES - per_group), w_in.dtype)
    gate_cols = []
    for g in range(NSA_GROUPS):
        gate_cols += [gates[:, g * per_group:(g + 1) * per_group], pad]
    return jnp.concatenate([q, kv] + gate_cols + [hz, m], axis=1).astype(BF16)


def _rope_tables(pos):
    half = ROT_DIM // 2
    inv = ROPE_THETA ** (-jnp.arange(0, ROT_DIM, 2, dtype=F32) / ROT_DIM)
    ang = pos.astype(F32)[:, None] * inv[None, :]
    cos, sin = jnp.cos(ang), jnp.sin(ang)
    n = pos.shape[0]
    ones = jnp.ones((n, HEAD_DIM - ROT_DIM), F32)
    zeros = jnp.zeros((n, HEAD_DIM - ROT_DIM), F32)
    z8 = jnp.zeros((n, half), F32)
    c = jnp.concatenate([cos, cos, ones], axis=1)
    s_lo = jnp.concatenate([-sin, z8, zeros], axis=1)
    s_hi = jnp.concatenate([z8, sin, zeros], axis=1)
    return tuple(jnp.concatenate([t, t], axis=1) for t in (c, s_lo, s_hi))


def _cmp_weights(cmp_w1, cmp_w2, cmp_pe):
    n_half = CMP_BLK // CMP_STRIDE
    eye = jnp.eye(NSA_GROUPS, dtype=F32)
    w1s, w2s, pes = [], [], []
    for j in range(2):
        w = cmp_w1[j].reshape(n_half, CMP_STRIDE, HEAD_DIM, HEAD_DIM)
        big = jnp.einsum('rsdh,ge->sgdreh', w, eye)
        w1s.append(big.reshape(CMP_STRIDE * KV_WIDTH, n_half * KV_WIDTH))
        w2s.append(jnp.einsum('dh,ge->gdeh', cmp_w2[j], eye).reshape(KV_WIDTH, KV_WIDTH))
        pe = cmp_pe[j].reshape(n_half, CMP_STRIDE, 1, HEAD_DIM)
        pe = jnp.broadcast_to(pe, (n_half, CMP_STRIDE, NSA_GROUPS, HEAD_DIM)).reshape(n_half, -1)
        pes.append(jnp.concatenate([pe, jnp.zeros((8 - n_half, pe.shape[1]), F32)], axis=0))
    return jnp.stack(w1s).astype(BF16), jnp.stack(w2s).astype(BF16), jnp.stack(pes).astype(BF16)


def _overlap(nc_pad, nc, ns, ns_pad):
    start = np.arange(nc_pad) * CMP_STRIDE
    end = start + CMP_BLK - 1
    s0 = np.arange(ns_pad) * SLC_BLK
    s1 = s0 + SLC_BLK - 1
    m = (start[:, None] <= s1[None, :]) & (end[:, None] >= s0[None, :])
    m &= (np.arange(nc_pad) < nc)[:, None] & (np.arange(ns_pad) < ns)[None, :]
    return jnp.asarray(m, BF16)


def kernel(x_prompt, x_sample, cache_cmp_k, cache_cmp_v, cache_slc_k, cache_slc_v, cache_win_k, cache_win_v,
           state_hgrn, state_ffn_conv, page_table, c_prompt, c_sample, ada_w, ada_b, norm_attn, norm_ffn, w_in,
           q_gain, k_gain, cmp_w1, cmp_w2, cmp_pe, hg_lb, hg_norm, w_br_nsa, w_br_hg, w_out, w_up, conv_w,
           conv_b, w_down):
    depth = ada_w.shape[0]
    assert depth == 1, "single-layer trunk only"
    b, t, d = x_prompt.shape
    db, tn, _ = x_sample.shape
    assert tn == 1
    n_pages = page_table.shape[1]
    past = n_pages * PAGE_SIZE
    d_ff = w_down.shape[1]
    tq = 256
    tm = 256
    lc = 128
    assert t % tq == 0 and t >= WINDOW and WINDOW % tq == 0 and n_pages % PAGES_PER_STEP == 0
    assert cache_win_k.shape[2] == WINDOW

    w_pack = _pack_w_in(w_in[0])
    qg = jnp.tile(q_gain[0], NSA_HEADS).reshape(1, NSA_WIDTH)
    kg = jnp.concatenate([jnp.tile(k_gain[0], (1, NSA_GROUPS)), jnp.zeros((5, KV_WIDTH), F32)], axis=0)
    hd = np.arange(NSA_WIDTH) // HEAD_DIM
    bd = jnp.asarray(hd[:, None] == hd[None, :], BF16)
    w1c, w2c, pe_x = _cmp_weights(cmp_w1[0], cmp_w2[0], cmp_pe[0])
    conv_w8 = jnp.concatenate([conv_w[0], jnp.zeros((8 - CONV_W, d_ff), F32)], axis=0)
    conv_b2 = conv_b[0].reshape(1, d_ff)
    wbn, wbh, wo = w_br_nsa[0].astype(BF16), w_br_hg[0].astype(BF16), w_out[0].astype(BF16)
    wu, wd = w_up[0].astype(BF16), w_down[0].astype(BF16)
    norm_a, norm_f, hg_n = norm_attn[0].reshape(1, d), norm_ffn[0].reshape(1, d), hg_norm[0].reshape(1, HG_DIM)

    n_c = b + db
    c_all = jnp.concatenate([c_prompt, c_sample, jnp.zeros((-n_c % 8, d), F32)], axis=0)
    ada = _ada(c_all, ada_w[0], ada_b[0])
    mods_p = [ada[:b, i * d:(i + 1) * d].reshape(b, 1, d) for i in range(6)]
    mods_s = [ada[b:n_c, i * d:(i + 1) * d].reshape(1, db, d) for i in range(6)]

    tpb = t // tm
    (q, kc_r, vc_r, ks_r, vs_r, kw_r, vw_r, kcb, vcb, kvb, gt, hz, sm) = _inproj(
        x_prompt.reshape(b * t, d), mods_p[0], mods_p[1], norm_a, w_pack, qg, kg, _rope_tables(jnp.arange(t)), bd,
        tm=tm, tiles_per_mod=tpb, rope_tiles=tpb)

    nsub = t // CMP_STRIDE
    pk = _mm(kcb.reshape(b * nsub, CMP_STRIDE * KV_WIDTH), w1c[0], 512, "cmp_proj_k").reshape(b, nsub, -1)
    pv = _mm(vcb.reshape(b * nsub, CMP_STRIDE * KV_WIDTH), w1c[1], 512, "cmp_proj_v").reshape(b, nsub, -1)
    kcv = _cmp_combine(pk, pv, pe_x, w1c, w2c)
    ov = _overlap(nsub, nsub - 1, t // SLC_BLK, LANES).T
    o_nsa = _nsa_prompt(q.reshape(b, t, -1), kcv, kvb.reshape(b, t, -1), gt.reshape(b, t, -1), ov, tq=tq)

    o_hg, st_p = _hgrn_prompt(hz.reshape(b, t, -1), hg_lb, hg_n, lc=lc)

    x1, h2 = _merge(o_nsa.reshape(b * t, -1), o_hg.reshape(b * t, -1), sm, x_prompt.reshape(b * t, d),
                    mods_p[2], mods_p[3], mods_p[4], norm_f, wbn, wbh, wo, tm=tm, tiles_per_mod=tpb)
    y_p, tail = _ffn_prompt(h2, x1, mods_p[5], wu, conv_w8, conv_b2, wd, b=b, tm=tm)

    def rows5(a):
        return a.reshape(1, b, t, NSA_GROUPS, HEAD_DIM)

    w = min(WINDOW, t)
    out_p = (rows5(kc_r), rows5(vc_r), rows5(ks_r), rows5(vs_r), rows5(kw_r)[:, :, -w:], rows5(vw_r)[:, :, -w:],
             st_p[None], tail[None, :, 8 - (CONV_W - 1):, :])

    pos_s = jnp.full((db,), past, jnp.int32)
    (q_s, kc_s, vc_s, ks_s, vs_s, kw_s, vw_s, _, _, _, gt_s, hz_s, sm_s) = _inproj(
        x_sample.reshape(db, d), mods_s[0], mods_s[1], norm_a, w_pack, qg, kg, _rope_tables(pos_s), bd,
        tm=db, tiles_per_mod=1, rope_tiles=1)

    def feature_major(a):
        return jnp.transpose(a, (0, 2, 3, 1))

    assert (past + tn) // CMP_STRIDE == past // CMP_STRIDE
    pk_s = _cmp_proj_paged(feature_major(cache_cmp_k[0]).reshape(-1, KV_WIDTH, PAGE_SIZE), page_table, w1c[0],
                           "cmp_proj_pk")
    pv_s = _cmp_proj_paged(feature_major(cache_cmp_v[0]).reshape(-1, KV_WIDTH, PAGE_SIZE), page_table, w1c[1],
                           "cmp_proj_pv")
    kcv_s = _cmp_combine(pk_s, pv_s, pe_x, w1c, w2c)
    nsub_s = past // CMP_STRIDE
    ns_s = -(-(past + tn) // SLC_BLK)
    ns_pad = -(-ns_s // LANES) * LANES
    ov_s = _overlap(nsub_s, nsub_s - 1, ns_s, ns_pad)
    o_cmp_s, idx8 = _nsa_dec_select(q_s.reshape(db, NSA_HEADS, HEAD_DIM), kcv_s, ov_s, past)
    idx = idx8[:, :NSA_GROUPS, :N_SEL].reshape(db, NSA_GROUPS * N_SEL)
    new_cols = [a.reshape(db, NSA_GROUPS, HEAD_DIM, 1) for a in (ks_s, vs_s, kw_s, vw_s)]
    per_group = 3 * HPG
    gt4 = jnp.stack([gt_s[:, g * LANES:g * LANES + per_group] for g in range(NSA_GROUPS)], axis=1)
    o_nsa_s, wk_new, wv_new = _nsa_dec_attend(
        idx, page_table, feature_major(cache_slc_k[0]), feature_major(cache_slc_v[0]),
        q_s.astype(F32).reshape(db, NSA_GROUPS, HPG, HEAD_DIM), new_cols,
        feature_major(cache_win_k[0]), feature_major(cache_win_v[0]),
        o_cmp_s.reshape(db, NSA_GROUPS, HPG, HEAD_DIM), gt4.reshape(db, NSA_GROUPS, HPG, 3), past)
    wk_new, wv_new = (jnp.transpose(a, (0, 3, 1, 2)) for a in (wk_new, wv_new))

    o_hg_s, st_s = _hgrn_dec(hz_s, state_hgrn[0], hg_lb, hg_n)
    x1_s, h2_s = _merge(o_nsa_s.reshape(db, NSA_WIDTH), o_hg_s.reshape(db, HG_WIDTH), sm_s, x_sample.reshape(db, d),
                        mods_s[2], mods_s[3], mods_s[4], norm_f, wbn, wbh, wo, tm=db, tiles_per_mod=1)
    buf = state_ffn_conv[0]
    y_s, u_s = _ffn_dec(h2_s, x1_s, mods_s[5].reshape(db, d), wu, conv_w8, conv_b2, wd, buf[:, 1], buf[:, 0])
    conv_s = jnp.stack([buf[:, 1], u_s], axis=1)

    def rows5s(a):
        return a.reshape(1, db, tn, NSA_GROUPS, HEAD_DIM)

    out_s = (rows5s(kc_s), rows5s(vc_s), rows5s(ks_s), rows5s(vs_s),
             wk_new.reshape(1, db, WINDOW, NSA_GROUPS, HEAD_DIM), wv_new.reshape(1, db, WINDOW, NSA_GROUPS, HEAD_DIM),
             st_s[None], conv_s[None])
    return (y_p.reshape(b, t, d), y_s.reshape(db, tn, d)) + out_p + out_s
```

```python
import functools

import numpy as np
import jax
import jax.numpy as jnp
from jax import lax
from jax.experimental import pallas as pl
from jax.experimental.pallas import tpu as pltpu

F32 = jnp.float32
BF16 = jnp.bfloat16

HEAD_DIM = 64
NSA_HEADS = 8
NSA_GROUPS = 2
HPG = NSA_HEADS // NSA_GROUPS
ROT_DIM = HEAD_DIM // 4
ROPE_THETA = 500000.0
CMP_BLK = 32
CMP_STRIDE = 16
SLC_BLK = 64
N_SEL = 16
WINDOW = 512
PAGE_SIZE = 128
HG_HEADS = 4
HG_DIM = 128
HG_SUB_LOG2 = 4
HG_SUB = 1 << HG_SUB_LOG2
HG_SAFE_DECAY = 60.0
SUBLANES = 8
SOFTMAX_VREGS = 32
CONV_W = 3
EPS = 1e-6
NEG = -1e30
SEL_BIAS = -1e9
LANES = 128
VMEM_LIMIT = 56 * 1024 * 1024

NSA_WIDTH = NSA_HEADS * HEAD_DIM
KV_WIDTH = NSA_GROUPS * HEAD_DIM
HG_WIDTH = HG_HEADS * HG_DIM
GRP_Q = HPG * HEAD_DIM

SEG_Q = 0
SEG_KV = SEG_Q + NSA_WIDTH
SEG_G = SEG_KV + 6 * KV_WIDTH
SEG_H = SEG_G + NSA_GROUPS * LANES
SEG_M = SEG_H + 4 * HG_WIDTH


def _cparams(sem):
    return pltpu.CompilerParams(dimension_semantics=sem, vmem_limit_bytes=VMEM_LIMIT)


def _dot(a, b):
    return jnp.dot(a.astype(BF16), b.astype(BF16), preferred_element_type=F32)


def _dot_nt(a, b):
    return lax.dot_general(a.astype(BF16), b.astype(BF16), (((1,), (1,)), ((), ())), preferred_element_type=F32)


def _split2(x):
    hi = x.astype(BF16)
    lo = (x - hi.astype(F32)).astype(BF16)
    return hi, lo


def _split3(x):
    hi = x.astype(BF16)
    r = x - hi.astype(F32)
    mid = r.astype(BF16)
    lo = (r - mid.astype(F32)).astype(BF16)
    return hi, mid, lo


def _dot_x3(a, b):
    ah, al = _split2(a)
    bh, bl = _split2(b)
    return (jnp.dot(ah, bh, preferred_element_type=F32) + jnp.dot(al, bh, preferred_element_type=F32)
            + jnp.dot(ah, bl, preferred_element_type=F32))


def _sigmoid(x):
    return 1.0 / (1.0 + jnp.exp(-x))


def _const_spec(shape):
    nd = len(shape)
    return pl.BlockSpec(shape, lambda *_: (0,) * nd)


def _resident_spec(shape):
    nd = len(shape)
    return pl.BlockSpec(shape, lambda *_: (0,) * nd, pipeline_mode=pl.Buffered(1))


def _ada_kernel(c_ref, w_ref, b_ref, o_ref):
    o_ref[...] = _dot_x3(c_ref[...], w_ref[...]) + b_ref[...]


def _ada(c_all, w, b):
    r, d = c_all.shape
    n = w.shape[1]
    tn = 1536
    return pl.pallas_call(
        _ada_kernel,
        grid=(n // tn,),
        in_specs=[_const_spec((r, d)), pl.BlockSpec((d, tn), lambda j: (0, j)), pl.BlockSpec((1, tn), lambda j: (0, j))],
        out_specs=pl.BlockSpec((r, tn), lambda j: (0, j)),
        out_shape=jax.ShapeDtypeStruct((r, n), F32),
        compiler_params=_cparams(("arbitrary",)),
        name="ada",
    )(c_all, w, b.reshape(1, n))


def _rope(x, c, s_lo, s_hi):
    width = x.shape[1]
    reps = width // LANES
    if reps > 1:
        c = jnp.concatenate([c] * reps, axis=1)
        s_lo = jnp.concatenate([s_lo] * reps, axis=1)
        s_hi = jnp.concatenate([s_hi] * reps, axis=1)
    half = ROT_DIM // 2
    return x * c + pltpu.roll(x, half, 1) * s_hi + pltpu.roll(x, width - half, 1) * s_lo


def _head_rms(z, bd):
    hi, lo = _split2(z * z)
    ss = jnp.dot(hi, bd, preferred_element_type=F32) + jnp.dot(lo, bd, preferred_element_type=F32)
    return z * lax.rsqrt(ss * (1.0 / HEAD_DIM) + EPS)


def _inproj_kernel(x_ref, sh_ref, sc_ref, ng_ref, w_ref, qg_ref, kg_ref, rc_ref, rlo_ref, rhi_ref, bd_ref,
                   q_ref, kc_ref, vc_ref, ks_ref, vs_ref, kw_ref, vw_ref, kcb_ref, vcb_ref, kvb_ref, gt_ref, hz_ref,
                   sm_ref):
    x = x_ref[...]
    xn = x * lax.rsqrt(jnp.mean(x * x, axis=-1, keepdims=True) + EPS) * ng_ref[...]
    h = (xn * (1.0 + sc_ref[...]) + sh_ref[...]).astype(BF16)
    rc, rlo, rhi = rc_ref[...], rlo_ref[...], rhi_ref[...]
    bd = bd_ref[...]

    zq = jnp.dot(h, w_ref[:, SEG_Q:SEG_KV], preferred_element_type=F32)
    q = _rope(_head_rms(zq, bd) * qg_ref[...], rc, rlo, rhi)
    q_ref[...] = (q * (HEAD_DIM ** -0.5)).astype(BF16)

    zkv = jnp.dot(h, w_ref[:, SEG_KV:SEG_G], preferred_element_type=F32)
    bd1 = bd[:LANES, :LANES]
    k_refs = (kc_ref, ks_ref, kw_ref)
    v_refs = (vc_ref, vs_ref, vw_ref)
    rows = []
    for j in range(3):
        zk = zkv[:, 2 * j * KV_WIDTH:(2 * j + 1) * KV_WIDTH]
        k = _rope(_head_rms(zk, bd1) * kg_ref[j:j + 1, :], rc, rlo, rhi)
        v = zkv[:, (2 * j + 1) * KV_WIDTH:(2 * j + 2) * KV_WIDTH]
        for g in range(NSA_GROUPS):
            k_refs[j][:, g, :] = k[:, g * HEAD_DIM:(g + 1) * HEAD_DIM]
            v_refs[j][:, g, :] = v[:, g * HEAD_DIM:(g + 1) * HEAD_DIM]
        rows.append((k, v))
    kcb_ref[...] = rows[0][0].astype(BF16)
    vcb_ref[...] = rows[0][1].astype(BF16)
    parts = []
    for g in range(NSA_GROUPS):
        for k, v in rows[1:]:
            parts.append(k[:, g * HEAD_DIM:(g + 1) * HEAD_DIM])
            parts.append(v[:, g * HEAD_DIM:(g + 1) * HEAD_DIM])
    kvb_ref[...] = jnp.concatenate(parts, axis=1).astype(BF16)

    gt_ref[...] = _sigmoid(jnp.dot(h, w_ref[:, SEG_G:SEG_H], preferred_element_type=F32))
    hz_ref[...] = jnp.dot(h, w_ref[:, SEG_H:SEG_M], preferred_element_type=F32)
    sm_ref[...] = _sigmoid(jnp.dot(h, w_ref[:, SEG_M:], preferred_element_type=F32))


def _inproj(x2d, shift, scale, norm_g, w_pack, qg, kg, rope_tabs, bd, *, tm, tiles_per_mod, rope_tiles):
    r, d = x2d.shape
    n_tiles = r // tm
    rows_mod = shift.shape[1]
    mod_spec = pl.BlockSpec((None, rows_mod, d), lambda i: (i // tiles_per_mod, 0, 0))
    rope_spec = pl.BlockSpec((tm, LANES), lambda i: (i % rope_tiles, 0))

    def row_spec(width):
        return pl.BlockSpec((tm, width), lambda i: (i, 0))

    widths = (KV_WIDTH, KV_WIDTH, 4 * KV_WIDTH, NSA_GROUPS * LANES, 4 * HG_WIDTH, w_pack.shape[1] - SEG_M)
    dtypes = (BF16, BF16, BF16, F32, F32, F32)
    cache_spec = pl.BlockSpec((tm, NSA_GROUPS, HEAD_DIM), lambda i: (i, 0, 0))
    cache_shape = jax.ShapeDtypeStruct((r, NSA_GROUPS, HEAD_DIM), F32)
    return pl.pallas_call(
        _inproj_kernel,
        grid=(n_tiles,),
        in_specs=[row_spec(d), mod_spec, mod_spec, _const_spec((1, d)), _resident_spec(w_pack.shape),
                  _const_spec(qg.shape), _const_spec(kg.shape), rope_spec, rope_spec, rope_spec, _const_spec(bd.shape)],
        out_specs=[row_spec(NSA_WIDTH)] + [cache_spec] * 6 + [row_spec(w) for w in widths],
        out_shape=[jax.ShapeDtypeStruct((r, NSA_WIDTH), BF16)] + [cache_shape] * 6
        + [jax.ShapeDtypeStruct((r, w), dt) for w, dt in zip(widths, dtypes)],
        compiler_params=_cparams(("arbitrary",)),
        name="inproj",
    )(x2d, shift, scale, norm_g, w_pack, qg, kg, *rope_tabs, bd)


def _mm_kernel(a_ref, w_ref, o_ref):
    o_ref[...] = _dot(a_ref[...], w_ref[...])


def _mm(a, w, tm, name):
    m, k = a.shape
    n = w.shape[1]
    tm = min(tm, m)
    return pl.pallas_call(
        _mm_kernel,
        grid=(m // tm,),
        in_specs=[pl.BlockSpec((tm, k), lambda i: (i, 0)), _const_spec((k, n))],
        out_specs=pl.BlockSpec((tm, n), lambda i: (i, 0)),
        out_shape=jax.ShapeDtypeStruct((m, n), F32),
        compiler_params=_cparams(("arbitrary",)),
        name=name,
    )(a, w)


PAGES_PER_STEP = 16


def _cmp_proj_paged_kernel(pt_ref, *refs):
    k_pages, v_pages = refs[:PAGES_PER_STEP], refs[PAGES_PER_STEP:2 * PAGES_PER_STEP]
    perm_ref, w_ref, ok_ref, ov_ref, rk_ref, rv_ref = refs[2 * PAGES_PER_STEP:]
    sub = PAGE_SIZE // CMP_STRIDE
    perm = perm_ref[...]
    for kv, (pages, rows_ref, o_ref) in enumerate(((k_pages, rk_ref, ok_ref), (v_pages, rv_ref, ov_ref))):
        for j, pr in enumerate(pages):
            rows = _dot_nt(perm, pr[...])
            for s in range(CMP_STRIDE):
                rows_ref[s, j * sub:(j + 1) * sub, :] = rows[s * sub:(s + 1) * sub, :]
        acc = None
        for s in range(CMP_STRIDE):
            part = _dot(rows_ref[s], w_ref[kv, s * KV_WIDTH:(s + 1) * KV_WIDTH, :])
            acc = part if acc is None else acc + part
        o_ref[...] = acc


def _cmp_proj_paged(pool_k, pool_v, page_table, w):
    db, n_pages = page_table.shape
    sub = PAGE_SIZE // CMP_STRIDE
    n = w.shape[2]
    steps = n_pages // PAGES_PER_STEP
    pos = np.arange(PAGE_SIZE)
    perm = jnp.asarray((pos[:, None] % sub) * CMP_STRIDE + pos[:, None] // sub == pos[None, :], BF16)
    page_specs = [pl.BlockSpec((None, KV_WIDTH, PAGE_SIZE), lambda b, c, pt, j=j: (pt[b, c * PAGES_PER_STEP + j], 0, 0))
                  for j in range(PAGES_PER_STEP)]
    out_spec = pl.BlockSpec((None, PAGES_PER_STEP * sub, n), lambda b, c, pt: (b, c, 0))
    rows_scratch = pltpu.VMEM((CMP_STRIDE, PAGES_PER_STEP * sub, KV_WIDTH), F32)
    grid_spec = pltpu.PrefetchScalarGridSpec(
        num_scalar_prefetch=1,
        grid=(db, steps),
        in_specs=page_specs + page_specs + [pl.BlockSpec(perm.shape, lambda b, c, pt: (0, 0)),
                                            pl.BlockSpec(w.shape, lambda b, c, pt: (0, 0, 0))],
        out_specs=[out_spec, out_spec],
        scratch_shapes=[rows_scratch, rows_scratch],
    )
    out = jax.ShapeDtypeStruct((db, n_pages * sub, n), F32)
    return pl.pallas_call(
        _cmp_proj_paged_kernel,
        grid_spec=grid_spec,
        out_shape=[out, out],
        compiler_params=_cparams(("arbitrary", "arbitrary")),
        name="cmp_proj_paged",
    )(page_table, *([pool_k] * PAGES_PER_STEP), *([pool_v] * PAGES_PER_STEP), perm, w)


def _cmp_combine_kernel(pk_ref, pv_ref, pe_ref, w1_ref, w2_ref, o_ref):
    n = pk_ref.shape[0]
    outs = []
    for j, p_ref in enumerate((pk_ref, pv_ref)):
        pe = jnp.dot(pe_ref[j], w1_ref[j], preferred_element_type=F32)
        pe_hid = pe[0:1, :KV_WIDTH] + pe[1:2, KV_WIDTH:]
        p = p_ref[...]
        nxt = pltpu.roll(p[:, KV_WIDTH:], n - 1, 0)
        hid = pe_hid + p[:, :KV_WIDTH] + nxt
        outs.append(_dot(jax.nn.gelu(hid), w2_ref[j]))
    for g in range(NSA_GROUPS):
        o_ref[g] = jnp.concatenate([o[:, g * HEAD_DIM:(g + 1) * HEAD_DIM] for o in outs], axis=1).astype(BF16)


def _cmp_combine(pk, pv, pe_x, w1, w2bd):
    bx, n, c = pk.shape
    return pl.pallas_call(
        _cmp_combine_kernel,
        grid=(bx,),
        in_specs=[pl.BlockSpec((None, n, c), lambda b: (b, 0, 0)), pl.BlockSpec((None, n, c), lambda b: (b, 0, 0)),
                  _const_spec(pe_x.shape), _const_spec(w1.shape), _const_spec(w2bd.shape)],
        out_specs=pl.BlockSpec((None, NSA_GROUPS, n, 2 * HEAD_DIM), lambda b: (b, 0, 0, 0)),
        out_shape=jax.ShapeDtypeStruct((bx, NSA_GROUPS, n, 2 * HEAD_DIM), BF16),
        compiler_params=_cparams(("arbitrary",)),
        name="cmp_combine",
    )(pk, pv, pe_x, w1, w2bd)


def _masked_softmax(s, mask):
    s = jnp.where(mask, s, NEG)
    m = jnp.max(s, axis=-1, keepdims=True)
    e = jnp.where(mask, jnp.exp(s - m), 0.0)
    return e / jnp.maximum(jnp.sum(e, axis=-1, keepdims=True), 1e-30)


def _top_k_mask(score, k, axis, idx_out=False):
    n = score.shape[axis]
    pos = lax.broadcasted_iota(jnp.int32, score.shape, axis).astype(F32)
    sel = jnp.zeros(score.shape, jnp.bool_)
    idx = jnp.zeros((score.shape[0], LANES), jnp.int32)
    lane_out = lax.broadcasted_iota(jnp.int32, (score.shape[0], LANES), 1)
    work = score
    for it in range(k):
        mx = jnp.max(work, axis=axis, keepdims=True)
        first = jnp.min(jnp.where(work == mx, pos, float(n)), axis=axis, keepdims=True)
        pick = pos == first
        sel = jnp.logical_or(sel, pick)
        work = jnp.where(pick, -jnp.inf, work)
        if idx_out:
            idx = jnp.where(lane_out == it, first.astype(jnp.int32), idx)
    return (sel, idx) if idx_out else sel


def _block_scores(imp, qpos, axis):
    blk = lax.broadcasted_iota(jnp.int32, imp.shape, axis)
    cur = qpos >> 6
    valid = (blk << 6) <= qpos
    forced = (blk == 0) | (blk == cur) | (blk == cur - 1)
    return jnp.where(forced, 1e9, jnp.where(valid, imp, -1e9))


def _nsa_prompt_kernel(q_ref, kcv_ref, kvb_ref, gt_ref, ov_ref, o_ref, s0_scr, s1_scr, p0_scr, p1_scr, a0_scr, a1_scr,
                       m_scr, acc_scr, *, tq):
    qi = pl.program_id(2)
    q0 = qi * tq
    tk = tq
    nc = kcv_ref.shape[0]
    rows = HPG * tq

    q = q_ref[...]
    qs = jnp.concatenate([q[:, h * HEAD_DIM:(h + 1) * HEAD_DIM] for h in range(HPG)], axis=0)
    qpos = q0 + lax.broadcasted_iota(jnp.int32, (tq, 1), 0)
    qpos4 = jnp.concatenate([qpos] * HPG, axis=0)

    kcv = kcv_ref[...]
    kc, vc = kcv[:, :HEAD_DIM], kcv[:, HEAD_DIM:]
    cmp_end = lax.broadcasted_iota(jnp.int32, (1, nc), 1) * CMP_STRIDE + (CMP_BLK - 1)
    p = _masked_softmax(_dot_nt(qs, kc), cmp_end <= qpos4)
    o_cmp = _dot(p, vc)
    psum = p[0:tq]
    for h in range(1, HPG):
        psum = psum + p[h * tq:(h + 1) * tq]
    p_hi, p_lo = _split2(psum)
    ov_t = ov_ref[...]
    imp_t = _dot_nt(ov_t, p_hi) + _dot_nt(ov_t, p_lo)

    qpos_row = q0 + lax.broadcasted_iota(jnp.int32, (1, tq), 1)
    sel_t = _top_k_mask(_block_scores(imp_t, qpos_row, 0), N_SEL, 0)
    bias = jnp.where(sel_t, 0.0, SEL_BIAS).T.astype(BF16)
    q_aug = jnp.concatenate([jnp.concatenate([bias] * HPG, axis=0), qs], axis=1)

    lane_blk = lax.broadcasted_iota(jnp.int32, (1, LANES), 1)
    kcol = lax.broadcasted_iota(jnp.int32, (tk, 1), 0)
    krow = lax.broadcasted_iota(jnp.int32, (1, tk), 1)
    ones_v = jnp.ones((tk, LANES - HEAD_DIM), BF16)
    rb = SOFTMAX_VREGS * SUBLANES * LANES // tk
    qpos_rb = [q0 + (r % (tq // rb)) * rb + lax.broadcasted_iota(jnp.int32, (rb, 1), 0) for r in range(rows // rb)]

    half = rows // 2
    s_bufs, p_bufs, a_bufs = (s0_scr, s1_scr), (p0_scr, p1_scr), (a0_scr, a1_scr)

    def key_tile(j):
        return kvb_ref[pl.ds(pl.multiple_of(j * tk, tk), tk), :]

    def scores(slot, j, slc):
        blk = key_tile(j)
        if slc:
            onehot = jnp.where(((j * tk + kcol) >> 6) == lane_blk, 1.0, 0.0).astype(BF16)
            q_lhs, k_rhs = q_aug, jnp.concatenate([onehot, blk[:, 0:HEAD_DIM]], axis=1)
        else:
            q_lhs, k_rhs = qs, blk[:, 2 * HEAD_DIM:3 * HEAD_DIM]
        for h0 in (0, half):
            s_bufs[slot][h0:h0 + half, :] = _dot_nt(q_lhs[h0:h0 + half], k_rhs)

    def softmax(slot, j, mask_kind, live=None):
        k0 = j * tk
        for r in range(rows // rb):
            sl = slice(r * rb, (r + 1) * rb)
            s = s_bufs[slot][sl, :]
            keep = None
            if mask_kind == "causal":
                keep = (k0 + krow) <= qpos_rb[r]
            elif mask_kind == "window":
                keep = qpos_rb[r] - (k0 + krow) < WINDOW
            if live is not None:
                keep = live if keep is None else jnp.logical_and(keep, live)
            if keep is not None:
                s = jnp.where(keep, s, NEG)
            m_old = m_scr[sl, :]
            m_new = jnp.maximum(m_old, jnp.max(s, axis=-1, keepdims=True))
            a_bufs[slot][sl, :] = jnp.exp(m_old - m_new)
            m_scr[sl, :] = m_new
            p_bufs[slot][sl, :] = jnp.exp(s - jnp.concatenate([m_new] * (tk // LANES), axis=1)).astype(BF16)

    def accumulate(slot, j, v_lane0):
        v_aug = jnp.concatenate([key_tile(j)[:, v_lane0:v_lane0 + HEAD_DIM], ones_v], axis=1)
        for h0 in (0, half):
            sl = slice(h0, h0 + half)
            acc_scr[sl, :] = a_bufs[slot][sl, :] * acc_scr[sl, :] + jnp.dot(p_bufs[slot][sl, :], v_aug,
                                                                           preferred_element_type=F32)

    def reset():
        m_scr[...] = jnp.full(m_scr.shape, NEG, F32)
        acc_scr[...] = jnp.zeros(acc_scr.shape, F32)

    def result():
        acc = acc_scr[...]
        return acc[:, :HEAD_DIM] / jnp.maximum(acc[:, HEAD_DIM:HEAD_DIM + 1], 1e-30)

    reset()
    scores(0, qi, True)
    scores(1, 0, True)
    softmax(0, qi, "causal")

    def past_pair(u, carry):
        ta, tb = 2 * u, 2 * u + 1
        scores(0, jnp.minimum(tb, qi), True)
        softmax(1, ta, None)
        accumulate(0, jnp.where(u == 0, qi, ta - 1), HEAD_DIM)
        scores(1, jnp.minimum(tb + 1, qi), True)
        softmax(0, tb, None, live=tb < qi)
        accumulate(1, ta, HEAD_DIM)
        return carry

    n_pairs = (qi + 1) // 2
    lax.fori_loop(0, n_pairs, past_pair, 0)
    accumulate(0, jnp.where(n_pairs == 0, qi, jnp.minimum(2 * n_pairs - 1, qi)), HEAD_DIM)
    o_slc = result()

    reset()
    n_back = WINDOW // tk
    tiles = [(qi, "causal", None)] + [(jnp.maximum(qi - back, 0), "window" if back == n_back else None, qi >= back)
                                      for back in range(1, n_back + 1)]
    scores(0, tiles[0][0], False)
    for i, (j, mask_kind, live) in enumerate(tiles):
        if i + 1 < len(tiles):
            scores((i + 1) % 2, tiles[i + 1][0], False)
        softmax(i % 2, j, mask_kind, live)
        accumulate(i % 2, j, 3 * HEAD_DIM)
    o_win = result()

    gt = gt_ref[...]
    outs = []
    for h in range(HPG):
        r0 = h * tq
        outs.append(gt[:, 3 * h:3 * h + 1] * o_cmp[r0:r0 + tq] + gt[:, 3 * h + 1:3 * h + 2] * o_slc[r0:r0 + tq]
                    + gt[:, 3 * h + 2:3 * h + 3] * o_win[r0:r0 + tq])
    o_ref[...] = jnp.concatenate(outs, axis=1).astype(BF16)


def _nsa_prompt(q, kcv, kvb, gt, ov, *, tq):
    b, t, _ = q.shape
    nc = kcv.shape[2]
    return pl.pallas_call(
        functools.partial(_nsa_prompt_kernel, tq=tq),
        grid=(b, NSA_GROUPS, t // tq),
        in_specs=[pl.BlockSpec((None, tq, GRP_Q), lambda bi, g, i: (bi, i, g)),
                  pl.BlockSpec((None, None, nc, 2 * HEAD_DIM), lambda bi, g, i: (bi, g, 0, 0)),
                  pl.BlockSpec((None, t, 4 * HEAD_DIM), lambda bi, g, i: (bi, 0, g)),
                  pl.BlockSpec((None, tq, LANES), lambda bi, g, i: (bi, i, g)),
                  _const_spec(ov.shape)],
        out_specs=pl.BlockSpec((None, tq, GRP_Q), lambda bi, g, i: (bi, i, g)),
        out_shape=jax.ShapeDtypeStruct((b, t, NSA_WIDTH), BF16),
        scratch_shapes=[pltpu.VMEM((HPG * tq, tq), F32)] * 2 + [pltpu.VMEM((HPG * tq, tq), BF16)] * 2
        + [pltpu.VMEM((HPG * tq, LANES), F32)] * 4,
        compiler_params=_cparams(("arbitrary", "arbitrary", "arbitrary")),
        name="nsa_prompt",
    )(q, kcv, kvb, gt, ov)


def _nsa_dec_select_kernel(q_ref, kcv_ref, ov_ref, ocmp_ref, idx_ref, *, qpos):
    nc = kcv_ref.shape[1]
    q = q_ref[...]
    cmp_end = lax.broadcasted_iota(jnp.int32, (1, nc), 1) * CMP_STRIDE + (CMP_BLK - 1)
    mask = cmp_end <= qpos
    ov = ov_ref[...]
    imps = []
    for g in range(NSA_GROUPS):
        kcv = kcv_ref[g]
        p = _masked_softmax(_dot_nt(q[g * HPG:(g + 1) * HPG], kcv[:, :HEAD_DIM]), mask)
        ocmp_ref[g * HPG:(g + 1) * HPG, :] = _dot(p, kcv[:, HEAD_DIM:])
        p_hi, p_lo = _split2(jnp.sum(p, axis=0, keepdims=True))
        imps.append(jnp.dot(p_hi, ov, preferred_element_type=F32) + jnp.dot(p_lo, ov, preferred_element_type=F32))
    imp = jnp.concatenate(imps + [jnp.zeros((8 - NSA_GROUPS, ov.shape[1]), F32)], axis=0)
    score = _block_scores(imp, jnp.full((8, 1), qpos, jnp.int32), 1)
    _, idx = _top_k_mask(score, N_SEL, 1, idx_out=True)
    idx_ref[...] = idx


def _nsa_dec_select(q3, kcv, ov, qpos):
    db, nh, _ = q3.shape
    nc = kcv.shape[2]
    return pl.pallas_call(
        functools.partial(_nsa_dec_select_kernel, qpos=qpos),
        grid=(db,),
        in_specs=[pl.BlockSpec((None, nh, HEAD_DIM), lambda b: (b, 0, 0)),
                  pl.BlockSpec((None, NSA_GROUPS, nc, 2 * HEAD_DIM), lambda b: (b, 0, 0, 0)),
                  _const_spec(ov.shape)],
        out_specs=[pl.BlockSpec((None, nh, HEAD_DIM), lambda b: (b, 0, 0)),
                   pl.BlockSpec((None, 8, LANES), lambda b: (b, 0, 0))],
        out_shape=[jax.ShapeDtypeStruct((db, nh, HEAD_DIM), F32), jax.ShapeDtypeStruct((db, 8, LANES), jnp.int32)],
        compiler_params=_cparams(("arbitrary",)),
        name="nsa_dec_select",
    )(q3, kcv, ov)


def _nsa_dec_attend_kernel(idx_ref, pt_ref, *refs, past):
    k_refs = refs[:N_SEL]
    v_refs = refs[N_SEL:2 * N_SEL]
    (q_ref, nk_ref, nv_ref, nwk_ref, nwv_ref, wk_ref, wv_ref, ocmp_ref, gt_ref, o_ref, wko_ref, wvo_ref) = refs[2 * N_SEL:]
    b = pl.program_id(0)
    g = pl.program_id(1)
    win = wk_ref.shape[1]
    q = q_ref[...]
    lane_blk = lax.broadcasted_iota(jnp.int32, (1, SLC_BLK), 1)

    s_parts, v_parts, pos_parts = [], [], []
    for i in range(N_SEL):
        blk = idx_ref[b, g * N_SEL + i]
        first_half = blk % (PAGE_SIZE // SLC_BLK) == 0
        kpos = blk * SLC_BLK + lane_blk
        from_new = kpos >= past
        k_t, v_t = k_refs[i][...], v_refs[i][...]
        k_t = jnp.where(from_new, nk_ref[...], jnp.where(first_half, k_t[:, :SLC_BLK], k_t[:, SLC_BLK:]))
        v_t = jnp.where(from_new, nv_ref[...], jnp.where(first_half, v_t[:, :SLC_BLK], v_t[:, SLC_BLK:]))
        s_parts.append(_dot(q, k_t))
        v_parts.append(v_t)
        pos_parts.append(kpos)
    p = _masked_softmax(jnp.concatenate(s_parts, axis=1), jnp.concatenate(pos_parts, axis=1) <= past)
    o_slc = _dot_nt(p, jnp.concatenate(v_parts, axis=1))

    lane = lax.broadcasted_iota(jnp.int32, (1, win), 1)
    wk_new = jnp.where(lane == win - 1, nwk_ref[...], pltpu.roll(wk_ref[...], win - 1, 1))
    wv_new = jnp.where(lane == win - 1, nwv_ref[...], pltpu.roll(wv_ref[...], win - 1, 1))
    wko_ref[...] = wk_new
    wvo_ref[...] = wv_new
    p = _masked_softmax(_dot(q, wk_new), (win - 1 - lane) < WINDOW)
    o_win = _dot_nt(p, wv_new)

    gt = gt_ref[...]
    o_ref[...] = gt[:, 0:1] * ocmp_ref[...] + gt[:, 1:2] * o_slc + gt[:, 2:3] * o_win


def _nsa_dec_attend(idx, page_table, pool_k, pool_v, q4, new_cols, win_k, win_v, o_cmp4, gt4, past):
    db, n_pages = page_table.shape
    win = win_k.shape[3]
    blocks_per_page = PAGE_SIZE // SLC_BLK
    assert blocks_per_page == 2

    def blk_map(i):
        def index(b, g, idx_s, pt_s):
            page = jnp.minimum(idx_s[b, g * N_SEL + i] // blocks_per_page, n_pages - 1)
            return pt_s[b, page], g, 0, 0
        return index

    sel_specs = [pl.BlockSpec((None, None, HEAD_DIM, PAGE_SIZE), blk_map(i)) for i in range(N_SEL)]

    def per_bg(shape):
        return pl.BlockSpec((None, None) + shape, lambda b, g, *_: (b, g, 0, 0))

    grid_spec = pltpu.PrefetchScalarGridSpec(
        num_scalar_prefetch=2,
        grid=(db, NSA_GROUPS),
        in_specs=sel_specs + sel_specs + [per_bg((HPG, HEAD_DIM))] + [per_bg((HEAD_DIM, 1))] * 4
        + [per_bg((HEAD_DIM, win))] * 2 + [per_bg((HPG, HEAD_DIM)), per_bg((HPG, 3))],
        out_specs=[per_bg((HPG, HEAD_DIM)), per_bg((HEAD_DIM, win)), per_bg((HEAD_DIM, win))],
    )
    return pl.pallas_call(
        functools.partial(_nsa_dec_attend_kernel, past=past),
        grid_spec=grid_spec,
        out_shape=[jax.ShapeDtypeStruct((db, NSA_GROUPS, HPG, HEAD_DIM), F32),
                   jax.ShapeDtypeStruct(win_k.shape, F32), jax.ShapeDtypeStruct(win_v.shape, F32)],
        compiler_params=_cparams(("arbitrary", "arbitrary")),
        name="nsa_dec_attend",
    )(idx, page_table, *([pool_k] * N_SEL), *([pool_v] * N_SEL), q4, *new_cols, win_k, win_v, o_cmp4, gt4)


def _forget_lower_bound(lb_raw):
    m = jnp.max(lb_raw, axis=0, keepdims=True)
    e = jnp.exp(lb_raw - m)
    return e[0:1] / jnp.sum(e, axis=0, keepdims=True)


def _hgrn_out(o, hg, norm_g):
    o = o * lax.rsqrt(jnp.mean(o * o, axis=-1, keepdims=True) + EPS) * norm_g
    return o * (hg * _sigmoid(hg))


def _hgrn_chunk_common(q, hf, v, lb, tri, st, b_s, k_s, v_s):
    heads = range(len(q))
    lc = q[0].shape[0]
    f = [lb[h] + (1.0 - lb[h]) * _sigmoid(hf[h]) for h in heads]
    k = [1.0 - f[h] for h in heads]
    parts = [p for h in heads for p in _split3(jnp.log(f[h]))]
    csum = jnp.dot(tri, jnp.concatenate(parts, axis=1), preferred_element_type=F32)
    bcum = [csum[:, 3 * h * HG_DIM:(3 * h + 1) * HG_DIM] + csum[:, (3 * h + 1) * HG_DIM:(3 * h + 2) * HG_DIM]
            + csum[:, (3 * h + 2) * HG_DIM:(3 * h + 3) * HG_DIM] for h in heads]
    for h in heads:
        b_s[h][...] = bcum[h]
        k_s[h][...] = k[h]
        v_s[h][...] = v[h]

    o_inter = [_dot_nt(q[h] * jnp.exp(bcum[h]), st[h]) for h in heads]
    span = None
    for i in range(lc // HG_SUB):
        r0 = i * HG_SUB
        for h in heads:
            d_i = bcum[h][r0:r0 + 1] - bcum[h][r0 + HG_SUB - 1:r0 + HG_SUB]
            span = d_i if span is None else jnp.maximum(span, d_i)
    st_new = [st[h] * jnp.exp(bcum[h][lc - 1:lc]) + _dot(v[h].T, k[h] * jnp.exp(bcum[h][lc - 1:lc] - bcum[h]))
              for h in heads]
    return o_inter, k, bcum, st_new, span


def _hgrn_intra_matmul(o_inter, q, k, bcum, v, o_s):
    heads = range(len(q))
    lc = q[0].shape[0]
    n_sub = lc // HG_SUB
    row_sub = lax.broadcasted_iota(jnp.int32, (lc, HG_DIM), 0) >> HG_SUB_LOG2
    causal = lax.broadcasted_iota(jnp.int32, (lc, lc), 1) <= lax.broadcasted_iota(jnp.int32, (lc, lc), 0)
    for h in heads:
        ref = [bcum[h][j * HG_SUB:j * HG_SUB + 1] for j in range(n_sub)]
        ref_rows = jnp.concatenate([jnp.broadcast_to(r, (HG_SUB, HG_DIM)) for r in ref], axis=0)
        q_t = q[h] * jnp.exp(bcum[h] - ref_rows)
        k_t = k[h] * jnp.exp(ref_rows - bcum[h])
        q_cols = [jnp.where(row_sub >= j, q_t * jnp.exp(jnp.minimum(ref_rows - ref[j], 0.0)), 0.0).astype(BF16)
                  for j in range(n_sub)]
        k_cols = [jnp.where(row_sub == j, k_t, 0.0).astype(BF16) for j in range(n_sub)]
        a = _dot_nt(jnp.concatenate(q_cols, axis=1), jnp.concatenate(k_cols, axis=1))
        o_s[h][...] = o_inter[h] + _dot(jnp.where(causal, a, 0.0), v[h])


def _hgrn_intra_direct(o_inter, q, k, bcum, v, b_s, k_s, v_s, o_s):
    lc = q.shape[0]
    t_idx = lax.broadcasted_iota(jnp.int32, (SUBLANES, 1), 0)
    for i in range(lc // HG_SUB):
        r0 = i * HG_SUB
        o_i = o_inter[r0:r0 + HG_SUB]
        if i > 0:
            ref_b = bcum[r0:r0 + 1]
            a = _dot_nt(q[r0:r0 + HG_SUB] * jnp.exp(bcum[r0:r0 + HG_SUB] - ref_b), k[:r0] * jnp.exp(ref_b - bcum[:r0]))
            o_i = o_i + _dot(a, v[:r0])
        for t0 in range(r0, r0 + HG_SUB, SUBLANES):
            b_t, q_t = bcum[t0:t0 + SUBLANES], q[t0:t0 + SUBLANES]
            acc = o_i[t0 - r0:t0 - r0 + SUBLANES]
            for s in range(r0, t0 + SUBLANES):
                d = b_t - b_s[s:s + 1, :]
                if s >= t0:
                    d = jnp.where(t_idx >= s - t0, d, NEG)
                w = jnp.exp(d) * q_t * k_s[s:s + 1, :]
                acc = acc + jnp.sum(w, axis=-1, keepdims=True) * v_s[s:s + 1, :]
            o_s[t0:t0 + SUBLANES, :] = acc


def _hgrn_prompt_kernel(hz_ref, lb_ref, ng_ref, tri_ref, o_ref, st_ref, state_t, b_s, k_s, v_s, o_s):
    c = pl.program_id(1)
    n_c = pl.num_programs(1)

    @pl.when(c == 0)
    def _():
        state_t[...] = jnp.zeros_like(state_t)

    lb_all = _forget_lower_bound(lb_ref[...])
    tri = tri_ref[...]

    def seg(j, h):
        return hz_ref[:, j * HG_WIDTH + h * HG_DIM:j * HG_WIDTH + (h + 1) * HG_DIM]

    heads = range(HG_HEADS)
    q, v = [seg(0, h) for h in heads], [seg(2, h) for h in heads]
    scratch = [[s.at[h] for h in heads] for s in (b_s, k_s, v_s, o_s)]
    o_inter, k, bcum, st_new, span = _hgrn_chunk_common(
        q, [seg(1, h) for h in heads], v, [lb_all[:, h * HG_DIM:(h + 1) * HG_DIM] for h in heads], tri,
        [state_t[h] for h in heads], *scratch[:3])
    for h in heads:
        state_t[h] = st_new[h]
    direct = jnp.max(span) > HG_SAFE_DECAY

    @pl.when(jnp.logical_not(direct))
    def _():
        _hgrn_intra_matmul(o_inter, q, k, bcum, v, scratch[3])

    @pl.when(direct)
    def _():
        for h in heads:
            _hgrn_intra_direct(o_inter[h], q[h], k[h], bcum[h], v[h], *(s[h] for s in scratch))

    for h in range(HG_HEADS):
        o_ref[:, h * HG_DIM:(h + 1) * HG_DIM] = _hgrn_out(o_s[h], seg(3, h), ng_ref[...]).astype(BF16)

    @pl.when(c == n_c - 1)
    def _():
        for h in range(HG_HEADS):
            st_ref[h] = state_t[h].T


def _hgrn_prompt(hz3, hg_lb, norm_g, *, lc):
    b, t, w = hz3.shape
    tri = jnp.tril(jnp.ones((lc, lc), F32)).astype(BF16)
    return pl.pallas_call(
        _hgrn_prompt_kernel,
        grid=(b, t // lc),
        in_specs=[pl.BlockSpec((None, lc, w), lambda bi, c: (bi, c, 0)), _const_spec(hg_lb.shape),
                  _const_spec((1, HG_DIM)), _const_spec((lc, lc))],
        out_specs=[pl.BlockSpec((None, lc, HG_WIDTH), lambda bi, c: (bi, c, 0)),
                   pl.BlockSpec((None, HG_HEADS, HG_DIM, HG_DIM), lambda bi, c: (bi, 0, 0, 0))],
        out_shape=[jax.ShapeDtypeStruct((b, t, HG_WIDTH), BF16),
                   jax.ShapeDtypeStruct((b, HG_HEADS, HG_DIM, HG_DIM), F32)],
        scratch_shapes=[pltpu.VMEM((HG_HEADS, HG_DIM, HG_DIM), F32)] + [pltpu.VMEM((HG_HEADS, lc, HG_DIM), F32)] * 4,
        compiler_params=_cparams(("arbitrary", "arbitrary")),
        name="hgrn_prompt",
    )(hz3, hg_lb, norm_g, tri)


def _hgrn_dec_kernel(hz_ref, s0_ref, lb_ref, ng_ref, o_ref, s_ref):
    hz = hz_ref[...]
    lb_all = _forget_lower_bound(lb_ref[...])
    eye = lax.broadcasted_iota(jnp.int32, (HG_DIM, HG_DIM), 0) == lax.broadcasted_iota(jnp.int32, (HG_DIM, HG_DIM), 1)

    def column(row):
        return jnp.sum(jnp.where(eye, row, 0.0), axis=1, keepdims=True)

    outs = []
    for h in range(HG_HEADS):
        def seg(j):
            return hz[:, j * HG_WIDTH + h * HG_DIM:j * HG_WIDTH + (h + 1) * HG_DIM]

        lb = lb_all[:, h * HG_DIM:(h + 1) * HG_DIM]
        f = lb + (1.0 - lb) * _sigmoid(seg(1))
        q, k, v = seg(0), 1.0 - f, seg(2)
        s0 = s0_ref[h]
        s_ref[h] = column(f) * s0 + column(k) * v
        o = jnp.sum(column(q * f) * s0, axis=0, keepdims=True) + jnp.sum(q * k, axis=-1, keepdims=True) * v
        outs.append(_hgrn_out(o, seg(3), ng_ref[...]))
    o_ref[...] = jnp.concatenate(outs, axis=1)


def _hgrn_dec(hz, s0, hg_lb, norm_g):
    db = hz.shape[0]
    return pl.pallas_call(
        _hgrn_dec_kernel,
        grid=(db,),
        in_specs=[pl.BlockSpec((None, 1, hz.shape[1]), lambda b: (b, 0, 0)),
                  pl.BlockSpec((None, HG_HEADS, HG_DIM, HG_DIM), lambda b: (b, 0, 0, 0)),
                  _const_spec(hg_lb.shape), _const_spec((1, HG_DIM))],
        out_specs=[pl.BlockSpec((None, 1, HG_WIDTH), lambda b: (b, 0, 0)),
                   pl.BlockSpec((None, HG_HEADS, HG_DIM, HG_DIM), lambda b: (b, 0, 0, 0))],
        out_shape=[jax.ShapeDtypeStruct((db, 1, HG_WIDTH), F32), jax.ShapeDtypeStruct(s0.shape, F32)],
        compiler_params=_cparams(("arbitrary",)),
        name="hgrn_dec",
    )(hz.reshape(db, 1, -1), s0, hg_lb, norm_g)


def _merge_kernel(on_ref, oh_ref, sm_ref, x_ref, g1_ref, sh_ref, sc_ref, ng_ref, wn_ref, wh_ref, wo_ref,
                  x1_ref, h2_ref):
    d = x_ref.shape[1]
    sm = sm_ref[...]
    mix = sm[:, :d] * _dot(on_ref[...], wn_ref[...]) + sm[:, d:] * _dot(oh_ref[...], wh_ref[...])
    x1 = x_ref[...] + g1_ref[...] * _dot(mix, wo_ref[...])
    x1_ref[...] = x1
    xn = x1 * lax.rsqrt(jnp.mean(x1 * x1, axis=-1, keepdims=True) + EPS) * ng_ref[...]
    h2_ref[...] = (xn * (1.0 + sc_ref[...]) + sh_ref[...]).astype(BF16)


def _merge(o_nsa, o_hg, sm, x2d, g1, sh2, sc2, norm_g, w_br_nsa, w_br_hg, w_out, *, tm, tiles_per_mod):
    r, d = x2d.shape
    rows_mod = g1.shape[1]
    mod_spec = pl.BlockSpec((None, rows_mod, d), lambda i: (i // tiles_per_mod, 0, 0))

    def row_spec(width):
        return pl.BlockSpec((tm, width), lambda i: (i, 0))

    return pl.pallas_call(
        _merge_kernel,
        grid=(r // tm,),
        in_specs=[row_spec(o_nsa.shape[1]), row_spec(o_hg.shape[1]), row_spec(sm.shape[1]), row_spec(d),
                  mod_spec, mod_spec, mod_spec, _const_spec((1, d)),
                  _const_spec(w_br_nsa.shape), _const_spec(w_br_hg.shape), _const_spec(w_out.shape)],
        out_specs=[row_spec(d), row_spec(d)],
        out_shape=[jax.ShapeDtypeStruct((r, d), F32), jax.ShapeDtypeStruct((r, d), BF16)],
        compiler_params=_cparams(("arbitrary",)),
        name="merge",
    )(o_nsa, o_hg, sm, x2d, g1, sh2, sc2, norm_g, w_br_nsa, w_br_hg, w_out)


FF_CHUNKS = 2


def _ffn_core(h2, x1, g2, wu_ref, cw_ref, cb_ref, wd_ref, prev_rows):
    d_ff = wd_ref.shape[0]
    ch = d_ff // FF_CHUNKS
    acc = None
    us = []
    for ci in range(FF_CHUNKS):
        c0, c1 = ci * ch, (ci + 1) * ch
        u = jnp.dot(h2, wu_ref[:, c0:c1], preferred_element_type=F32)
        v = jnp.dot(h2, wu_ref[:, d_ff + c0:d_ff + c1], preferred_element_type=F32)
        u_m1, u_m2 = prev_rows(u, c0, c1)
        y = cb_ref[:, c0:c1] + cw_ref[0:1, c0:c1] * u_m2 + cw_ref[1:2, c0:c1] * u_m1 + cw_ref[2:3, c0:c1] * u
        part = _dot(y * _sigmoid(y) * v, wd_ref[c0:c1, :])
        acc = part if acc is None else acc + part
        us.append(u)
    return x1 + g2 * acc, us


def _ffn_prompt_kernel(h2_ref, x1_ref, g2_ref, wu_ref, cw_ref, cb_ref, wd_ref, y_ref, tail_ref, carry):
    ti = pl.program_id(1)
    tm = h2_ref.shape[0]

    @pl.when(ti == 0)
    def _():
        carry[...] = jnp.zeros_like(carry)

    row = lax.broadcasted_iota(jnp.int32, (tm, 1), 0)

    def prev_rows(u, c0, c1):
        last = carry[:, c0:c1]
        m1 = jnp.where(row == 0, last[7:8], pltpu.roll(u, 1, 0))
        m2 = jnp.where(row == 0, last[6:7], jnp.where(row == 1, last[7:8], pltpu.roll(u, 2, 0)))
        return m1, m2

    out, us = _ffn_core(h2_ref[...], x1_ref[...], g2_ref[...], wu_ref, cw_ref, cb_ref, wd_ref, prev_rows)
    y_ref[...] = out
    ch = us[0].shape[1]
    for ci, u in enumerate(us):
        carry[:, ci * ch:(ci + 1) * ch] = u[tm - 8:, :]
    tail_ref[...] = carry[...]


def _ffn_prompt(h2, x1, g2, w_up, conv_w8, conv_b, w_down, *, b, tm):
    r, d = x1.shape
    d_ff = w_down.shape[0]
    tpb = r // b // tm

    def row_spec(width):
        return pl.BlockSpec((tm, width), lambda bi, i: (bi * tpb + i, 0))

    return pl.pallas_call(
        _ffn_prompt_kernel,
        grid=(b, tpb),
        in_specs=[row_spec(d), row_spec(d), pl.BlockSpec((None, 1, d), lambda bi, i: (bi, 0, 0)),
                  _resident_spec(w_up.shape), _const_spec(conv_w8.shape), _const_spec(conv_b.shape),
                  _resident_spec(w_down.shape)],
        out_specs=[row_spec(d), pl.BlockSpec((None, 8, d_ff), lambda bi, i: (bi, 0, 0))],
        out_shape=[jax.ShapeDtypeStruct((r, d), F32), jax.ShapeDtypeStruct((b, 8, d_ff), F32)],
        scratch_shapes=[pltpu.VMEM((8, d_ff), F32)],
        compiler_params=_cparams(("arbitrary", "arbitrary")),
        name="ffn_prompt",
    )(h2, x1, g2, w_up, conv_w8, conv_b, w_down)


def _ffn_dec_kernel(h2_ref, x1_ref, g2_ref, wu_ref, cw_ref, cb_ref, wd_ref, m1_ref, m2_ref, y_ref, u_ref):
    def prev_rows(u, c0, c1):
        return m1_ref[:, c0:c1], m2_ref[:, c0:c1]

    out, us = _ffn_core(h2_ref[...], x1_ref[...], g2_ref[...], wu_ref, cw_ref, cb_ref, wd_ref, prev_rows)
    y_ref[...] = out
    u_ref[...] = jnp.concatenate(us, axis=1)


def _ffn_dec(h2, x1, g2, w_up, conv_w8, conv_b, w_down, u_m1, u_m2):
    r, d = x1.shape
    d_ff = w_down.shape[0]
    args = (h2, x1, g2, w_up, conv_w8, conv_b, w_down, u_m1, u_m2)
    return pl.pallas_call(
        _ffn_dec_kernel,
        grid=(1,),
        in_specs=[_const_spec(a.shape) for a in args],
        out_specs=[_const_spec((r, d)), _const_spec((r, d_ff))],
        out_shape=[jax.ShapeDtypeStruct((r, d), F32), jax.ShapeDtypeStruct((r, d_ff), F32)],
        compiler_params=_cparams(("arbitrary",)),
        name="ffn_dec",
    )(*args)


def _pack_w_in(w_in):
    cuts = np.cumsum([NSA_WIDTH] + [KV_WIDTH] * 6 + [3 * NSA_HEADS] + [HG_WIDTH] * 4)
    q, kv, gates, rest = w_in[:, :cuts[0]], w_in[:, cuts[0]:cuts[6]], w_in[:, cuts[6]:cuts[7]], w_in[:, cuts[7]:]
    hz, m = rest[:, :4 * HG_WIDTH], rest[:, 4 * HG_WIDTH:]
    per_group = 3 * HPG
    pad = jnp.zeros((w_in.shape[0], LANES - per_group), w_in.dtype)
    gate_cols = []
    for g in range(NSA_GROUPS):
        gate_cols += [gates[:, g * per_group:(g + 1) * per_group], pad]
    return jnp.concatenate([q, kv] + gate_cols + [hz, m], axis=1).astype(BF16)


def _rope_tables(pos):
    half = ROT_DIM // 2
    inv = ROPE_THETA ** (-jnp.arange(0, ROT_DIM, 2, dtype=F32) / ROT_DIM)
    ang = pos.astype(F32)[:, None] * inv[None, :]
    cos, sin = jnp.cos(ang), jnp.sin(ang)
    n = pos.shape[0]
    ones = jnp.ones((n, HEAD_DIM - ROT_DIM), F32)
    zeros = jnp.zeros((n, HEAD_DIM - ROT_DIM), F32)
    z8 = jnp.zeros((n, half), F32)
    c = jnp.concatenate([cos, cos, ones], axis=1)
    s_lo = jnp.concatenate([-sin, z8, zeros], axis=1)
    s_hi = jnp.concatenate([z8, sin, zeros], axis=1)
    return tuple(jnp.concatenate([t, t], axis=1) for t in (c, s_lo, s_hi))


def _cmp_weights(cmp_w1, cmp_w2, cmp_pe):
    n_half = CMP_BLK // CMP_STRIDE
    eye = jnp.eye(NSA_GROUPS, dtype=F32)
    w1s, w2s, pes = [], [], []
    for j in range(2):
        w = cmp_w1[j].reshape(n_half, CMP_STRIDE, HEAD_DIM, HEAD_DIM)
        big = jnp.einsum('rsdh,ge->sgdreh', w, eye)
        w1s.append(big.reshape(CMP_STRIDE * KV_WIDTH, n_half * KV_WIDTH))
        w2s.append(jnp.einsum('dh,ge->gdeh', cmp_w2[j], eye).reshape(KV_WIDTH, KV_WIDTH))
        pe = cmp_pe[j].reshape(n_half, CMP_STRIDE, 1, HEAD_DIM)
        pe = jnp.broadcast_to(pe, (n_half, CMP_STRIDE, NSA_GROUPS, HEAD_DIM)).reshape(n_half, -1)
        pes.append(jnp.concatenate([pe, jnp.zeros((8 - n_half, pe.shape[1]), F32)], axis=0))
    return jnp.stack(w1s).astype(BF16), jnp.stack(w2s).astype(BF16), jnp.stack(pes).astype(BF16)


def _overlap(nc_pad, nc, ns, ns_pad):
    start = np.arange(nc_pad) * CMP_STRIDE
    end = start + CMP_BLK - 1
    s0 = np.arange(ns_pad) * SLC_BLK
    s1 = s0 + SLC_BLK - 1
    m = (start[:, None] <= s1[None, :]) & (end[:, None] >= s0[None, :])
    m &= (np.arange(nc_pad) < nc)[:, None] & (np.arange(ns_pad) < ns)[None, :]
    return jnp.asarray(m, BF16)


def kernel(x_prompt, x_sample, cache_cmp_k, cache_cmp_v, cache_slc_k, cache_slc_v, cache_win_k, cache_win_v,
           state_hgrn, state_ffn_conv, page_table, c_prompt, c_sample, ada_w, ada_b, norm_attn, norm_ffn, w_in,
           q_gain, k_gain, cmp_w1, cmp_w2, cmp_pe, hg_lb, hg_norm, w_br_nsa, w_br_hg, w_out, w_up, conv_w,
           conv_b, w_down):
    depth = ada_w.shape[0]
    assert depth == 1, "single-layer trunk only"
    b, t, d = x_prompt.shape
    db, tn, _ = x_sample.shape
    assert tn == 1
    n_pages = page_table.shape[1]
    past = n_pages * PAGE_SIZE
    d_ff = w_down.shape[1]
    tq = 256
    tm = 256
    lc = 128
    assert t % tq == 0 and t >= WINDOW and WINDOW % tq == 0 and n_pages % PAGES_PER_STEP == 0
    assert cache_win_k.shape[2] == WINDOW

    w_pack = _pack_w_in(w_in[0])
    qg = jnp.tile(q_gain[0], NSA_HEADS).reshape(1, NSA_WIDTH)
    kg = jnp.concatenate([jnp.tile(k_gain[0], (1, NSA_GROUPS)), jnp.zeros((5, KV_WIDTH), F32)], axis=0)
    hd = np.arange(NSA_WIDTH) // HEAD_DIM
    bd = jnp.asarray(hd[:, None] == hd[None, :], BF16)
    w1c, w2c, pe_x = _cmp_weights(cmp_w1[0], cmp_w2[0], cmp_pe[0])
    conv_w8 = jnp.concatenate([conv_w[0], jnp.zeros((8 - CONV_W, d_ff), F32)], axis=0)
    conv_b2 = conv_b[0].reshape(1, d_ff)
    wbn, wbh, wo = w_br_nsa[0].astype(BF16), w_br_hg[0].astype(BF16), w_out[0].astype(BF16)
    wu, wd = w_up[0].astype(BF16), w_down[0].astype(BF16)
    norm_a, norm_f, hg_n = norm_attn[0].reshape(1, d), norm_ffn[0].reshape(1, d), hg_norm[0].reshape(1, HG_DIM)

    n_c = b + db
    c_all = jnp.concatenate([c_prompt, c_sample, jnp.zeros((-n_c % 8, d), F32)], axis=0)
    ada = _ada(c_all, ada_w[0], ada_b[0])
    mods_p = [ada[:b, i * d:(i + 1) * d].reshape(b, 1, d) for i in range(6)]
    mods_s = [ada[b:n_c, i * d:(i + 1) * d].reshape(1, db, d) for i in range(6)]

    tpb = t // tm
    (q, kc_r, vc_r, ks_r, vs_r, kw_r, vw_r, kcb, vcb, kvb, gt, hz, sm) = _inproj(
        x_prompt.reshape(b * t, d), mods_p[0], mods_p[1], norm_a, w_pack, qg, kg, _rope_tables(jnp.arange(t)), bd,
        tm=tm, tiles_per_mod=tpb, rope_tiles=tpb)

    nsub = t // CMP_STRIDE
    pk = _mm(kcb.reshape(b * nsub, CMP_STRIDE * KV_WIDTH), w1c[0], 512, "cmp_proj_k").reshape(b, nsub, -1)
    pv = _mm(vcb.reshape(b * nsub, CMP_STRIDE * KV_WIDTH), w1c[1], 512, "cmp_proj_v").reshape(b, nsub, -1)
    kcv = _cmp_combine(pk, pv, pe_x, w1c, w2c)
    ov = _overlap(nsub, nsub - 1, t // SLC_BLK, LANES).T
    o_nsa = _nsa_prompt(q.reshape(b, t, -1), kcv, kvb.reshape(b, t, -1), gt.reshape(b, t, -1), ov, tq=tq)

    o_hg, st_p = _hgrn_prompt(hz.reshape(b, t, -1), hg_lb, hg_n, lc=lc)

    x1, h2 = _merge(o_nsa.reshape(b * t, -1), o_hg.reshape(b * t, -1), sm, x_prompt.reshape(b * t, d),
                    mods_p[2], mods_p[3], mods_p[4], norm_f, wbn, wbh, wo, tm=tm, tiles_per_mod=tpb)
    y_p, tail = _ffn_prompt(h2, x1, mods_p[5], wu, conv_w8, conv_b2, wd, b=b, tm=tm)

    def rows5(a):
        return a.reshape(1, b, t, NSA_GROUPS, HEAD_DIM)

    w = min(WINDOW, t)
    out_p = (rows5(kc_r), rows5(vc_r), rows5(ks_r), rows5(vs_r), rows5(kw_r)[:, :, -w:], rows5(vw_r)[:, :, -w:],
             st_p[None], tail[None, :, 8 - (CONV_W - 1):, :])

    pos_s = jnp.full((db,), past, jnp.int32)
    (q_s, kc_s, vc_s, ks_s, vs_s, kw_s, vw_s, _, _, _, gt_s, hz_s, sm_s) = _inproj(
        x_sample.reshape(db, d), mods_s[0], mods_s[1], norm_a, w_pack, qg, kg, _rope_tables(pos_s), bd,
        tm=db, tiles_per_mod=1, rope_tiles=1)

    def feature_major(a):
        return jnp.transpose(a, (0, 2, 3, 1))

    assert (past + tn) // CMP_STRIDE == past // CMP_STRIDE
    pk_s, pv_s = _cmp_proj_paged(feature_major(cache_cmp_k[0]).reshape(-1, KV_WIDTH, PAGE_SIZE),
                                 feature_major(cache_cmp_v[0]).reshape(-1, KV_WIDTH, PAGE_SIZE), page_table, w1c)
    kcv_s = _cmp_combine(pk_s, pv_s, pe_x, w1c, w2c)
    nsub_s = past // CMP_STRIDE
    ns_s = -(-(past + tn) // SLC_BLK)
    ns_pad = -(-ns_s // LANES) * LANES
    ov_s = _overlap(nsub_s, nsub_s - 1, ns_s, ns_pad)
    o_cmp_s, idx8 = _nsa_dec_select(q_s.reshape(db, NSA_HEADS, HEAD_DIM), kcv_s, ov_s, past)
    idx = idx8[:, :NSA_GROUPS, :N_SEL].reshape(db, NSA_GROUPS * N_SEL)
    new_cols = [a.reshape(db, NSA_GROUPS, HEAD_DIM, 1) for a in (ks_s, vs_s, kw_s, vw_s)]
    per_group = 3 * HPG
    gt4 = jnp.stack([gt_s[:, g * LANES:g * LANES + per_group] for g in range(NSA_GROUPS)], axis=1)
    o_nsa_s, wk_new, wv_new = _nsa_dec_attend(
        idx, page_table, feature_major(cache_slc_k[0]), feature_major(cache_slc_v[0]),
        q_s.astype(F32).reshape(db, NSA_GROUPS, HPG, HEAD_DIM), new_cols,
        feature_major(cache_win_k[0]), feature_major(cache_win_v[0]),
        o_cmp_s.reshape(db, NSA_GROUPS, HPG, HEAD_DIM), gt4.reshape(db, NSA_GROUPS, HPG, 3), past)
    wk_new, wv_new = (jnp.transpose(a, (0, 3, 1, 2)) for a in (wk_new, wv_new))

    o_hg_s, st_s = _hgrn_dec(hz_s, state_hgrn[0], hg_lb, hg_n)
    x1_s, h2_s = _merge(o_nsa_s.reshape(db, NSA_WIDTH), o_hg_s.reshape(db, HG_WIDTH), sm_s, x_sample.reshape(db, d),
                        mods_s[2], mods_s[3], mods_s[4], norm_f, wbn, wbh, wo, tm=db, tiles_per_mod=1)
    buf = state_ffn_conv[0]
    y_s, u_s = _ffn_dec(h2_s, x1_s, mods_s[5].reshape(db, d), wu, conv_w8, conv_b2, wd, buf[:, 1], buf[:, 0])
    conv_s = jnp.stack([buf[:, 1], u_s], axis=1)

    def rows5s(a):
        return a.reshape(1, db, tn, NSA_GROUPS, HEAD_DIM)

    out_s = (rows5s(kc_s), rows5s(vc_s), rows5s(ks_s), rows5s(vs_s),
             wk_new.reshape(1, db, WINDOW, NSA_GROUPS, HEAD_DIM), wv_new.reshape(1, db, WINDOW, NSA_GROUPS, HEAD_DIM),
             st_s[None], conv_s[None])
    return (y_p.reshape(b, t, d), y_s.reshape(db, tn, d)) + out_p + out_s
```

```python
import functools

import numpy as np
import jax
import jax.numpy as jnp
from jax import lax
from jax.experimental import pallas as pl
from jax.experimental.pallas import tpu as pltpu

F32 = jnp.float32
BF16 = jnp.bfloat16

HEAD_DIM = 64
NSA_HEADS = 8
NSA_GROUPS = 2
HPG = NSA_HEADS // NSA_GROUPS
ROT_DIM = HEAD_DIM // 4
ROPE_THETA = 500000.0
CMP_BLK = 32
CMP_STRIDE = 16
SLC_BLK = 64
N_SEL = 16
WINDOW = 512
PAGE_SIZE = 128
HG_HEADS = 4
HG_DIM = 128
HG_SUB_LOG2 = 4
HG_SUB = 1 << HG_SUB_LOG2
HG_SAFE_DECAY = 60.0
SUBLANES = 8
SOFTMAX_VREGS = 32
CONV_W = 3
EPS = 1e-6
NEG = -1e30
SEL_BIAS = -1e9
LOG2E = 1.4426950408889634
LANES = 128
VMEM_LIMIT = 56 * 1024 * 1024

NSA_WIDTH = NSA_HEADS * HEAD_DIM
KV_WIDTH = NSA_GROUPS * HEAD_DIM
HG_WIDTH = HG_HEADS * HG_DIM
GRP_Q = HPG * HEAD_DIM

SEG_Q = 0
SEG_KV = SEG_Q + NSA_WIDTH
SEG_G = SEG_KV + 6 * KV_WIDTH
SEG_H = SEG_G + NSA_GROUPS * LANES
SEG_M = SEG_H + 4 * HG_WIDTH


def _cparams(sem):
    return pltpu.CompilerParams(dimension_semantics=sem, vmem_limit_bytes=VMEM_LIMIT)


def _dot(a, b):
    return jnp.dot(a.astype(BF16), b.astype(BF16), preferred_element_type=F32)


def _dot_nt(a, b):
    return lax.dot_general(a.astype(BF16), b.astype(BF16), (((1,), (1,)), ((), ())), preferred_element_type=F32)


def _split2(x):
    hi = x.astype(BF16)
    lo = (x - hi.astype(F32)).astype(BF16)
    return hi, lo


def _split3(x):
    hi = x.astype(BF16)
    r = x - hi.astype(F32)
    mid = r.astype(BF16)
    lo = (r - mid.astype(F32)).astype(BF16)
    return hi, mid, lo


def _dot_x3(a, b):
    ah, al = _split2(a)
    bh, bl = _split2(b)
    return (jnp.dot(ah, bh, preferred_element_type=F32) + jnp.dot(al, bh, preferred_element_type=F32)
            + jnp.dot(ah, bl, preferred_element_type=F32))


def _sigmoid(x):
    return 1.0 / (1.0 + jnp.exp(-x))


def _const_spec(shape):
    nd = len(shape)
    return pl.BlockSpec(shape, lambda *_: (0,) * nd)


def _resident_spec(shape):
    nd = len(shape)
    return pl.BlockSpec(shape, lambda *_: (0,) * nd, pipeline_mode=pl.Buffered(1))


def _ada_kernel(c_ref, w_ref, b_ref, o_ref):
    o_ref[...] = _dot_x3(c_ref[...], w_ref[...]) + b_ref[...]


def _ada(c_all, w, b):
    r, d = c_all.shape
    n = w.shape[1]
    tn = 1536
    return pl.pallas_call(
        _ada_kernel,
        grid=(n // tn,),
        in_specs=[_const_spec((r, d)), pl.BlockSpec((d, tn), lambda j: (0, j)), pl.BlockSpec((1, tn), lambda j: (0, j))],
        out_specs=pl.BlockSpec((r, tn), lambda j: (0, j)),
        out_shape=jax.ShapeDtypeStruct((r, n), F32),
        compiler_params=_cparams(("arbitrary",)),
        name="ada",
    )(c_all, w, b.reshape(1, n))


def _rope(x, c, s_lo, s_hi):
    width = x.shape[1]
    reps = width // LANES
    if reps > 1:
        c = jnp.concatenate([c] * reps, axis=1)
        s_lo = jnp.concatenate([s_lo] * reps, axis=1)
        s_hi = jnp.concatenate([s_hi] * reps, axis=1)
    half = ROT_DIM // 2
    return x * c + pltpu.roll(x, half, 1) * s_hi + pltpu.roll(x, width - half, 1) * s_lo


def _head_rms(z, bd):
    hi, lo = _split2(z * z)
    ss = jnp.dot(hi, bd, preferred_element_type=F32) + jnp.dot(lo, bd, preferred_element_type=F32)
    return z * lax.rsqrt(ss * (1.0 / HEAD_DIM) + EPS)


def _inproj_kernel(x_ref, sh_ref, sc_ref, ng_ref, w_ref, qg_ref, kg_ref, rc_ref, rlo_ref, rhi_ref, bd_ref,
                   q_ref, kc_ref, vc_ref, ks_ref, vs_ref, kw_ref, vw_ref, kcb_ref, vcb_ref, kvb_ref, gt_ref, hz_ref,
                   sm_ref):
    x = x_ref[...]
    xn = x * lax.rsqrt(jnp.mean(x * x, axis=-1, keepdims=True) + EPS) * ng_ref[...]
    h = (xn * (1.0 + sc_ref[...]) + sh_ref[...]).astype(BF16)
    rc, rlo, rhi = rc_ref[...], rlo_ref[...], rhi_ref[...]
    bd = bd_ref[...]

    zq = jnp.dot(h, w_ref[:, SEG_Q:SEG_KV], preferred_element_type=F32)
    q = _rope(_head_rms(zq, bd) * qg_ref[...], rc, rlo, rhi)
    q_ref[...] = (q * (HEAD_DIM ** -0.5 * LOG2E)).astype(BF16)

    zkv = jnp.dot(h, w_ref[:, SEG_KV:SEG_G], preferred_element_type=F32)
    bd1 = bd[:LANES, :LANES]
    k_refs = (kc_ref, ks_ref, kw_ref)
    v_refs = (vc_ref, vs_ref, vw_ref)
    rows = []
    for j in range(3):
        zk = zkv[:, 2 * j * KV_WIDTH:(2 * j + 1) * KV_WIDTH]
        k = _rope(_head_rms(zk, bd1) * kg_ref[j:j + 1, :], rc, rlo, rhi)
        v = zkv[:, (2 * j + 1) * KV_WIDTH:(2 * j + 2) * KV_WIDTH]
        for g in range(NSA_GROUPS):
            k_refs[j][:, g, :] = k[:, g * HEAD_DIM:(g + 1) * HEAD_DIM]
            v_refs[j][:, g, :] = v[:, g * HEAD_DIM:(g + 1) * HEAD_DIM]
        rows.append((k, v))
    kcb_ref[...] = rows[0][0].astype(BF16)
    vcb_ref[...] = rows[0][1].astype(BF16)
    parts = []
    for g in range(NSA_GROUPS):
        for k, v in rows[1:]:
            parts.append(k[:, g * HEAD_DIM:(g + 1) * HEAD_DIM])
            parts.append(v[:, g * HEAD_DIM:(g + 1) * HEAD_DIM])
    kvb_ref[...] = jnp.concatenate(parts, axis=1).astype(BF16)

    gt_ref[...] = _sigmoid(jnp.dot(h, w_ref[:, SEG_G:SEG_H], preferred_element_type=F32))
    hz_ref[...] = jnp.dot(h, w_ref[:, SEG_H:SEG_M], preferred_element_type=F32)
    sm_ref[...] = _sigmoid(jnp.dot(h, w_ref[:, SEG_M:], preferred_element_type=F32))


def _inproj(x2d, shift, scale, norm_g, w_pack, qg, kg, rope_tabs, bd, *, tm, tiles_per_mod, rope_tiles):
    r, d = x2d.shape
    n_tiles = r // tm
    rows_mod = shift.shape[1]
    mod_spec = pl.BlockSpec((None, rows_mod, d), lambda i: (i // tiles_per_mod, 0, 0))
    rope_spec = pl.BlockSpec((tm, LANES), lambda i: (i % rope_tiles, 0))

    def row_spec(width):
        return pl.BlockSpec((tm, width), lambda i: (i, 0))

    widths = (KV_WIDTH, KV_WIDTH, 4 * KV_WIDTH, NSA_GROUPS * LANES, 4 * HG_WIDTH, w_pack.shape[1] - SEG_M)
    dtypes = (BF16, BF16, BF16, F32, F32, F32)
    cache_spec = pl.BlockSpec((tm, NSA_GROUPS, HEAD_DIM), lambda i: (i, 0, 0))
    cache_shape = jax.ShapeDtypeStruct((r, NSA_GROUPS, HEAD_DIM), F32)
    return pl.pallas_call(
        _inproj_kernel,
        grid=(n_tiles,),
        in_specs=[row_spec(d), mod_spec, mod_spec, _const_spec((1, d)), _resident_spec(w_pack.shape),
                  _const_spec(qg.shape), _const_spec(kg.shape), rope_spec, rope_spec, rope_spec, _const_spec(bd.shape)],
        out_specs=[row_spec(NSA_WIDTH)] + [cache_spec] * 6 + [row_spec(w) for w in widths],
        out_shape=[jax.ShapeDtypeStruct((r, NSA_WIDTH), BF16)] + [cache_shape] * 6
        + [jax.ShapeDtypeStruct((r, w), dt) for w, dt in zip(widths, dtypes)],
        compiler_params=_cparams(("arbitrary",)),
        name="inproj",
    )(x2d, shift, scale, norm_g, w_pack, qg, kg, *rope_tabs, bd)


def _mm_kernel(a_ref, w_ref, o_ref):
    o_ref[...] = _dot(a_ref[...], w_ref[...])


def _mm(a, w, tm, name):
    m, k = a.shape
    n = w.shape[1]
    tm = min(tm, m)
    return pl.pallas_call(
        _mm_kernel,
        grid=(m // tm,),
        in_specs=[pl.BlockSpec((tm, k), lambda i: (i, 0)), _const_spec((k, n))],
        out_specs=pl.BlockSpec((tm, n), lambda i: (i, 0)),
        out_shape=jax.ShapeDtypeStruct((m, n), F32),
        compiler_params=_cparams(("arbitrary",)),
        name=name,
    )(a, w)


PAGES_PER_STEP = 16


def _cmp_proj_paged_kernel(pt_ref, *refs):
    k_pages, v_pages = refs[:PAGES_PER_STEP], refs[PAGES_PER_STEP:2 * PAGES_PER_STEP]
    perm_ref, w_ref, ok_ref, ov_ref, xk_ref, xv_ref = refs[2 * PAGES_PER_STEP:]
    sub = PAGE_SIZE // CMP_STRIDE
    perm = perm_ref[...]
    half = PAGES_PER_STEP // 2
    for kv, (pages, x_ref, o_ref) in enumerate(((k_pages, xk_ref, ok_ref), (v_pages, xv_ref, ov_ref))):
        for j0 in (0, half):
            stacked = jnp.concatenate([pr[...].astype(BF16) for pr in pages[j0:j0 + half]], axis=0)
            rows = _dot_nt(perm, stacked)
            for s in range(CMP_STRIDE):
                for j in range(half):
                    x_ref[(j0 + j) * sub:(j0 + j + 1) * sub, s * KV_WIDTH:(s + 1) * KV_WIDTH] = (
                        rows[s * sub:(s + 1) * sub, j * KV_WIDTH:(j + 1) * KV_WIDTH])
        o_ref[...] = _dot(x_ref[...], w_ref[kv])


def _cmp_proj_paged(pool_k, pool_v, page_table, w):
    db, n_pages = page_table.shape
    sub = PAGE_SIZE // CMP_STRIDE
    n = w.shape[2]
    steps = n_pages // PAGES_PER_STEP
    pos = np.arange(PAGE_SIZE)
    perm = jnp.asarray((pos[:, None] % sub) * CMP_STRIDE + pos[:, None] // sub == pos[None, :], BF16)
    page_specs = [pl.BlockSpec((None, KV_WIDTH, PAGE_SIZE), lambda b, c, pt, j=j: (pt[b, c * PAGES_PER_STEP + j], 0, 0))
                  for j in range(PAGES_PER_STEP)]
    out_spec = pl.BlockSpec((None, PAGES_PER_STEP * sub, n), lambda b, c, pt: (b, c, 0))
    rows_scratch = pltpu.VMEM((PAGES_PER_STEP * sub, CMP_STRIDE * KV_WIDTH), F32)
    grid_spec = pltpu.PrefetchScalarGridSpec(
        num_scalar_prefetch=1,
        grid=(db, steps),
        in_specs=page_specs + page_specs + [pl.BlockSpec(perm.shape, lambda b, c, pt: (0, 0)),
                                            pl.BlockSpec(w.shape, lambda b, c, pt: (0, 0, 0))],
        out_specs=[out_spec, out_spec],
        scratch_shapes=[rows_scratch, rows_scratch],
    )
    out = jax.ShapeDtypeStruct((db, n_pages * sub, n), F32)
    return pl.pallas_call(
        _cmp_proj_paged_kernel,
        grid_spec=grid_spec,
        out_shape=[out, out],
        compiler_params=_cparams(("arbitrary", "arbitrary")),
        name="cmp_proj_paged",
    )(page_table, *([pool_k] * PAGES_PER_STEP), *([pool_v] * PAGES_PER_STEP), perm, w)


def _cmp_combine_kernel(pk_ref, pv_ref, pe_ref, w1_ref, w2_ref, o_ref):
    n = pk_ref.shape[0]
    outs = []
    for j, p_ref in enumerate((pk_ref, pv_ref)):
        pe = jnp.dot(pe_ref[j], w1_ref[j], preferred_element_type=F32)
        pe_hid = pe[0:1, :KV_WIDTH] + pe[1:2, KV_WIDTH:]
        p = p_ref[...]
        nxt = pltpu.roll(p[:, KV_WIDTH:], n - 1, 0)
        hid = pe_hid + p[:, :KV_WIDTH] + nxt
        outs.append(_dot(jax.nn.gelu(hid), w2_ref[j]))
    for g in range(NSA_GROUPS):
        o_ref[g] = jnp.concatenate([o[:, g * HEAD_DIM:(g + 1) * HEAD_DIM] for o in outs], axis=1).astype(BF16)


def _cmp_combine(pk, pv, pe_x, w1, w2bd):
    bx, n, c = pk.shape
    return pl.pallas_call(
        _cmp_combine_kernel,
        grid=(bx,),
        in_specs=[pl.BlockSpec((None, n, c), lambda b: (b, 0, 0)), pl.BlockSpec((None, n, c), lambda b: (b, 0, 0)),
                  _const_spec(pe_x.shape), _const_spec(w1.shape), _const_spec(w2bd.shape)],
        out_specs=pl.BlockSpec((None, NSA_GROUPS, n, 2 * HEAD_DIM), lambda b: (b, 0, 0, 0)),
        out_shape=jax.ShapeDtypeStruct((bx, NSA_GROUPS, n, 2 * HEAD_DIM), BF16),
        compiler_params=_cparams(("arbitrary",)),
        name="cmp_combine",
    )(pk, pv, pe_x, w1, w2bd)


def _masked_softmax(s, mask):
    s = jnp.where(mask, s, NEG)
    m = jnp.max(s, axis=-1, keepdims=True)
    e = jnp.where(mask, jnp.exp2(s - m), 0.0)
    return e / jnp.maximum(jnp.sum(e, axis=-1, keepdims=True), 1e-30)


def _top_k_mask(score, k, axis, idx_out=False):
    n = score.shape[axis]
    pos = lax.broadcasted_iota(jnp.int32, score.shape, axis).astype(F32)
    sel = jnp.zeros(score.shape, jnp.bool_)
    idx = jnp.zeros((score.shape[0], LANES), jnp.int32)
    lane_out = lax.broadcasted_iota(jnp.int32, (score.shape[0], LANES), 1)
    work = score
    for it in range(k):
        mx = jnp.max(work, axis=axis, keepdims=True)
        first = jnp.min(jnp.where(work == mx, pos, float(n)), axis=axis, keepdims=True)
        pick = pos == first
        sel = jnp.logical_or(sel, pick)
        work = jnp.where(pick, -jnp.inf, work)
        if idx_out:
            idx = jnp.where(lane_out == it, first.astype(jnp.int32), idx)
    return (sel, idx) if idx_out else sel


def _block_scores(imp, qpos, axis):
    blk = lax.broadcasted_iota(jnp.int32, imp.shape, axis)
    cur = qpos >> 6
    valid = (blk << 6) <= qpos
    forced = (blk == 0) | (blk == cur) | (blk == cur - 1)
    return jnp.where(forced, 1e9, jnp.where(valid, imp, -1e9))


def _nsa_prompt_kernel(q_ref, kcv_ref, kvb_ref, gt_ref, ov_ref, o_ref, s0_scr, s1_scr, p0_scr, p1_scr, a0_scr, a1_scr,
                       m_scr, acc_scr, *, tq, tk):
    qi = pl.program_id(2)
    q0 = qi * tq
    nc = kcv_ref.shape[0]
    rows = HPG * tq

    q = q_ref[...]
    qs = jnp.concatenate([q[:, h * HEAD_DIM:(h + 1) * HEAD_DIM] for h in range(HPG)], axis=0)
    qpos = q0 + lax.broadcasted_iota(jnp.int32, (tq, 1), 0)
    qpos4 = jnp.concatenate([qpos] * HPG, axis=0)

    kcv = kcv_ref[...]
    kc, vc = kcv[:, :HEAD_DIM], kcv[:, HEAD_DIM:]
    cmp_end = lax.broadcasted_iota(jnp.int32, (1, nc), 1) * CMP_STRIDE + (CMP_BLK - 1)
    p = _masked_softmax(_dot_nt(qs, kc), cmp_end <= qpos4)
    o_cmp = _dot(p, vc)
    psum = p[0:tq]
    for h in range(1, HPG):
        psum = psum + p[h * tq:(h + 1) * tq]
    p_hi, p_lo = _split2(psum)
    ov_t = ov_ref[...]
    imp_t = _dot_nt(ov_t, p_hi) + _dot_nt(ov_t, p_lo)

    qpos_row = q0 + lax.broadcasted_iota(jnp.int32, (1, tq), 1)
    sel_t = _top_k_mask(_block_scores(imp_t, qpos_row, 0), N_SEL, 0)
    bias = jnp.where(sel_t, 0.0, SEL_BIAS).T.astype(BF16)
    q_aug = jnp.concatenate([jnp.concatenate([bias] * HPG, axis=0), qs], axis=1)

    lane_blk = lax.broadcasted_iota(jnp.int32, (1, LANES), 1)
    kcol = lax.broadcasted_iota(jnp.int32, (tk, 1), 0)
    krow = lax.broadcasted_iota(jnp.int32, (1, tk), 1)
    ones_v = jnp.ones((tk, LANES - HEAD_DIM), BF16)
    rb = SOFTMAX_VREGS * SUBLANES * LANES // tk
    qpos_rb = [q0 + (r % (tq // rb)) * rb + lax.broadcasted_iota(jnp.int32, (rb, 1), 0) for r in range(rows // rb)]

    half = rows // 2
    s_bufs, p_bufs, a_bufs = (s0_scr, s1_scr), (p0_scr, p1_scr), (a0_scr, a1_scr)

    def key_tile(j):
        return kvb_ref[pl.ds(pl.multiple_of(j * tk, tk), tk), :]

    def scores(slot, j, slc):
        blk = key_tile(j)
        if slc:
            onehot = jnp.where(((j * tk + kcol) >> 6) == lane_blk, 1.0, 0.0).astype(BF16)
            q_lhs, k_rhs = q_aug, jnp.concatenate([onehot, blk[:, 0:HEAD_DIM]], axis=1)
        else:
            q_lhs, k_rhs = qs, blk[:, 2 * HEAD_DIM:3 * HEAD_DIM]
        for h0 in (0, half):
            s_bufs[slot][h0:h0 + half, :] = _dot_nt(q_lhs[h0:h0 + half], k_rhs)

    def softmax(slot, j, mask_kind, live=None):
        k0 = j * tk
        for r in range(rows // rb):
            sl = slice(r * rb, (r + 1) * rb)
            s = s_bufs[slot][sl, :]
            keep = None
            if mask_kind == "causal":
                keep = (k0 + krow) <= qpos_rb[r]
            elif mask_kind == "window":
                keep = qpos_rb[r] - (k0 + krow) < WINDOW
            if live is not None:
                keep = live if keep is None else jnp.logical_and(keep, live)
            if keep is not None:
                s = jnp.where(keep, s, NEG)
            m_old = m_scr[sl, :]
            m_new = jnp.maximum(m_old, jnp.max(s, axis=-1, keepdims=True))
            a_bufs[slot][sl, :] = jnp.exp2(m_old - m_new)
            m_scr[sl, :] = m_new
            p_bufs[slot][sl, :] = jnp.exp2(s - jnp.concatenate([m_new] * (tk // LANES), axis=1)).astype(BF16)

    def accumulate(slot, j, v_lane0):
        v_aug = jnp.concatenate([key_tile(j)[:, v_lane0:v_lane0 + HEAD_DIM], ones_v], axis=1)
        for h0 in (0, half):
            sl = slice(h0, h0 + half)
            acc_scr[sl, :] = a_bufs[slot][sl, :] * acc_scr[sl, :] + jnp.dot(p_bufs[slot][sl, :], v_aug,
                                                                           preferred_element_type=F32)

    def reset():
        m_scr[...] = jnp.full(m_scr.shape, NEG, F32)
        acc_scr[...] = jnp.zeros(acc_scr.shape, F32)

    def result():
        acc = acc_scr[...]
        return acc[:, :HEAD_DIM] / jnp.maximum(acc[:, HEAD_DIM:HEAD_DIM + 1], 1e-30)

    jd = q0 // tk
    reset()
    scores(0, jd, True)
    scores(1, 0, True)
    softmax(0, jd, "causal")

    def past_pair(u, carry):
        ta, tb = 2 * u, 2 * u + 1
        scores(0, jnp.minimum(tb, jd), True)
        softmax(1, ta, None)
        accumulate(0, jnp.where(u == 0, jd, ta - 1), HEAD_DIM)
        scores(1, jnp.minimum(tb + 1, jd), True)
        softmax(0, tb, None, live=tb < jd)
        accumulate(1, ta, HEAD_DIM)
        return carry

    n_pairs = (jd + 1) // 2
    lax.fori_loop(0, n_pairs, past_pair, 0)
    accumulate(0, jnp.where(n_pairs == 0, jd, jnp.minimum(2 * n_pairs - 1, jd)), HEAD_DIM)
    o_slc = result()

    reset()
    n_back = -(-(WINDOW - 1) // tk)
    tiles = [(jd, "causal", None)] + [
        (jnp.maximum(jd - back, 0), "window" if (back + 1) * tk - 1 >= WINDOW else None, jd >= back)
        for back in range(1, n_back + 1)]
    scores(0, tiles[0][0], False)
    for i, (j, mask_kind, live) in enumerate(tiles):
        if i + 1 < len(tiles):
            scores((i + 1) % 2, tiles[i + 1][0], False)
        softmax(i % 2, j, mask_kind, live)
        accumulate(i % 2, j, 3 * HEAD_DIM)
    o_win = result()

    gt = gt_ref[...]
    outs = []
    for h in range(HPG):
        r0 = h * tq
        outs.append(gt[:, 3 * h:3 * h + 1] * o_cmp[r0:r0 + tq] + gt[:, 3 * h + 1:3 * h + 2] * o_slc[r0:r0 + tq]
                    + gt[:, 3 * h + 2:3 * h + 3] * o_win[r0:r0 + tq])
    o_ref[...] = jnp.concatenate(outs, axis=1).astype(BF16)


def _nsa_prompt(q, kcv, kvb, gt, ov, *, tq, tk):
    b, t, _ = q.shape
    nc = kcv.shape[2]
    assert tk % tq == 0 and t % tk == 0 and tk <= WINDOW
    return pl.pallas_call(
        functools.partial(_nsa_prompt_kernel, tq=tq, tk=tk),
        grid=(b, NSA_GROUPS, t // tq),
        in_specs=[pl.BlockSpec((None, tq, GRP_Q), lambda bi, g, i: (bi, i, g)),
                  pl.BlockSpec((None, None, nc, 2 * HEAD_DIM), lambda bi, g, i: (bi, g, 0, 0)),
                  pl.BlockSpec((None, t, 4 * HEAD_DIM), lambda bi, g, i: (bi, 0, g)),
                  pl.BlockSpec((None, tq, LANES), lambda bi, g, i: (bi, i, g)),
                  _const_spec(ov.shape)],
        out_specs=pl.BlockSpec((None, tq, GRP_Q), lambda bi, g, i: (bi, i, g)),
        out_shape=jax.ShapeDtypeStruct((b, t, NSA_WIDTH), BF16),
        scratch_shapes=[pltpu.VMEM((HPG * tq, tk), F32)] * 2 + [pltpu.VMEM((HPG * tq, tk), BF16)] * 2
        + [pltpu.VMEM((HPG * tq, LANES), F32)] * 4,
        compiler_params=_cparams(("arbitrary", "arbitrary", "arbitrary")),
        name="nsa_prompt",
    )(q, kcv, kvb, gt, ov)


def _nsa_dec_select_kernel(q_ref, kcv_ref, ov_ref, ocmp_ref, idx_ref, *, qpos):
    nc = kcv_ref.shape[1]
    q = q_ref[...]
    cmp_end = lax.broadcasted_iota(jnp.int32, (1, nc), 1) * CMP_STRIDE + (CMP_BLK - 1)
    mask = cmp_end <= qpos
    ov = ov_ref[...]
    imps = []
    for g in range(NSA_GROUPS):
        kcv = kcv_ref[g]
        p = _masked_softmax(_dot_nt(q[g * HPG:(g + 1) * HPG], kcv[:, :HEAD_DIM]), mask)
        ocmp_ref[g * HPG:(g + 1) * HPG, :] = _dot(p, kcv[:, HEAD_DIM:])
        p_hi, p_lo = _split2(jnp.sum(p, axis=0, keepdims=True))
        imps.append(jnp.dot(p_hi, ov, preferred_element_type=F32) + jnp.dot(p_lo, ov, preferred_element_type=F32))
    imp = jnp.concatenate(imps + [jnp.zeros((8 - NSA_GROUPS, ov.shape[1]), F32)], axis=0)
    score = _block_scores(imp, jnp.full((8, 1), qpos, jnp.int32), 1)
    _, idx = _top_k_mask(score, N_SEL, 1, idx_out=True)
    idx_ref[...] = idx


def _nsa_dec_select(q3, kcv, ov, qpos):
    db, nh, _ = q3.shape
    nc = kcv.shape[2]
    return pl.pallas_call(
        functools.partial(_nsa_dec_select_kernel, qpos=qpos),
        grid=(db,),
        in_specs=[pl.BlockSpec((None, nh, HEAD_DIM), lambda b: (b, 0, 0)),
                  pl.BlockSpec((None, NSA_GROUPS, nc, 2 * HEAD_DIM), lambda b: (b, 0, 0, 0)),
                  _const_spec(ov.shape)],
        out_specs=[pl.BlockSpec((None, nh, HEAD_DIM), lambda b: (b, 0, 0)),
                   pl.BlockSpec((None, 8, LANES), lambda b: (b, 0, 0))],
        out_shape=[jax.ShapeDtypeStruct((db, nh, HEAD_DIM), F32), jax.ShapeDtypeStruct((db, 8, LANES), jnp.int32)],
        compiler_params=_cparams(("arbitrary",)),
        name="nsa_dec_select",
    )(q3, kcv, ov)


def _nsa_dec_attend_kernel(idx_ref, pt_ref, *refs, past):
    k_refs = refs[:N_SEL]
    v_refs = refs[N_SEL:2 * N_SEL]
    (q_ref, nk_ref, nv_ref, nwk_ref, nwv_ref, wk_ref, wv_ref, ocmp_ref, gt_ref, o_ref, wko_ref, wvo_ref) = refs[2 * N_SEL:]
    b = pl.program_id(0)
    g = pl.program_id(1)
    win = wk_ref.shape[1]
    q = q_ref[...]
    lane_blk = lax.broadcasted_iota(jnp.int32, (1, SLC_BLK), 1)

    s_parts, v_parts, pos_parts = [], [], []
    for i in range(N_SEL):
        blk = idx_ref[b, g * N_SEL + i]
        first_half = blk % (PAGE_SIZE // SLC_BLK) == 0
        kpos = blk * SLC_BLK + lane_blk
        from_new = kpos >= past
        k_t, v_t = k_refs[i][...], v_refs[i][...]
        k_t = jnp.where(from_new, nk_ref[...], jnp.where(first_half, k_t[:, :SLC_BLK], k_t[:, SLC_BLK:]))
        v_t = jnp.where(from_new, nv_ref[...], jnp.where(first_half, v_t[:, :SLC_BLK], v_t[:, SLC_BLK:]))
        s_parts.append(_dot(q, k_t))
        v_parts.append(v_t)
        pos_parts.append(kpos)
    p = _masked_softmax(jnp.concatenate(s_parts, axis=1), jnp.concatenate(pos_parts, axis=1) <= past)
    o_slc = _dot_nt(p, jnp.concatenate(v_parts, axis=1))

    lane = lax.broadcasted_iota(jnp.int32, (1, win), 1)
    wk_new = jnp.where(lane == win - 1, nwk_ref[...], pltpu.roll(wk_ref[...], win - 1, 1))
    wv_new = jnp.where(lane == win - 1, nwv_ref[...], pltpu.roll(wv_ref[...], win - 1, 1))
    wko_ref[...] = wk_new
    wvo_ref[...] = wv_new
    p = _masked_softmax(_dot(q, wk_new), (win - 1 - lane) < WINDOW)
    o_win = _dot_nt(p, wv_new)

    gt = gt_ref[...]
    o_ref[...] = gt[:, 0:1] * ocmp_ref[...] + gt[:, 1:2] * o_slc + gt[:, 2:3] * o_win


def _nsa_dec_attend(idx, page_table, pool_k, pool_v, q4, new_cols, win_k, win_v, o_cmp4, gt4, past):
    db, n_pages = page_table.shape
    win = win_k.shape[3]
    blocks_per_page = PAGE_SIZE // SLC_BLK
    assert blocks_per_page == 2

    def blk_map(i):
        def index(b, g, idx_s, pt_s):
            page = jnp.minimum(idx_s[b, g * N_SEL + i] // blocks_per_page, n_pages - 1)
            return pt_s[b, page], g, 0, 0
        return index

    sel_specs = [pl.BlockSpec((None, None, HEAD_DIM, PAGE_SIZE), blk_map(i)) for i in range(N_SEL)]

    def per_bg(shape):
        return pl.BlockSpec((None, None) + shape, lambda b, g, *_: (b, g, 0, 0))

    grid_spec = pltpu.PrefetchScalarGridSpec(
        num_scalar_prefetch=2,
        grid=(db, NSA_GROUPS),
        in_specs=sel_specs + sel_specs + [per_bg((HPG, HEAD_DIM))] + [per_bg((HEAD_DIM, 1))] * 4
        + [per_bg((HEAD_DIM, win))] * 2 + [per_bg((HPG, HEAD_DIM)), per_bg((HPG, 3))],
        out_specs=[per_bg((HPG, HEAD_DIM)), per_bg((HEAD_DIM, win)), per_bg((HEAD_DIM, win))],
    )
    return pl.pallas_call(
        functools.partial(_nsa_dec_attend_kernel, past=past),
        grid_spec=grid_spec,
        out_shape=[jax.ShapeDtypeStruct((db, NSA_GROUPS, HPG, HEAD_DIM), F32),
                   jax.ShapeDtypeStruct(win_k.shape, F32), jax.ShapeDtypeStruct(win_v.shape, F32)],
        compiler_params=_cparams(("arbitrary", "arbitrary")),
        name="nsa_dec_attend",
    )(idx, page_table, *([pool_k] * N_SEL), *([pool_v] * N_SEL), q4, *new_cols, win_k, win_v, o_cmp4, gt4)


def _forget_lower_bound(lb_raw):
    m = jnp.max(lb_raw, axis=0, keepdims=True)
    e = jnp.exp(lb_raw - m)
    return e[0:1] / jnp.sum(e, axis=0, keepdims=True)


def _hgrn_out(o, hg, norm_g):
    o = o * lax.rsqrt(jnp.mean(o * o, axis=-1, keepdims=True) + EPS) * norm_g
    return o * (hg * _sigmoid(hg))


def _hgrn_chunk_common(q, hf, v, lb, tri, st, b_s, k_s, v_s):
    heads = range(len(q))
    lc = q[0].shape[0]
    f = [lb[h] + (1.0 - lb[h]) * _sigmoid(hf[h]) for h in heads]
    k = [1.0 - f[h] for h in heads]
    parts = [p for h in heads for p in _split3(jnp.log(f[h]))]
    csum = jnp.dot(tri, jnp.concatenate(parts, axis=1), preferred_element_type=F32)
    bcum = [csum[:, 3 * h * HG_DIM:(3 * h + 1) * HG_DIM] + csum[:, (3 * h + 1) * HG_DIM:(3 * h + 2) * HG_DIM]
            + csum[:, (3 * h + 2) * HG_DIM:(3 * h + 3) * HG_DIM] for h in heads]
    for h in heads:
        b_s[h][...] = bcum[h]
        k_s[h][...] = k[h]
        v_s[h][...] = v[h]

    o_inter = [_dot_nt(q[h] * jnp.exp(bcum[h]), st[h]) for h in heads]
    span = None
    for i in range(lc // HG_SUB):
        r0 = i * HG_SUB
        for h in heads:
            d_i = bcum[h][r0:r0 + 1] - bcum[h][r0 + HG_SUB - 1:r0 + HG_SUB]
            span = d_i if span is None else jnp.maximum(span, d_i)
    st_new = [st[h] * jnp.exp(bcum[h][lc - 1:lc]) + _dot(v[h].T, k[h] * jnp.exp(bcum[h][lc - 1:lc] - bcum[h]))
              for h in heads]
    return o_inter, k, bcum, st_new, span


def _hgrn_intra_matmul(o_inter, q, k, bcum, v, o_s):
    heads = range(len(q))
    lc = q[0].shape[0]
    n_sub = lc // HG_SUB
    row_sub = lax.broadcasted_iota(jnp.int32, (lc, HG_DIM), 0) >> HG_SUB_LOG2
    causal = lax.broadcasted_iota(jnp.int32, (lc, lc), 1) <= lax.broadcasted_iota(jnp.int32, (lc, lc), 0)
    for h in heads:
        ref = [bcum[h][j * HG_SUB:j * HG_SUB + 1] for j in range(n_sub)]
        ref_rows = jnp.concatenate([jnp.broadcast_to(r, (HG_SUB, HG_DIM)) for r in ref], axis=0)
        q_t = q[h] * jnp.exp(bcum[h] - ref_rows)
        k_t = k[h] * jnp.exp(ref_rows - bcum[h])
        q_cols = [jnp.where(row_sub >= j, q_t * jnp.exp(jnp.minimum(ref_rows - ref[j], 0.0)), 0.0).astype(BF16)
                  for j in range(n_sub)]
        k_cols = [jnp.where(row_sub == j, k_t, 0.0).astype(BF16) for j in range(n_sub)]
        a = _dot_nt(jnp.concatenate(q_cols, axis=1), jnp.concatenate(k_cols, axis=1))
        o_s[h][...] = o_inter[h] + _dot(jnp.where(causal, a, 0.0), v[h])


def _hgrn_intra_direct(o_inter, q, k, bcum, v, b_s, k_s, v_s, o_s):
    lc = q.shape[0]
    t_idx = lax.broadcasted_iota(jnp.int32, (SUBLANES, 1), 0)
    for i in range(lc // HG_SUB):
        r0 = i * HG_SUB
        o_i = o_inter[r0:r0 + HG_SUB]
        if i > 0:
            ref_b = bcum[r0:r0 + 1]
            a = _dot_nt(q[r0:r0 + HG_SUB] * jnp.exp(bcum[r0:r0 + HG_SUB] - ref_b), k[:r0] * jnp.exp(ref_b - bcum[:r0]))
            o_i = o_i + _dot(a, v[:r0])
        for t0 in range(r0, r0 + HG_SUB, SUBLANES):
            b_t, q_t = bcum[t0:t0 + SUBLANES], q[t0:t0 + SUBLANES]
            acc = o_i[t0 - r0:t0 - r0 + SUBLANES]
            for s in range(r0, t0 + SUBLANES):
                d = b_t - b_s[s:s + 1, :]
                if s >= t0:
                    d = jnp.where(t_idx >= s - t0, d, NEG)
                w = jnp.exp(d) * q_t * k_s[s:s + 1, :]
                acc = acc + jnp.sum(w, axis=-1, keepdims=True) * v_s[s:s + 1, :]
            o_s[t0:t0 + SUBLANES, :] = acc


def _hgrn_prompt_kernel(hz_ref, lb_ref, ng_ref, tri_ref, o_ref, st_ref, state_t, b_s, k_s, v_s, o_s):
    c = pl.program_id(1)
    n_c = pl.num_programs(1)

    @pl.when(c == 0)
    def _():
        state_t[...] = jnp.zeros_like(state_t)

    lb_all = _forget_lower_bound(lb_ref[...])
    tri = tri_ref[...]

    def seg(j, h):
        return hz_ref[:, j * HG_WIDTH + h * HG_DIM:j * HG_WIDTH + (h + 1) * HG_DIM]

    heads = range(HG_HEADS)
    q, v = [seg(0, h) for h in heads], [seg(2, h) for h in heads]
    scratch = [[s.at[h] for h in heads] for s in (b_s, k_s, v_s, o_s)]
    o_inter, k, bcum, st_new, span = _hgrn_chunk_common(
        q, [seg(1, h) for h in heads], v, [lb_all[:, h * HG_DIM:(h + 1) * HG_DIM] for h in heads], tri,
        [state_t[h] for h in heads], *scratch[:3])
    for h in heads:
        state_t[h] = st_new[h]
    direct = jnp.max(span) > HG_SAFE_DECAY

    @pl.when(jnp.logical_not(direct))
    def _():
        _hgrn_intra_matmul(o_inter, q, k, bcum, v, scratch[3])

    @pl.when(direct)
    def _():
        for h in heads:
            _hgrn_intra_direct(o_inter[h], q[h], k[h], bcum[h], v[h], *(s[h] for s in scratch))

    for h in range(HG_HEADS):
        o_ref[:, h * HG_DIM:(h + 1) * HG_DIM] = _hgrn_out(o_s[h], seg(3, h), ng_ref[...]).astype(BF16)

    @pl.when(c == n_c - 1)
    def _():
        for h in range(HG_HEADS):
            st_ref[h] = state_t[h].T


def _hgrn_prompt(hz3, hg_lb, norm_g, *, lc):
    b, t, w = hz3.shape
    tri = jnp.tril(jnp.ones((lc, lc), F32)).astype(BF16)
    return pl.pallas_call(
        _hgrn_prompt_kernel,
        grid=(b, t // lc),
        in_specs=[pl.BlockSpec((None, lc, w), lambda bi, c: (bi, c, 0)), _const_spec(hg_lb.shape),
                  _const_spec((1, HG_DIM)), _const_spec((lc, lc))],
        out_specs=[pl.BlockSpec((None, lc, HG_WIDTH), lambda bi, c: (bi, c, 0)),
                   pl.BlockSpec((None, HG_HEADS, HG_DIM, HG_DIM), lambda bi, c: (bi, 0, 0, 0))],
        out_shape=[jax.ShapeDtypeStruct((b, t, HG_WIDTH), BF16),
                   jax.ShapeDtypeStruct((b, HG_HEADS, HG_DIM, HG_DIM), F32)],
        scratch_shapes=[pltpu.VMEM((HG_HEADS, HG_DIM, HG_DIM), F32)] + [pltpu.VMEM((HG_HEADS, lc, HG_DIM), F32)] * 4,
        compiler_params=_cparams(("arbitrary", "arbitrary")),
        name="hgrn_prompt",
    )(hz3, hg_lb, norm_g, tri)


def _hgrn_dec_kernel(hz_ref, s0_ref, lb_ref, ng_ref, o_ref, s_ref):
    hz = hz_ref[...]
    lb_all = _forget_lower_bound(lb_ref[...])
    eye = lax.broadcasted_iota(jnp.int32, (HG_DIM, HG_DIM), 0) == lax.broadcasted_iota(jnp.int32, (HG_DIM, HG_DIM), 1)

    def column(row):
        return jnp.sum(jnp.where(eye, row, 0.0), axis=1, keepdims=True)

    outs = []
    for h in range(HG_HEADS):
        def seg(j):
            return hz[:, j * HG_WIDTH + h * HG_DIM:j * HG_WIDTH + (h + 1) * HG_DIM]

        lb = lb_all[:, h * HG_DIM:(h + 1) * HG_DIM]
        f = lb + (1.0 - lb) * _sigmoid(seg(1))
        q, k, v = seg(0), 1.0 - f, seg(2)
        s0 = s0_ref[h]
        s_ref[h] = column(f) * s0 + column(k) * v
        o = jnp.sum(column(q * f) * s0, axis=0, keepdims=True) + jnp.sum(q * k, axis=-1, keepdims=True) * v
        outs.append(_hgrn_out(o, seg(3), ng_ref[...]))
    o_ref[...] = jnp.concatenate(outs, axis=1)


def _hgrn_dec(hz, s0, hg_lb, norm_g):
    db = hz.shape[0]
    return pl.pallas_call(
        _hgrn_dec_kernel,
        grid=(db,),
        in_specs=[pl.BlockSpec((None, 1, hz.shape[1]), lambda b: (b, 0, 0)),
                  pl.BlockSpec((None, HG_HEADS, HG_DIM, HG_DIM), lambda b: (b, 0, 0, 0)),
                  _const_spec(hg_lb.shape), _const_spec((1, HG_DIM))],
        out_specs=[pl.BlockSpec((None, 1, HG_WIDTH), lambda b: (b, 0, 0)),
                   pl.BlockSpec((None, HG_HEADS, HG_DIM, HG_DIM), lambda b: (b, 0, 0, 0))],
        out_shape=[jax.ShapeDtypeStruct((db, 1, HG_WIDTH), F32), jax.ShapeDtypeStruct(s0.shape, F32)],
        compiler_params=_cparams(("arbitrary",)),
        name="hgrn_dec",
    )(hz.reshape(db, 1, -1), s0, hg_lb, norm_g)


def _merge_kernel(on_ref, oh_ref, sm_ref, x_ref, g1_ref, sh_ref, sc_ref, ng_ref, wn_ref, wh_ref, wo_ref,
                  x1_ref, h2_ref):
    d = x_ref.shape[1]
    sm = sm_ref[...]
    mix = sm[:, :d] * _dot(on_ref[...], wn_ref[...]) + sm[:, d:] * _dot(oh_ref[...], wh_ref[...])
    x1 = x_ref[...] + g1_ref[...] * _dot(mix, wo_ref[...])
    x1_ref[...] = x1
    xn = x1 * lax.rsqrt(jnp.mean(x1 * x1, axis=-1, keepdims=True) + EPS) * ng_ref[...]
    h2_ref[...] = (xn * (1.0 + sc_ref[...]) + sh_ref[...]).astype(BF16)


def _merge(o_nsa, o_hg, sm, x2d, g1, sh2, sc2, norm_g, w_br_nsa, w_br_hg, w_out, *, tm, tiles_per_mod):
    r, d = x2d.shape
    rows_mod = g1.shape[1]
    mod_spec = pl.BlockSpec((None, rows_mod, d), lambda i: (i // tiles_per_mod, 0, 0))

    def row_spec(width):
        return pl.BlockSpec((tm, width), lambda i: (i, 0))

    return pl.pallas_call(
        _merge_kernel,
        grid=(r // tm,),
        in_specs=[row_spec(o_nsa.shape[1]), row_spec(o_hg.shape[1]), row_spec(sm.shape[1]), row_spec(d),
                  mod_spec, mod_spec, mod_spec, _const_spec((1, d)),
                  _const_spec(w_br_nsa.shape), _const_spec(w_br_hg.shape), _const_spec(w_out.shape)],
        out_specs=[row_spec(d), row_spec(d)],
        out_shape=[jax.ShapeDtypeStruct((r, d), F32), jax.ShapeDtypeStruct((r, d), BF16)],
        compiler_params=_cparams(("arbitrary",)),
        name="merge",
    )(o_nsa, o_hg, sm, x2d, g1, sh2, sc2, norm_g, w_br_nsa, w_br_hg, w_out)


FF_CHUNKS = 2


def _ffn_core(h2, x1, g2, wu_ref, cw_ref, cb_ref, wd_ref, prev_rows):
    d_ff = wd_ref.shape[0]
    ch = d_ff // FF_CHUNKS
    acc = None
    us = []
    for ci in range(FF_CHUNKS):
        c0, c1 = ci * ch, (ci + 1) * ch
        u = jnp.dot(h2, wu_ref[:, c0:c1], preferred_element_type=F32)
        v = jnp.dot(h2, wu_ref[:, d_ff + c0:d_ff + c1], preferred_element_type=F32)
        u_m1, u_m2 = prev_rows(u, c0, c1)
        y = cb_ref[:, c0:c1] + cw_ref[0:1, c0:c1] * u_m2 + cw_ref[1:2, c0:c1] * u_m1 + cw_ref[2:3, c0:c1] * u
        part = _dot(y * _sigmoid(y) * v, wd_ref[c0:c1, :])
        acc = part if acc is None else acc + part
        us.append(u)
    return x1 + g2 * acc, us


def _ffn_prompt_kernel(h2_ref, x1_ref, g2_ref, wu_ref, cw_ref, cb_ref, wd_ref, y_ref, tail_ref, carry):
    ti = pl.program_id(1)
    tm = h2_ref.shape[0]

    @pl.when(ti == 0)
    def _():
        carry[...] = jnp.zeros_like(carry)

    row = lax.broadcasted_iota(jnp.int32, (tm, 1), 0)

    def prev_rows(u, c0, c1):
        last = carry[:, c0:c1]
        m1 = jnp.where(row == 0, last[7:8], pltpu.roll(u, 1, 0))
        m2 = jnp.where(row == 0, last[6:7], jnp.where(row == 1, last[7:8], pltpu.roll(u, 2, 0)))
        return m1, m2

    out, us = _ffn_core(h2_ref[...], x1_ref[...], g2_ref[...], wu_ref, cw_ref, cb_ref, wd_ref, prev_rows)
    y_ref[...] = out
    ch = us[0].shape[1]
    for ci, u in enumerate(us):
        carry[:, ci * ch:(ci + 1) * ch] = u[tm - 8:, :]
    tail_ref[...] = carry[...]


def _ffn_prompt(h2, x1, g2, w_up, conv_w8, conv_b, w_down, *, b, tm):
    r, d = x1.shape
    d_ff = w_down.shape[0]
    tpb = r // b // tm

    def row_spec(width):
        return pl.BlockSpec((tm, width), lambda bi, i: (bi * tpb + i, 0))

    return pl.pallas_call(
        _ffn_prompt_kernel,
        grid=(b, tpb),
        in_specs=[row_spec(d), row_spec(d), pl.BlockSpec((None, 1, d), lambda bi, i: (bi, 0, 0)),
                  _resident_spec(w_up.shape), _const_spec(conv_w8.shape), _const_spec(conv_b.shape),
                  _resident_spec(w_down.shape)],
        out_specs=[row_spec(d), pl.BlockSpec((None, 8, d_ff), lambda bi, i: (bi, 0, 0))],
        out_shape=[jax.ShapeDtypeStruct((r, d), F32), jax.ShapeDtypeStruct((b, 8, d_ff), F32)],
        scratch_shapes=[pltpu.VMEM((8, d_ff), F32)],
        compiler_params=_cparams(("arbitrary", "arbitrary")),
        name="ffn_prompt",
    )(h2, x1, g2, w_up, conv_w8, conv_b, w_down)


def _ffn_dec_kernel(h2_ref, x1_ref, g2_ref, wu_ref, cw_ref, cb_ref, wd_ref, m1_ref, m2_ref, y_ref, u_ref):
    def prev_rows(u, c0, c1):
        return m1_ref[:, c0:c1], m2_ref[:, c0:c1]

    out, us = _ffn_core(h2_ref[...], x1_ref[...], g2_ref[...], wu_ref, cw_ref, cb_ref, wd_ref, prev_rows)
    y_ref[...] = out
    u_ref[...] = jnp.concatenate(us, axis=1)


def _ffn_dec(h2, x1, g2, w_up, conv_w8, conv_b, w_down, u_m1, u_m2):
    r, d = x1.shape
    d_ff = w_down.shape[0]
    args = (h2, x1, g2, w_up, conv_w8, conv_b, w_down, u_m1, u_m2)
    return pl.pallas_call(
        _ffn_dec_kernel,
        grid=(1,),
        in_specs=[_const_spec(a.shape) for a in args],
        out_specs=[_const_spec((r, d)), _const_spec((r, d_ff))],
        out_shape=[jax.ShapeDtypeStruct((r, d), F32), jax.ShapeDtypeStruct((r, d_ff), F32)],
        compiler_params=_cparams(("arbitrary",)),
        name="ffn_dec",
    )(*args)


def _pack_w_in(w_in):
    cuts = np.cumsum([NSA_WIDTH] + [KV_WIDTH] * 6 + [3 * NSA_HEADS] + [HG_WIDTH] * 4)
    q, kv, gates, rest = w_in[:, :cuts[0]], w_in[:, cuts[0]:cuts[6]], w_in[:, cuts[6]:cuts[7]], w_in[:, cuts[7]:]
    hz, m = rest[:, :4 * HG_WIDTH], rest[:, 4 * HG_WIDTH:]
    per_group = 3 * HPG
    pad = jnp.zeros((w_in.shape[0], LANES - per_group), w_in.dtype)
    gate_cols = []
    for g in range(NSA_GROUPS):
        gate_cols += [gates[:, g * per_group:(g + 1) * per_group], pad]
    return jnp.concatenate([q, kv] + gate_cols + [hz, m], axis=1).astype(BF16)


def _rope_tables(pos):
    half = ROT_DIM // 2
    inv = ROPE_THETA ** (-jnp.arange(0, ROT_DIM, 2, dtype=F32) / ROT_DIM)
    ang = pos.astype(F32)[:, None] * inv[None, :]
    cos, sin = jnp.cos(ang), jnp.sin(ang)
    n = pos.shape[0]
    ones = jnp.ones((n, HEAD_DIM - ROT_DIM), F32)
    zeros = jnp.zeros((n, HEAD_DIM - ROT_DIM), F32)
    z8 = jnp.zeros((n, half), F32)
    c = jnp.concatenate([cos, cos, ones], axis=1)
    s_lo = jnp.concatenate([-sin, z8, zeros], axis=1)
    s_hi = jnp.concatenate([z8, sin, zeros], axis=1)
    return tuple(jnp.concatenate([t, t], axis=1) for t in (c, s_lo, s_hi))


def _cmp_weights(cmp_w1, cmp_w2, cmp_pe):
    n_half = CMP_BLK // CMP_STRIDE
    eye = jnp.eye(NSA_GROUPS, dtype=F32)
    w1s, w2s, pes = [], [], []
    for j in range(2):
        w = cmp_w1[j].reshape(n_half, CMP_STRIDE, HEAD_DIM, HEAD_DIM)
        big = jnp.einsum('rsdh,ge->sgdreh', w, eye)
        w1s.append(big.reshape(CMP_STRIDE * KV_WIDTH, n_half * KV_WIDTH))
        w2s.append(jnp.einsum('dh,ge->gdeh', cmp_w2[j], eye).reshape(KV_WIDTH, KV_WIDTH))
        pe = cmp_pe[j].reshape(n_half, CMP_STRIDE, 1, HEAD_DIM)
        pe = jnp.broadcast_to(pe, (n_half, CMP_STRIDE, NSA_GROUPS, HEAD_DIM)).reshape(n_half, -1)
        pes.append(jnp.concatenate([pe, jnp.zeros((8 - n_half, pe.shape[1]), F32)], axis=0))
    return jnp.stack(w1s).astype(BF16), jnp.stack(w2s).astype(BF16), jnp.stack(pes).astype(BF16)


def _overlap(nc_pad, nc, ns, ns_pad):
    start = np.arange(nc_pad) * CMP_STRIDE
    end = start + CMP_BLK - 1
    s0 = np.arange(ns_pad) * SLC_BLK
    s1 = s0 + SLC_BLK - 1
    m = (start[:, None] <= s1[None, :]) & (end[:, None] >= s0[None, :])
    m &= (np.arange(nc_pad) < nc)[:, None] & (np.arange(ns_pad) < ns)[None, :]
    return jnp.asarray(m, BF16)


def kernel(x_prompt, x_sample, cache_cmp_k, cache_cmp_v, cache_slc_k, cache_slc_v, cache_win_k, cache_win_v,
           state_hgrn, state_ffn_conv, page_table, c_prompt, c_sample, ada_w, ada_b, norm_attn, norm_ffn, w_in,
           q_gain, k_gain, cmp_w1, cmp_w2, cmp_pe, hg_lb, hg_norm, w_br_nsa, w_br_hg, w_out, w_up, conv_w,
           conv_b, w_down):
    depth = ada_w.shape[0]
    assert depth == 1, "single-layer trunk only"
    b, t, d = x_prompt.shape
    db, tn, _ = x_sample.shape
    assert tn == 1
    n_pages = page_table.shape[1]
    past = n_pages * PAGE_SIZE
    d_ff = w_down.shape[1]
    tq = 256
    tk = 256
    tm = 256
    lc = 128
    assert t >= WINDOW and n_pages % PAGES_PER_STEP == 0
    assert cache_win_k.shape[2] == WINDOW

    w_pack = _pack_w_in(w_in[0])
    qg = jnp.tile(q_gain[0], NSA_HEADS).reshape(1, NSA_WIDTH)
    kg = jnp.concatenate([jnp.tile(k_gain[0], (1, NSA_GROUPS)), jnp.zeros((5, KV_WIDTH), F32)], axis=0)
    hd = np.arange(NSA_WIDTH) // HEAD_DIM
    bd = jnp.asarray(hd[:, None] == hd[None, :], BF16)
    w1c, w2c, pe_x = _cmp_weights(cmp_w1[0], cmp_w2[0], cmp_pe[0])
    conv_w8 = jnp.concatenate([conv_w[0], jnp.zeros((8 - CONV_W, d_ff), F32)], axis=0)
    conv_b2 = conv_b[0].reshape(1, d_ff)
    wbn, wbh, wo = w_br_nsa[0].astype(BF16), w_br_hg[0].astype(BF16), w_out[0].astype(BF16)
    wu, wd = w_up[0].astype(BF16), w_down[0].astype(BF16)
    norm_a, norm_f, hg_n = norm_attn[0].reshape(1, d), norm_ffn[0].reshape(1, d), hg_norm[0].reshape(1, HG_DIM)

    n_c = b + db
    c_all = jnp.concatenate([c_prompt, c_sample, jnp.zeros((-n_c % 8, d), F32)], axis=0)
    ada = _ada(c_all, ada_w[0], ada_b[0])
    mods_p = [ada[:b, i * d:(i + 1) * d].reshape(b, 1, d) for i in range(6)]
    mods_s = [ada[b:n_c, i * d:(i + 1) * d].reshape(1, db, d) for i in range(6)]

    tpb = t // tm
    (q, kc_r, vc_r, ks_r, vs_r, kw_r, vw_r, kcb, vcb, kvb, gt, hz, sm) = _inproj(
        x_prompt.reshape(b * t, d), mods_p[0], mods_p[1], norm_a, w_pack, qg, kg, _rope_tables(jnp.arange(t)), bd,
        tm=tm, tiles_per_mod=tpb, rope_tiles=tpb)

    nsub = t // CMP_STRIDE
    pk = _mm(kcb.reshape(b * nsub, CMP_STRIDE * KV_WIDTH), w1c[0], 512, "cmp_proj_k").reshape(b, nsub, -1)
    pv = _mm(vcb.reshape(b * nsub, CMP_STRIDE * KV_WIDTH), w1c[1], 512, "cmp_proj_v").reshape(b, nsub, -1)
    kcv = _cmp_combine(pk, pv, pe_x, w1c, w2c)
    ov = _overlap(nsub, nsub - 1, t // SLC_BLK, LANES).T
    o_nsa = _nsa_prompt(q.reshape(b, t, -1), kcv, kvb.reshape(b, t, -1), gt.reshape(b, t, -1), ov, tq=tq, tk=tk)

    o_hg, st_p = _hgrn_prompt(hz.reshape(b, t, -1), hg_lb, hg_n, lc=lc)

    x1, h2 = _merge(o_nsa.reshape(b * t, -1), o_hg.reshape(b * t, -1), sm, x_prompt.reshape(b * t, d),
                    mods_p[2], mods_p[3], mods_p[4], norm_f, wbn, wbh, wo, tm=tm, tiles_per_mod=tpb)
    y_p, tail = _ffn_prompt(h2, x1, mods_p[5], wu, conv_w8, conv_b2, wd, b=b, tm=tm)

    def rows5(a):
        return a.reshape(1, b, t, NSA_GROUPS, HEAD_DIM)

    w = min(WINDOW, t)
    out_p = (rows5(kc_r), rows5(vc_r), rows5(ks_r), rows5(vs_r), rows5(kw_r)[:, :, -w:], rows5(vw_r)[:, :, -w:],
             st_p[None], tail[None, :, 8 - (CONV_W - 1):, :])

    pos_s = jnp.full((db,), past, jnp.int32)
    (q_s, kc_s, vc_s, ks_s, vs_s, kw_s, vw_s, _, _, _, gt_s, hz_s, sm_s) = _inproj(
        x_sample.reshape(db, d), mods_s[0], mods_s[1], norm_a, w_pack, qg, kg, _rope_tables(pos_s), bd,
        tm=db, tiles_per_mod=1, rope_tiles=1)

    def feature_major(a):
        return jnp.transpose(a, (0, 2, 3, 1))

    assert (past + tn) // CMP_STRIDE == past // CMP_STRIDE
    pk_s, pv_s = _cmp_proj_paged(feature_major(cache_cmp_k[0]).reshape(-1, KV_WIDTH, PAGE_SIZE),
                                 feature_major(cache_cmp_v[0]).reshape(-1, KV_WIDTH, PAGE_SIZE), page_table, w1c)
    kcv_s = _cmp_combine(pk_s, pv_s, pe_x, w1c, w2c)
    nsub_s = past // CMP_STRIDE
    ns_s = -(-(past + tn) // SLC_BLK)
    ns_pad = -(-ns_s // LANES) * LANES
    ov_s = _overlap(nsub_s, nsub_s - 1, ns_s, ns_pad)
    o_cmp_s, idx8 = _nsa_dec_select(q_s.reshape(db, NSA_HEADS, HEAD_DIM), kcv_s, ov_s, past)
    idx = idx8[:, :NSA_GROUPS, :N_SEL].reshape(db, NSA_GROUPS * N_SEL)
    new_cols = [a.reshape(db, NSA_GROUPS, HEAD_DIM, 1) for a in (ks_s, vs_s, kw_s, vw_s)]
    per_group = 3 * HPG
    gt4 = jnp.stack([gt_s[:, g * LANES:g * LANES + per_group] for g in range(NSA_GROUPS)], axis=1)
    o_nsa_s, wk_new, wv_new = _nsa_dec_attend(
        idx, page_table, feature_major(cache_slc_k[0]), feature_major(cache_slc_v[0]),
        q_s.astype(F32).reshape(db, NSA_GROUPS, HPG, HEAD_DIM), new_cols,
        feature_major(cache_win_k[0]), feature_major(cache_win_v[0]),
        o_cmp_s.reshape(db, NSA_GROUPS, HPG, HEAD_DIM), gt4.reshape(db, NSA_GROUPS, HPG, 3), past)
    wk_new, wv_new = (jnp.transpose(a, (0, 3, 1, 2)) for a in (wk_new, wv_new))

    o_hg_s, st_s = _hgrn_dec(hz_s, state_hgrn[0], hg_lb, hg_n)
    x1_s, h2_s = _merge(o_nsa_s.reshape(db, NSA_WIDTH), o_hg_s.reshape(db, HG_WIDTH), sm_s, x_sample.reshape(db, d),
                        mods_s[2], mods_s[3], mods_s[4], norm_f, wbn, wbh, wo, tm=db, tiles_per_mod=1)
    buf = state_ffn_conv[0]
    y_s, u_s = _ffn_dec(h2_s, x1_s, mods_s[5].reshape(db, d), wu, conv_w8, conv_b2, wd, buf[:, 1], buf[:, 0])
    conv_s = jnp.stack([buf[:, 1], u_s], axis=1)

    def rows5s(a):
        return a.reshape(1, db, tn, NSA_GROUPS, HEAD_DIM)

    out_s = (rows5s(kc_s), rows5s(vc_s), rows5s(ks_s), rows5s(vs_s),
             wk_new.reshape(1, db, WINDOW, NSA_GROUPS, HEAD_DIM), wv_new.reshape(1, db, WINDOW, NSA_GROUPS, HEAD_DIM),
             st_s[None], conv_s[None])
    return (y_p.reshape(b, t, d), y_s.reshape(db, tn, d)) + out_p + out_s
```

```python
import functools

import numpy as np
import jax
import jax.numpy as jnp
from jax import lax
from jax.experimental import pallas as pl
from jax.experimental.pallas import tpu as pltpu

F32 = jnp.float32
BF16 = jnp.bfloat16

HEAD_DIM = 64
NSA_HEADS = 8
NSA_GROUPS = 2
HPG = NSA_HEADS // NSA_GROUPS
ROT_DIM = HEAD_DIM // 4
ROPE_THETA = 500000.0
CMP_BLK = 32
CMP_STRIDE = 16
SLC_BLK = 64
N_SEL = 16
WINDOW = 512
PAGE_SIZE = 128
HG_HEADS = 4
HG_DIM = 128
HG_SUB_LOG2 = 4
HG_SUB = 1 << HG_SUB_LOG2
HG_SAFE_DECAY = 60.0
SUBLANES = 8
SOFTMAX_VREGS = 32
CONV_W = 3
EPS = 1e-6
NEG = -1e30
SEL_BIAS = -1e9
LOG2E = 1.4426950408889634
LANES = 128
VMEM_LIMIT = 56 * 1024 * 1024

NSA_WIDTH = NSA_HEADS * HEAD_DIM
KV_WIDTH = NSA_GROUPS * HEAD_DIM
HG_WIDTH = HG_HEADS * HG_DIM
GRP_Q = HPG * HEAD_DIM

SEG_Q = 0
SEG_KV = SEG_Q + NSA_WIDTH
SEG_G = SEG_KV + 6 * KV_WIDTH
SEG_H = SEG_G + NSA_GROUPS * LANES
SEG_M = SEG_H + 4 * HG_WIDTH


def _cparams(sem):
    return pltpu.CompilerParams(dimension_semantics=sem, vmem_limit_bytes=VMEM_LIMIT)


def _dot(a, b):
    return jnp.dot(a.astype(BF16), b.astype(BF16), preferred_element_type=F32)


def _dot_nt(a, b):
    return lax.dot_general(a.astype(BF16), b.astype(BF16), (((1,), (1,)), ((), ())), preferred_element_type=F32)


def _split2(x):
    hi = x.astype(BF16)
    lo = (x - hi.astype(F32)).astype(BF16)
    return hi, lo


def _split3(x):
    hi = x.astype(BF16)
    r = x - hi.astype(F32)
    mid = r.astype(BF16)
    lo = (r - mid.astype(F32)).astype(BF16)
    return hi, mid, lo


def _dot_x3(a, b):
    ah, al = _split2(a)
    bh, bl = _split2(b)
    return (jnp.dot(ah, bh, preferred_element_type=F32) + jnp.dot(al, bh, preferred_element_type=F32)
            + jnp.dot(ah, bl, preferred_element_type=F32))


def _sigmoid(x):
    return 1.0 / (1.0 + jnp.exp(-x))


def _const_spec(shape):
    nd = len(shape)
    return pl.BlockSpec(shape, lambda *_: (0,) * nd)


def _resident_spec(shape):
    nd = len(shape)
    return pl.BlockSpec(shape, lambda *_: (0,) * nd, pipeline_mode=pl.Buffered(1))


def _ada_kernel(c_ref, w_ref, b_ref, o_ref):
    o_ref[...] = _dot_x3(c_ref[...], w_ref[...]) + b_ref[...]


def _ada(c_all, w, b):
    r, d = c_all.shape
    n = w.shape[1]
    tn = 1536
    return pl.pallas_call(
        _ada_kernel,
        grid=(n // tn,),
        in_specs=[_const_spec((r, d)), pl.BlockSpec((d, tn), lambda j: (0, j)), pl.BlockSpec((1, tn), lambda j: (0, j))],
        out_specs=pl.BlockSpec((r, tn), lambda j: (0, j)),
        out_shape=jax.ShapeDtypeStruct((r, n), F32),
        compiler_params=_cparams(("arbitrary",)),
        name="ada",
    )(c_all, w, b.reshape(1, n))


def _rope(x, c, s_lo, s_hi):
    width = x.shape[1]
    reps = width // LANES
    if reps > 1:
        c = jnp.concatenate([c] * reps, axis=1)
        s_lo = jnp.concatenate([s_lo] * reps, axis=1)
        s_hi = jnp.concatenate([s_hi] * reps, axis=1)
    half = ROT_DIM // 2
    return x * c + pltpu.roll(x, half, 1) * s_hi + pltpu.roll(x, width - half, 1) * s_lo


def _head_rms(z, bd):
    hi, lo = _split2(z * z)
    ss = jnp.dot(hi, bd, preferred_element_type=F32) + jnp.dot(lo, bd, preferred_element_type=F32)
    return z * lax.rsqrt(ss * (1.0 / HEAD_DIM) + EPS)


def _inproj_kernel(x_ref, sh_ref, sc_ref, ng_ref, w_ref, qg_ref, kg_ref, rc_ref, rlo_ref, rhi_ref, bd_ref,
                   q_ref, kc_ref, vc_ref, ks_ref, vs_ref, kw_ref, vw_ref, kcb_ref, vcb_ref, kvb_ref, gt_ref, hz_ref,
                   sm_ref):
    x = x_ref[...]
    xn = x * lax.rsqrt(jnp.mean(x * x, axis=-1, keepdims=True) + EPS) * ng_ref[...]
    h = (xn * (1.0 + sc_ref[...]) + sh_ref[...]).astype(BF16)
    rc, rlo, rhi = rc_ref[...], rlo_ref[...], rhi_ref[...]
    bd = bd_ref[...]

    zq = jnp.dot(h, w_ref[:, SEG_Q:SEG_KV], preferred_element_type=F32)
    q = _rope(_head_rms(zq, bd) * qg_ref[...], rc, rlo, rhi)
    q_ref[...] = (q * (HEAD_DIM ** -0.5 * LOG2E)).astype(BF16)

    zkv = jnp.dot(h, w_ref[:, SEG_KV:SEG_G], preferred_element_type=F32)
    bd1 = bd[:LANES, :LANES]
    k_refs = (kc_ref, ks_ref, kw_ref)
    v_refs = (vc_ref, vs_ref, vw_ref)
    rows = []
    for j in range(3):
        zk = zkv[:, 2 * j * KV_WIDTH:(2 * j + 1) * KV_WIDTH]
        k = _rope(_head_rms(zk, bd1) * kg_ref[j:j + 1, :], rc, rlo, rhi)
        v = zkv[:, (2 * j + 1) * KV_WIDTH:(2 * j + 2) * KV_WIDTH]
        for g in range(NSA_GROUPS):
            k_refs[j][:, g, :] = k[:, g * HEAD_DIM:(g + 1) * HEAD_DIM]
            v_refs[j][:, g, :] = v[:, g * HEAD_DIM:(g + 1) * HEAD_DIM]
        rows.append((k, v))
    kcb_ref[...] = rows[0][0].astype(BF16)
    vcb_ref[...] = rows[0][1].astype(BF16)
    parts = []
    for g in range(NSA_GROUPS):
        for k, v in rows[1:]:
            parts.append(k[:, g * HEAD_DIM:(g + 1) * HEAD_DIM])
            parts.append(v[:, g * HEAD_DIM:(g + 1) * HEAD_DIM])
    kvb_ref[...] = jnp.concatenate(parts, axis=1).astype(BF16)

    gt_ref[...] = _sigmoid(jnp.dot(h, w_ref[:, SEG_G:SEG_H], preferred_element_type=F32))
    hz_ref[...] = jnp.dot(h, w_ref[:, SEG_H:SEG_M], preferred_element_type=F32)
    sm_ref[...] = _sigmoid(jnp.dot(h, w_ref[:, SEG_M:], preferred_element_type=F32)).astype(BF16)


def _inproj(x2d, shift, scale, norm_g, w_pack, qg, kg, rope_tabs, bd, *, tm, tiles_per_mod, rope_tiles):
    r, d = x2d.shape
    n_tiles = r // tm
    rows_mod = shift.shape[1]
    mod_spec = pl.BlockSpec((None, rows_mod, d), lambda i: (i // tiles_per_mod, 0, 0))
    rope_spec = pl.BlockSpec((tm, LANES), lambda i: (i % rope_tiles, 0))

    def row_spec(width):
        return pl.BlockSpec((tm, width), lambda i: (i, 0))

    widths = (KV_WIDTH, KV_WIDTH, 4 * KV_WIDTH, NSA_GROUPS * LANES, 4 * HG_WIDTH, w_pack.shape[1] - SEG_M)
    dtypes = (BF16, BF16, BF16, F32, F32, BF16)
    cache_spec = pl.BlockSpec((tm, NSA_GROUPS, HEAD_DIM), lambda i: (i, 0, 0))
    cache_shape = jax.ShapeDtypeStruct((r, NSA_GROUPS, HEAD_DIM), F32)
    return pl.pallas_call(
        _inproj_kernel,
        grid=(n_tiles,),
        in_specs=[row_spec(d), mod_spec, mod_spec, _const_spec((1, d)), _resident_spec(w_pack.shape),
                  _const_spec(qg.shape), _const_spec(kg.shape), rope_spec, rope_spec, rope_spec, _const_spec(bd.shape)],
        out_specs=[row_spec(NSA_WIDTH)] + [cache_spec] * 6 + [row_spec(w) for w in widths],
        out_shape=[jax.ShapeDtypeStruct((r, NSA_WIDTH), BF16)] + [cache_shape] * 6
        + [jax.ShapeDtypeStruct((r, w), dt) for w, dt in zip(widths, dtypes)],
        compiler_params=_cparams(("arbitrary",)),
        name="inproj",
    )(x2d, shift, scale, norm_g, w_pack, qg, kg, *rope_tabs, bd)


def _mm_kernel(a_ref, w_ref, o_ref):
    o_ref[...] = _dot(a_ref[...], w_ref[...])


def _mm(a, w, tm, name):
    m, k = a.shape
    n = w.shape[1]
    tm = min(tm, m)
    return pl.pallas_call(
        _mm_kernel,
        grid=(m // tm,),
        in_specs=[pl.BlockSpec((tm, k), lambda i: (i, 0)), _const_spec((k, n))],
        out_specs=pl.BlockSpec((tm, n), lambda i: (i, 0)),
        out_shape=jax.ShapeDtypeStruct((m, n), F32),
        compiler_params=_cparams(("arbitrary",)),
        name=name,
    )(a, w)


PAGES_PER_STEP = 16


def _cmp_proj_paged_kernel(pt_ref, *refs):
    k_pages, v_pages = refs[:PAGES_PER_STEP], refs[PAGES_PER_STEP:2 * PAGES_PER_STEP]
    perm_ref, w_ref, ok_ref, ov_ref, xk_ref, xv_ref = refs[2 * PAGES_PER_STEP:]
    sub = PAGE_SIZE // CMP_STRIDE
    perm = perm_ref[...]
    half = PAGES_PER_STEP // 2
    for kv, (pages, x_ref, o_ref) in enumerate(((k_pages, xk_ref, ok_ref), (v_pages, xv_ref, ov_ref))):
        for j0 in (0, half):
            stacked = jnp.concatenate([pr[...].astype(BF16) for pr in pages[j0:j0 + half]], axis=0)
            rows = _dot_nt(perm, stacked)
            for s in range(CMP_STRIDE):
                for j in range(half):
                    x_ref[(j0 + j) * sub:(j0 + j + 1) * sub, s * KV_WIDTH:(s + 1) * KV_WIDTH] = (
                        rows[s * sub:(s + 1) * sub, j * KV_WIDTH:(j + 1) * KV_WIDTH])
        o_ref[...] = _dot(x_ref[...], w_ref[kv])


def _cmp_proj_paged(pool_k, pool_v, page_table, w):
    db, n_pages = page_table.shape
    sub = PAGE_SIZE // CMP_STRIDE
    n = w.shape[2]
    steps = n_pages // PAGES_PER_STEP
    pos = np.arange(PAGE_SIZE)
    perm = jnp.asarray((pos[:, None] % sub) * CMP_STRIDE + pos[:, None] // sub == pos[None, :], BF16)
    page_specs = [pl.BlockSpec((None, KV_WIDTH, PAGE_SIZE), lambda b, c, pt, j=j: (pt[b, c * PAGES_PER_STEP + j], 0, 0))
                  for j in range(PAGES_PER_STEP)]
    out_spec = pl.BlockSpec((None, PAGES_PER_STEP * sub, n), lambda b, c, pt: (b, c, 0))
    rows_scratch = pltpu.VMEM((PAGES_PER_STEP * sub, CMP_STRIDE * KV_WIDTH), F32)
    grid_spec = pltpu.PrefetchScalarGridSpec(
        num_scalar_prefetch=1,
        grid=(db, steps),
        in_specs=page_specs + page_specs + [pl.BlockSpec(perm.shape, lambda b, c, pt: (0, 0)),
                                            pl.BlockSpec(w.shape, lambda b, c, pt: (0, 0, 0))],
        out_specs=[out_spec, out_spec],
        scratch_shapes=[rows_scratch, rows_scratch],
    )
    out = jax.ShapeDtypeStruct((db, n_pages * sub, n), F32)
    return pl.pallas_call(
        _cmp_proj_paged_kernel,
        grid_spec=grid_spec,
        out_shape=[out, out],
        compiler_params=_cparams(("arbitrary", "arbitrary")),
        name="cmp_proj_paged",
    )(page_table, *([pool_k] * PAGES_PER_STEP), *([pool_v] * PAGES_PER_STEP), perm, w)


def _cmp_combine_kernel(pk_ref, pv_ref, pe_ref, w1_ref, w2_ref, o_ref):
    n = pk_ref.shape[0]
    outs = []
    for j, p_ref in enumerate((pk_ref, pv_ref)):
        pe = jnp.dot(pe_ref[j], w1_ref[j], preferred_element_type=F32)
        pe_hid = pe[0:1, :KV_WIDTH] + pe[1:2, KV_WIDTH:]
        p = p_ref[...]
        nxt = pltpu.roll(p[:, KV_WIDTH:], n - 1, 0)
        hid = pe_hid + p[:, :KV_WIDTH] + nxt
        outs.append(_dot(jax.nn.gelu(hid), w2_ref[j]))
    for g in range(NSA_GROUPS):
        o_ref[g] = jnp.concatenate([o[:, g * HEAD_DIM:(g + 1) * HEAD_DIM] for o in outs], axis=1).astype(BF16)


def _cmp_combine(pk, pv, pe_x, w1, w2bd):
    bx, n, c = pk.shape
    return pl.pallas_call(
        _cmp_combine_kernel,
        grid=(bx,),
        in_specs=[pl.BlockSpec((None, n, c), lambda b: (b, 0, 0)), pl.BlockSpec((None, n, c), lambda b: (b, 0, 0)),
                  _const_spec(pe_x.shape), _const_spec(w1.shape), _const_spec(w2bd.shape)],
        out_specs=pl.BlockSpec((None, NSA_GROUPS, n, 2 * HEAD_DIM), lambda b: (b, 0, 0, 0)),
        out_shape=jax.ShapeDtypeStruct((bx, NSA_GROUPS, n, 2 * HEAD_DIM), BF16),
        compiler_params=_cparams(("arbitrary",)),
        name="cmp_combine",
    )(pk, pv, pe_x, w1, w2bd)


def _masked_softmax(s, mask):
    s = jnp.where(mask, s, NEG)
    m = jnp.max(s, axis=-1, keepdims=True)
    e = jnp.where(mask, jnp.exp2(s - m), 0.0)
    return e / jnp.maximum(jnp.sum(e, axis=-1, keepdims=True), 1e-30)


def _top_k_mask(score, k, axis, idx_out=False):
    n = score.shape[axis]
    pos = lax.broadcasted_iota(jnp.int32, score.shape, axis).astype(F32)
    idx = jnp.zeros((score.shape[0], LANES), jnp.int32)
    lane_out = lax.broadcasted_iota(jnp.int32, (score.shape[0], LANES), 1)
    work = score
    for it in range(k):
        mx = jnp.max(work, axis=axis, keepdims=True)
        first = jnp.min(jnp.where(work == mx, pos, float(n)), axis=axis, keepdims=True)
        work = jnp.where(pos == first, -jnp.inf, work)
        if idx_out:
            idx = jnp.where(lane_out == it, first.astype(jnp.int32), idx)
    sel = work == -jnp.inf
    return (sel, idx) if idx_out else sel


def _block_scores(imp, qpos, axis):
    blk = lax.broadcasted_iota(jnp.int32, imp.shape, axis)
    cur = qpos >> 6
    valid = (blk << 6) <= qpos
    forced = (blk == 0) | (blk == cur) | (blk == cur - 1)
    return jnp.where(forced, 1e9, jnp.where(valid, imp, -1e9))


def _nsa_prompt_kernel(q_ref, kcv_ref, kvb_ref, gt_ref, ov_ref, o_ref, *scr, tq, tk):
    qi = pl.program_id(2)
    q0 = qi * tq
    nc = kcv_ref.shape[0]
    rows = HPG * tq

    q = q_ref[...]
    qs = jnp.concatenate([q[:, h * HEAD_DIM:(h + 1) * HEAD_DIM] for h in range(HPG)], axis=0)
    qpos = q0 + lax.broadcasted_iota(jnp.int32, (tq, 1), 0)
    qpos4 = jnp.concatenate([qpos] * HPG, axis=0)

    lane_blk = lax.broadcasted_iota(jnp.int32, (1, LANES), 1)
    kcol = lax.broadcasted_iota(jnp.int32, (tk, 1), 0)
    krow = lax.broadcasted_iota(jnp.int32, (1, tk), 1)
    ones_v = jnp.ones((tk, LANES - HEAD_DIM), BF16)
    rb = SOFTMAX_VREGS * SUBLANES * LANES // tk
    qpos_rb = [q0 + (r % (tq // rb)) * rb + lax.broadcasted_iota(jnp.int32, (rb, 1), 0) for r in range(rows // rb)]

    half = rows // 2

    def key_tile(j):
        return kvb_ref[pl.ds(pl.multiple_of(j * tk, tk), tk), :]

    def branch(s0, s1, p0, p1, a0, a1, m_scr, acc_scr, slc):
        s_bufs, p_bufs, a_bufs = (s0, s1), (p0, p1), (a0, a1)
        v_lane0 = HEAD_DIM if slc else 3 * HEAD_DIM

        def scores(slot, j):
            blk = key_tile(j)
            if slc:
                onehot = jnp.where(((j * tk + kcol) >> 6) == lane_blk, 1.0, 0.0).astype(BF16)
                q_lhs, k_rhs = q_aug, jnp.concatenate([onehot, blk[:, 0:HEAD_DIM]], axis=1)
            else:
                q_lhs, k_rhs = qs, blk[:, 2 * HEAD_DIM:3 * HEAD_DIM]
            for h0 in (0, half):
                s_bufs[slot][h0:h0 + half, :] = _dot_nt(q_lhs[h0:h0 + half], k_rhs)

        def softmax(slot, j, mask_kind, live=None):
            k0 = j * tk
            for r in range(rows // rb):
                sl = slice(r * rb, (r + 1) * rb)
                s = s_bufs[slot][sl, :]
                keep = None
                if mask_kind == "causal":
                    keep = (k0 + krow) <= qpos_rb[r]
                elif mask_kind == "window":
                    keep = qpos_rb[r] - (k0 + krow) < WINDOW
                if live is not None:
                    keep = live if keep is None else jnp.logical_and(keep, live)
                if keep is not None:
                    s = jnp.where(keep, s, NEG)
                m_old = m_scr[sl, :]
                m_new = jnp.maximum(m_old, jnp.max(s, axis=-1, keepdims=True))
                a_bufs[slot][sl, :] = jnp.exp2(m_old - m_new)
                m_scr[sl, :] = m_new
                p_bufs[slot][sl, :] = jnp.exp2(s - jnp.concatenate([m_new] * (tk // LANES), axis=1)).astype(BF16)

        def accumulate(slot, j):
            v_aug = jnp.concatenate([key_tile(j)[:, v_lane0:v_lane0 + HEAD_DIM], ones_v], axis=1)
            for h0 in (0, half):
                sl = slice(h0, h0 + half)
                acc_scr[sl, :] = a_bufs[slot][sl, :] * acc_scr[sl, :] + jnp.dot(p_bufs[slot][sl, :], v_aug,
                                                                               preferred_element_type=F32)

        def reset():
            m_scr[...] = jnp.full(m_scr.shape, NEG, F32)
            acc_scr[...] = jnp.zeros(acc_scr.shape, F32)

        def result():
            acc = acc_scr[...]
            return acc[:, :HEAD_DIM] / jnp.maximum(acc[:, HEAD_DIM:HEAD_DIM + 1], 1e-30)

        return scores, softmax, accumulate, reset, result

    jd = q0 // tk
    n_scr = len(scr) // 2

    scores, softmax, accumulate, reset, result = branch(*scr[n_scr:], slc=False)
    reset()
    n_back = -(-(WINDOW - 1) // tk)
    tiles = [(jd, "causal", None)] + [
        (jnp.maximum(jd - back, 0), "window" if (back + 1) * tk - 1 >= WINDOW else None, jd >= back)
        for back in range(1, n_back + 1)]
    scores(0, tiles[0][0])
    for i, (j, mask_kind, live) in enumerate(tiles):
        if i + 1 < len(tiles):
            scores((i + 1) % 2, tiles[i + 1][0])
        softmax(i % 2, j, mask_kind, live)
        accumulate(i % 2, j)
    o_win = result()

    kcv = kcv_ref[...]
    kc, vc = kcv[:, :HEAD_DIM], kcv[:, HEAD_DIM:]
    cmp_end = lax.broadcasted_iota(jnp.int32, (1, nc), 1) * CMP_STRIDE + (CMP_BLK - 1)
    p = _masked_softmax(_dot_nt(qs, kc), cmp_end <= qpos4)
    o_cmp = _dot(p, vc)
    psum = p[0:tq]
    for h in range(1, HPG):
        psum = psum + p[h * tq:(h + 1) * tq]
    p_hi, p_lo = _split2(psum)
    ov_t = ov_ref[...]
    imp_t = _dot_nt(ov_t, p_hi) + _dot_nt(ov_t, p_lo)

    qpos_row = q0 + lax.broadcasted_iota(jnp.int32, (1, tq), 1)
    sel_t = _top_k_mask(_block_scores(imp_t, qpos_row, 0), N_SEL, 0)
    bias = jnp.where(sel_t, 0.0, SEL_BIAS).T.astype(BF16)
    q_aug = jnp.concatenate([jnp.concatenate([bias] * HPG, axis=0), qs], axis=1)

    scores, softmax, accumulate, reset, result = branch(*scr[:n_scr], slc=True)
    reset()
    scores(0, jd)
    scores(1, 0)
    softmax(0, jd, "causal")

    def past_pair(u, carry):
        ta, tb = 2 * u, 2 * u + 1
        scores(0, jnp.minimum(tb, jd))
        softmax(1, ta, None)
        accumulate(0, jnp.where(u == 0, jd, ta - 1))
        scores(1, jnp.minimum(tb + 1, jd))
        softmax(0, tb, None, live=tb < jd)
        accumulate(1, ta)
        return carry

    n_pairs = (jd + 1) // 2
    lax.fori_loop(0, n_pairs, past_pair, 0)
    accumulate(0, jnp.where(n_pairs == 0, jd, jnp.minimum(2 * n_pairs - 1, jd)))
    o_slc = result()

    gt = gt_ref[...]
    outs = []
    for h in range(HPG):
        r0 = h * tq
        outs.append(gt[:, 3 * h:3 * h + 1] * o_cmp[r0:r0 + tq] + gt[:, 3 * h + 1:3 * h + 2] * o_slc[r0:r0 + tq]
                    + gt[:, 3 * h + 2:3 * h + 3] * o_win[r0:r0 + tq])
    o_ref[...] = jnp.concatenate(outs, axis=1).astype(BF16)


def _nsa_prompt(q, kcv, kvb, gt, ov, *, tq, tk):
    b, t, _ = q.shape
    nc = kcv.shape[2]
    assert tk % tq == 0 and t % tk == 0 and tk <= WINDOW
    return pl.pallas_call(
        functools.partial(_nsa_prompt_kernel, tq=tq, tk=tk),
        grid=(b, NSA_GROUPS, t // tq),
        in_specs=[pl.BlockSpec((None, tq, GRP_Q), lambda bi, g, i: (bi, i, g)),
                  pl.BlockSpec((None, None, nc, 2 * HEAD_DIM), lambda bi, g, i: (bi, g, 0, 0)),
                  pl.BlockSpec((None, t, 4 * HEAD_DIM), lambda bi, g, i: (bi, 0, g)),
                  pl.BlockSpec((None, tq, LANES), lambda bi, g, i: (bi, i, g)),
                  _const_spec(ov.shape)],
        out_specs=pl.BlockSpec((None, tq, GRP_Q), lambda bi, g, i: (bi, i, g)),
        out_shape=jax.ShapeDtypeStruct((b, t, NSA_WIDTH), BF16),
        scratch_shapes=([pltpu.VMEM((HPG * tq, tk), F32)] * 2 + [pltpu.VMEM((HPG * tq, tk), BF16)] * 2
                        + [pltpu.VMEM((HPG * tq, LANES), F32)] * 4) * 2,
        compiler_params=_cparams(("arbitrary", "arbitrary", "arbitrary")),
        name="nsa_prompt",
    )(q, kcv, kvb, gt, ov)


def _nsa_dec_select_kernel(q_ref, kcv_ref, ov_ref, ocmp_ref, idx_ref, *, qpos):
    nc = kcv_ref.shape[1]
    q = q_ref[...]
    cmp_end = lax.broadcasted_iota(jnp.int32, (1, nc), 1) * CMP_STRIDE + (CMP_BLK - 1)
    mask = cmp_end <= qpos
    ov = ov_ref[...]
    imps = []
    for g in range(NSA_GROUPS):
        kcv = kcv_ref[g]
        p = _masked_softmax(_dot_nt(q[g * HPG:(g + 1) * HPG], kcv[:, :HEAD_DIM]), mask)
        ocmp_ref[g * HPG:(g + 1) * HPG, :] = _dot(p, kcv[:, HEAD_DIM:])
        p_hi, p_lo = _split2(jnp.sum(p, axis=0, keepdims=True))
        imps.append(jnp.dot(p_hi, ov, preferred_element_type=F32) + jnp.dot(p_lo, ov, preferred_element_type=F32))
    imp = jnp.concatenate(imps + [jnp.zeros((8 - NSA_GROUPS, ov.shape[1]), F32)], axis=0)
    score = _block_scores(imp, jnp.full((8, 1), qpos, jnp.int32), 1)
    _, idx = _top_k_mask(score, N_SEL, 1, idx_out=True)
    idx_ref[...] = idx


def _nsa_dec_select(q3, kcv, ov, qpos):
    db, nh, _ = q3.shape
    nc = kcv.shape[2]
    return pl.pallas_call(
        functools.partial(_nsa_dec_select_kernel, qpos=qpos),
        grid=(db,),
        in_specs=[pl.BlockSpec((None, nh, HEAD_DIM), lambda b: (b, 0, 0)),
                  pl.BlockSpec((None, NSA_GROUPS, nc, 2 * HEAD_DIM), lambda b: (b, 0, 0, 0)),
                  _const_spec(ov.shape)],
        out_specs=[pl.BlockSpec((None, nh, HEAD_DIM), lambda b: (b, 0, 0)),
                   pl.BlockSpec((None, 8, LANES), lambda b: (b, 0, 0))],
        out_shape=[jax.ShapeDtypeStruct((db, nh, HEAD_DIM), F32), jax.ShapeDtypeStruct((db, 8, LANES), jnp.int32)],
        compiler_params=_cparams(("arbitrary",)),
        name="nsa_dec_select",
    )(q3, kcv, ov)


def _nsa_dec_attend_kernel(idx_ref, pt_ref, *refs, past):
    k_refs = refs[:N_SEL]
    v_refs = refs[N_SEL:2 * N_SEL]
    (q_ref, nk_ref, nv_ref, nwk_ref, nwv_ref, wk_ref, wv_ref, ocmp_ref, gt_ref, o_ref, wko_ref, wvo_ref) = refs[2 * N_SEL:]
    b = pl.program_id(0)
    g = pl.program_id(1)
    win = wk_ref.shape[1]
    q = q_ref[...]
    lane_blk = lax.broadcasted_iota(jnp.int32, (1, SLC_BLK), 1)

    s_parts, v_parts, pos_parts = [], [], []
    for i in range(N_SEL):
        blk = idx_ref[b, g * N_SEL + i]
        first_half = blk % (PAGE_SIZE // SLC_BLK) == 0
        kpos = blk * SLC_BLK + lane_blk
        from_new = kpos >= past
        k_t, v_t = k_refs[i][...], v_refs[i][...]
        k_t = jnp.where(from_new, nk_ref[...], jnp.where(first_half, k_t[:, :SLC_BLK], k_t[:, SLC_BLK:]))
        v_t = jnp.where(from_new, nv_ref[...], jnp.where(first_half, v_t[:, :SLC_BLK], v_t[:, SLC_BLK:]))
        s_parts.append(_dot(q, k_t))
        v_parts.append(v_t)
        pos_parts.append(kpos)
    p = _masked_softmax(jnp.concatenate(s_parts, axis=1), jnp.concatenate(pos_parts, axis=1) <= past)
    o_slc = _dot_nt(p, jnp.concatenate(v_parts, axis=1))

    lane = lax.broadcasted_iota(jnp.int32, (1, win), 1)
    wk_new = jnp.where(lane == win - 1, nwk_ref[...], pltpu.roll(wk_ref[...], win - 1, 1))
    wv_new = jnp.where(lane == win - 1, nwv_ref[...], pltpu.roll(wv_ref[...], win - 1, 1))
    wko_ref[...] = wk_new
    wvo_ref[...] = wv_new
    p = _masked_softmax(_dot(q, wk_new), (win - 1 - lane) < WINDOW)
    o_win = _dot_nt(p, wv_new)

    gt = gt_ref[...]
    o_ref[...] = gt[:, 0:1] * ocmp_ref[...] + gt[:, 1:2] * o_slc + gt[:, 2:3] * o_win


def _nsa_dec_attend(idx, page_table, pool_k, pool_v, q4, new_cols, win_k, win_v, o_cmp4, gt4, past):
    db, n_pages = page_table.shape
    win = win_k.shape[3]
    blocks_per_page = PAGE_SIZE // SLC_BLK
    assert blocks_per_page == 2

    def blk_map(i):
        def index(b, g, idx_s, pt_s):
            page = jnp.minimum(idx_s[b, g * N_SEL + i] // blocks_per_page, n_pages - 1)
            return pt_s[b, page], g, 0, 0
        return index

    sel_specs = [pl.BlockSpec((None, None, HEAD_DIM, PAGE_SIZE), blk_map(i)) for i in range(N_SEL)]

    def per_bg(shape):
        return pl.BlockSpec((None, None) + shape, lambda b, g, *_: (b, g, 0, 0))

    grid_spec = pltpu.PrefetchScalarGridSpec(
        num_scalar_prefetch=2,
        grid=(db, NSA_GROUPS),
        in_specs=sel_specs + sel_specs + [per_bg((HPG, HEAD_DIM))] + [per_bg((HEAD_DIM, 1))] * 4
        + [per_bg((HEAD_DIM, win))] * 2 + [per_bg((HPG, HEAD_DIM)), per_bg((HPG, 3))],
        out_specs=[per_bg((HPG, HEAD_DIM)), per_bg((HEAD_DIM, win)), per_bg((HEAD_DIM, win))],
    )
    return pl.pallas_call(
        functools.partial(_nsa_dec_attend_kernel, past=past),
        grid_spec=grid_spec,
        out_shape=[jax.ShapeDtypeStruct((db, NSA_GROUPS, HPG, HEAD_DIM), F32),
                   jax.ShapeDtypeStruct(win_k.shape, F32), jax.ShapeDtypeStruct(win_v.shape, F32)],
        compiler_params=_cparams(("arbitrary", "arbitrary")),
        name="nsa_dec_attend",
    )(idx, page_table, *([pool_k] * N_SEL), *([pool_v] * N_SEL), q4, *new_cols, win_k, win_v, o_cmp4, gt4)


def _forget_lower_bound(lb_raw):
    m = jnp.max(lb_raw, axis=0, keepdims=True)
    e = jnp.exp(lb_raw - m)
    return e[0:1] / jnp.sum(e, axis=0, keepdims=True)


def _hgrn_out(o, hg, norm_g):
    o = o * lax.rsqrt(jnp.mean(o * o, axis=-1, keepdims=True) + EPS) * norm_g
    return o * (hg * _sigmoid(hg))


def _hgrn_chunk_common(q, hf, v, lb, tri, st, b_s, k_s, v_s):
    heads = range(len(q))
    lc = q[0].shape[0]
    f = [lb[h] + (1.0 - lb[h]) * _sigmoid(hf[h]) for h in heads]
    k = [1.0 - f[h] for h in heads]
    parts = [p for h in heads for p in _split3(jnp.log(f[h]))]
    csum = jnp.dot(tri, jnp.concatenate(parts, axis=1), preferred_element_type=F32)
    bcum = [csum[:, 3 * h * HG_DIM:(3 * h + 1) * HG_DIM] + csum[:, (3 * h + 1) * HG_DIM:(3 * h + 2) * HG_DIM]
            + csum[:, (3 * h + 2) * HG_DIM:(3 * h + 3) * HG_DIM] for h in heads]
    for h in heads:
        b_s[h][...] = bcum[h]
        k_s[h][...] = k[h]
        v_s[h][...] = v[h]

    o_inter = [_dot_nt(q[h] * jnp.exp(bcum[h]), st[h]) for h in heads]
    span = None
    for i in range(lc // HG_SUB):
        r0 = i * HG_SUB
        for h in heads:
            d_i = bcum[h][r0:r0 + 1] - bcum[h][r0 + HG_SUB - 1:r0 + HG_SUB]
            span = d_i if span is None else jnp.maximum(span, d_i)
    st_new = [st[h] * jnp.exp(bcum[h][lc - 1:lc]) + _dot(v[h].T, k[h] * jnp.exp(bcum[h][lc - 1:lc] - bcum[h]))
              for h in heads]
    return o_inter, k, bcum, st_new, span


def _hgrn_intra_matmul(o_inter, q, k, bcum, v, o_s):
    heads = range(len(q))
    lc = q[0].shape[0]
    n_sub = lc // HG_SUB
    row_sub = lax.broadcasted_iota(jnp.int32, (lc, HG_DIM), 0) >> HG_SUB_LOG2
    causal = lax.broadcasted_iota(jnp.int32, (lc, lc), 1) <= lax.broadcasted_iota(jnp.int32, (lc, lc), 0)
    for h in heads:
        ref = [bcum[h][j * HG_SUB:j * HG_SUB + 1] for j in range(n_sub)]
        ref_rows = jnp.concatenate([jnp.broadcast_to(r, (HG_SUB, HG_DIM)) for r in ref], axis=0)
        q_t = q[h] * jnp.exp(bcum[h] - ref_rows)
        k_t = k[h] * jnp.exp(ref_rows - bcum[h])
        q_cols = [jnp.where(row_sub >= j, q_t * jnp.exp(jnp.minimum(ref_rows - ref[j], 0.0)), 0.0).astype(BF16)
                  for j in range(n_sub)]
        k_cols = [jnp.where(row_sub == j, k_t, 0.0).astype(BF16) for j in range(n_sub)]
        a = _dot_nt(jnp.concatenate(q_cols, axis=1), jnp.concatenate(k_cols, axis=1))
        o_s[h][...] = o_inter[h] + _dot(jnp.where(causal, a, 0.0), v[h])


def _hgrn_intra_direct(o_inter, q, k, bcum, v, b_s, k_s, v_s, o_s):
    lc = q.shape[0]
    t_idx = lax.broadcasted_iota(jnp.int32, (SUBLANES, 1), 0)
    for i in range(lc // HG_SUB):
        r0 = i * HG_SUB
        o_i = o_inter[r0:r0 + HG_SUB]
        if i > 0:
            ref_b = bcum[r0:r0 + 1]
            a = _dot_nt(q[r0:r0 + HG_SUB] * jnp.exp(bcum[r0:r0 + HG_SUB] - ref_b), k[:r0] * jnp.exp(ref_b - bcum[:r0]))
            o_i = o_i + _dot(a, v[:r0])
        for t0 in range(r0, r0 + HG_SUB, SUBLANES):
            b_t, q_t = bcum[t0:t0 + SUBLANES], q[t0:t0 + SUBLANES]
            acc = o_i[t0 - r0:t0 - r0 + SUBLANES]
            for s in range(r0, t0 + SUBLANES):
                d = b_t - b_s[s:s + 1, :]
                if s >= t0:
                    d = jnp.where(t_idx >= s - t0, d, NEG)
                w = jnp.exp(d) * q_t * k_s[s:s + 1, :]
                acc = acc + jnp.sum(w, axis=-1, keepdims=True) * v_s[s:s + 1, :]
            o_s[t0:t0 + SUBLANES, :] = acc


def _hgrn_prompt_kernel(hz_ref, lb_ref, ng_ref, tri_ref, o_ref, st_ref, state_t, b_s, k_s, v_s, o_s):
    c = pl.program_id(1)
    n_c = pl.num_programs(1)

    @pl.when(c == 0)
    def _():
        state_t[...] = jnp.zeros_like(state_t)

    lb_all = _forget_lower_bound(lb_ref[...])
    tri = tri_ref[...]

    def seg(j, h):
        return hz_ref[:, j * HG_WIDTH + h * HG_DIM:j * HG_WIDTH + (h + 1) * HG_DIM]

    heads = range(HG_HEADS)
    q, v = [seg(0, h) for h in heads], [seg(2, h) for h in heads]
    scratch = [[s.at[h] for h in heads] for s in (b_s, k_s, v_s, o_s)]
    o_inter, k, bcum, st_new, span = _hgrn_chunk_common(
        q, [seg(1, h) for h in heads], v, [lb_all[:, h * HG_DIM:(h + 1) * HG_DIM] for h in heads], tri,
        [state_t[h] for h in heads], *scratch[:3])
    for h in heads:
        state_t[h] = st_new[h]
    direct = jnp.max(span) > HG_SAFE_DECAY

    @pl.when(jnp.logical_not(direct))
    def _():
        _hgrn_intra_matmul(o_inter, q, k, bcum, v, scratch[3])

    @pl.when(direct)
    def _():
        for h in heads:
            _hgrn_intra_direct(o_inter[h], q[h], k[h], bcum[h], v[h], *(s[h] for s in scratch))

    for h in range(HG_HEADS):
        o_ref[:, h * HG_DIM:(h + 1) * HG_DIM] = _hgrn_out(o_s[h], seg(3, h), ng_ref[...]).astype(BF16)

    @pl.when(c == n_c - 1)
    def _():
        for h in range(HG_HEADS):
            st_ref[h] = state_t[h].T


def _hgrn_prompt(hz3, hg_lb, norm_g, *, lc):
    b, t, w = hz3.shape
    tri = jnp.tril(jnp.ones((lc, lc), F32)).astype(BF16)
    return pl.pallas_call(
        _hgrn_prompt_kernel,
        grid=(b, t // lc),
        in_specs=[pl.BlockSpec((None, lc, w), lambda bi, c: (bi, c, 0)), _const_spec(hg_lb.shape),
                  _const_spec((1, HG_DIM)), _const_spec((lc, lc))],
        out_specs=[pl.BlockSpec((None, lc, HG_WIDTH), lambda bi, c: (bi, c, 0)),
                   pl.BlockSpec((None, HG_HEADS, HG_DIM, HG_DIM), lambda bi, c: (bi, 0, 0, 0))],
        out_shape=[jax.ShapeDtypeStruct((b, t, HG_WIDTH), BF16),
                   jax.ShapeDtypeStruct((b, HG_HEADS, HG_DIM, HG_DIM), F32)],
        scratch_shapes=[pltpu.VMEM((HG_HEADS, HG_DIM, HG_DIM), F32)] + [pltpu.VMEM((HG_HEADS, lc, HG_DIM), F32)] * 4,
        compiler_params=_cparams(("arbitrary", "arbitrary")),
        name="hgrn_prompt",
    )(hz3, hg_lb, norm_g, tri)


def _hgrn_dec_kernel(hz_ref, s0_ref, lb_ref, ng_ref, o_ref, s_ref):
    hz = hz_ref[...]
    lb_all = _forget_lower_bound(lb_ref[...])
    eye = lax.broadcasted_iota(jnp.int32, (HG_DIM, HG_DIM), 0) == lax.broadcasted_iota(jnp.int32, (HG_DIM, HG_DIM), 1)

    def column(row):
        return jnp.sum(jnp.where(eye, row, 0.0), axis=1, keepdims=True)

    outs = []
    for h in range(HG_HEADS):
        def seg(j):
            return hz[:, j * HG_WIDTH + h * HG_DIM:j * HG_WIDTH + (h + 1) * HG_DIM]

        lb = lb_all[:, h * HG_DIM:(h + 1) * HG_DIM]
        f = lb + (1.0 - lb) * _sigmoid(seg(1))
        q, k, v = seg(0), 1.0 - f, seg(2)
        s0 = s0_ref[h]
        s_ref[h] = column(f) * s0 + column(k) * v
        o = jnp.sum(column(q * f) * s0, axis=0, keepdims=True) + jnp.sum(q * k, axis=-1, keepdims=True) * v
        outs.append(_hgrn_out(o, seg(3), ng_ref[...]))
    o_ref[...] = jnp.concatenate(outs, axis=1)


def _hgrn_dec(hz, s0, hg_lb, norm_g):
    db = hz.shape[0]
    return pl.pallas_call(
        _hgrn_dec_kernel,
        grid=(db,),
        in_specs=[pl.BlockSpec((None, 1, hz.shape[1]), lambda b: (b, 0, 0)),
                  pl.BlockSpec((None, HG_HEADS, HG_DIM, HG_DIM), lambda b: (b, 0, 0, 0)),
                  _const_spec(hg_lb.shape), _const_spec((1, HG_DIM))],
        out_specs=[pl.BlockSpec((None, 1, HG_WIDTH), lambda b: (b, 0, 0)),
                   pl.BlockSpec((None, HG_HEADS, HG_DIM, HG_DIM), lambda b: (b, 0, 0, 0))],
        out_shape=[jax.ShapeDtypeStruct((db, 1, HG_WIDTH), F32), jax.ShapeDtypeStruct(s0.shape, F32)],
        compiler_params=_cparams(("arbitrary",)),
        name="hgrn_dec",
    )(hz.reshape(db, 1, -1), s0, hg_lb, norm_g)


def _merge_core(on_ref, oh_ref, sm_ref, x_ref, g1_ref, sh_ref, sc_ref, ng_ref, wn_ref, wh_ref, wo_ref):
    d = x_ref.shape[1]
    sm = sm_ref[...]
    mix = sm[:, :d] * _dot(on_ref[...], wn_ref[...]) + sm[:, d:] * _dot(oh_ref[...], wh_ref[...])
    x1 = x_ref[...] + g1_ref[...] * _dot(mix, wo_ref[...])
    xn = x1 * lax.rsqrt(jnp.mean(x1 * x1, axis=-1, keepdims=True) + EPS) * ng_ref[...]
    return x1, (xn * (1.0 + sc_ref[...]) + sh_ref[...]).astype(BF16)


def _merge_kernel(*refs):
    x1_ref, h2_ref = refs[-2:]
    x1_ref[...], h2_ref[...] = _merge_core(*refs[:-2])


def _merge(o_nsa, o_hg, sm, x2d, g1, sh2, sc2, norm_g, w_br_nsa, w_br_hg, w_out, *, tm, tiles_per_mod):
    r, d = x2d.shape
    rows_mod = g1.shape[1]
    mod_spec = pl.BlockSpec((None, rows_mod, d), lambda i: (i // tiles_per_mod, 0, 0))

    def row_spec(width):
        return pl.BlockSpec((tm, width), lambda i: (i, 0))

    return pl.pallas_call(
        _merge_kernel,
        grid=(r // tm,),
        in_specs=[row_spec(o_nsa.shape[1]), row_spec(o_hg.shape[1]), row_spec(sm.shape[1]), row_spec(d),
                  mod_spec, mod_spec, mod_spec, _const_spec((1, d)),
                  _const_spec(w_br_nsa.shape), _const_spec(w_br_hg.shape), _const_spec(w_out.shape)],
        out_specs=[row_spec(d), row_spec(d)],
        out_shape=[jax.ShapeDtypeStruct((r, d), F32), jax.ShapeDtypeStruct((r, d), BF16)],
        compiler_params=_cparams(("arbitrary",)),
        name="merge",
    )(o_nsa, o_hg, sm, x2d, g1, sh2, sc2, norm_g, w_br_nsa, w_br_hg, w_out)


FF_CHUNKS = 2


def _ffn_core(h2, x1, g2, wu_ref, cw_ref, cb_ref, wd_ref, prev_rows):
    d_ff = wd_ref.shape[0]
    ch = d_ff // FF_CHUNKS
    acc = None
    us = []
    for ci in range(FF_CHUNKS):
        c0, c1 = ci * ch, (ci + 1) * ch
        u = jnp.dot(h2, wu_ref[:, c0:c1], preferred_element_type=F32)
        v = jnp.dot(h2, wu_ref[:, d_ff + c0:d_ff + c1], preferred_element_type=F32)
        u_m1, u_m2 = prev_rows(u, c0, c1)
        y = cb_ref[:, c0:c1] + cw_ref[0:1, c0:c1] * u_m2 + cw_ref[1:2, c0:c1] * u_m1 + cw_ref[2:3, c0:c1] * u
        part = _dot(y * _sigmoid(y) * v, wd_ref[c0:c1, :])
        acc = part if acc is None else acc + part
        us.append(u)
    return x1 + g2 * acc, us


def _post_prompt_kernel(*refs):
    merge_refs = refs[:11]
    g2_ref, wu_ref, cw_ref, cb_ref, wd_ref, y_ref, tail_ref, carry = refs[11:]
    ti = pl.program_id(1)
    tm = y_ref.shape[0]

    @pl.when(ti == 0)
    def _():
        carry[...] = jnp.zeros_like(carry)

    row = lax.broadcasted_iota(jnp.int32, (tm, 1), 0)

    def prev_rows(u, c0, c1):
        last = carry[:, c0:c1]
        m1 = jnp.where(row == 0, last[7:8], pltpu.roll(u, 1, 0))
        m2 = jnp.where(row == 0, last[6:7], jnp.where(row == 1, last[7:8], pltpu.roll(u, 2, 0)))
        return m1, m2

    x1, h2 = _merge_core(*merge_refs)
    out, us = _ffn_core(h2, x1, g2_ref[...], wu_ref, cw_ref, cb_ref, wd_ref, prev_rows)
    y_ref[...] = out
    ch = us[0].shape[1]
    for ci, u in enumerate(us):
        carry[:, ci * ch:(ci + 1) * ch] = u[tm - 8:, :]
    tail_ref[...] = carry[...]


def _post_prompt(o_nsa, o_hg, sm, x2d, mods, norm_g, w_br_nsa, w_br_hg, w_out, w_up, conv_w8, conv_b, w_down, *, b, tm):
    r, d = x2d.shape
    d_ff = w_down.shape[0]
    tpb = r // b // tm
    g1, sh2, sc2, g2 = mods

    def row_spec(width):
        return pl.BlockSpec((tm, width), lambda bi, i: (bi * tpb + i, 0))

    mod_spec = pl.BlockSpec((None, 1, d), lambda bi, i: (bi, 0, 0))
    return pl.pallas_call(
        _post_prompt_kernel,
        grid=(b, tpb),
        in_specs=[row_spec(o_nsa.shape[1]), row_spec(o_hg.shape[1]), row_spec(sm.shape[1]), row_spec(d),
                  mod_spec, mod_spec, mod_spec, _const_spec((1, d)),
                  _resident_spec(w_br_nsa.shape), _resident_spec(w_br_hg.shape), _resident_spec(w_out.shape),
                  mod_spec, _resident_spec(w_up.shape), _const_spec(conv_w8.shape), _const_spec(conv_b.shape),
                  _resident_spec(w_down.shape)],
        out_specs=[row_spec(d), pl.BlockSpec((None, 8, d_ff), lambda bi, i: (bi, 0, 0))],
        out_shape=[jax.ShapeDtypeStruct((r, d), F32), jax.ShapeDtypeStruct((b, 8, d_ff), F32)],
        scratch_shapes=[pltpu.VMEM((8, d_ff), F32)],
        compiler_params=_cparams(("arbitrary", "arbitrary")),
        name="post_prompt",
    )(o_nsa, o_hg, sm, x2d, g1, sh2, sc2, norm_g, w_br_nsa, w_br_hg, w_out, g2, w_up, conv_w8, conv_b, w_down)


def _ffn_dec_kernel(h2_ref, x1_ref, g2_ref, wu_ref, cw_ref, cb_ref, wd_ref, m1_ref, m2_ref, y_ref, u_ref):
    def prev_rows(u, c0, c1):
        return m1_ref[:, c0:c1], m2_ref[:, c0:c1]

    out, us = _ffn_core(h2_ref[...], x1_ref[...], g2_ref[...], wu_ref, cw_ref, cb_ref, wd_ref, prev_rows)
    y_ref[...] = out
    u_ref[...] = jnp.concatenate(us, axis=1)


def _ffn_dec(h2, x1, g2, w_up, conv_w8, conv_b, w_down, u_m1, u_m2):
    r, d = x1.shape
    d_ff = w_down.shape[0]
    args = (h2, x1, g2, w_up, conv_w8, conv_b, w_down, u_m1, u_m2)
    return pl.pallas_call(
        _ffn_dec_kernel,
        grid=(1,),
        in_specs=[_const_spec(a.shape) for a in args],
        out_specs=[_const_spec((r, d)), _const_spec((r, d_ff))],
        out_shape=[jax.ShapeDtypeStruct((r, d), F32), jax.ShapeDtypeStruct((r, d_ff), F32)],
        compiler_params=_cparams(("arbitrary",)),
        name="ffn_dec",
    )(*args)


def _pack_w_in(w_in):
    cuts = np.cumsum([NSA_WIDTH] + [KV_WIDTH] * 6 + [3 * NSA_HEADS] + [HG_WIDTH] * 4)
    q, kv, gates, rest = w_in[:, :cuts[0]], w_in[:, cuts[0]:cuts[6]], w_in[:, cuts[6]:cuts[7]], w_in[:, cuts[7]:]
    hz, m = rest[:, :4 * HG_WIDTH], rest[:, 4 * HG_WIDTH:]
    per_group = 3 * HPG
    pad = jnp.zeros((w_in.shape[0], LANES - per_group), w_in.dtype)
    gate_cols = []
    for g in range(NSA_GROUPS):
        gate_cols += [gates[:, g * per_group:(g + 1) * per_group], pad]
    return jnp.concatenate([q, kv] + gate_cols + [hz, m], axis=1).astype(BF16)


def _rope_tables(pos):
    half = ROT_DIM // 2
    inv = ROPE_THETA ** (-jnp.arange(0, ROT_DIM, 2, dtype=F32) / ROT_DIM)
    ang = pos.astype(F32)[:, None] * inv[None, :]
    cos, sin = jnp.cos(ang), jnp.sin(ang)
    n = pos.shape[0]
    ones = jnp.ones((n, HEAD_DIM - ROT_DIM), F32)
    zeros = jnp.zeros((n, HEAD_DIM - ROT_DIM), F32)
    z8 = jnp.zeros((n, half), F32)
    c = jnp.concatenate([cos, cos, ones], axis=1)
    s_lo = jnp.concatenate([-sin, z8, zeros], axis=1)
    s_hi = jnp.concatenate([z8, sin, zeros], axis=1)
    return tuple(jnp.concatenate([t, t], axis=1) for t in (c, s_lo, s_hi))


def _cmp_weights(cmp_w1, cmp_w2, cmp_pe):
    n_half = CMP_BLK // CMP_STRIDE
    eye = jnp.eye(NSA_GROUPS, dtype=F32)
    w1s, w2s, pes = [], [], []
    for j in range(2):
        w = cmp_w1[j].reshape(n_half, CMP_STRIDE, HEAD_DIM, HEAD_DIM)
        big = jnp.einsum('rsdh,ge->sgdreh', w, eye)
        w1s.append(big.reshape(CMP_STRIDE * KV_WIDTH, n_half * KV_WIDTH))
        w2s.append(jnp.einsum('dh,ge->gdeh', cmp_w2[j], eye).reshape(KV_WIDTH, KV_WIDTH))
        pe = cmp_pe[j].reshape(n_half, CMP_STRIDE, 1, HEAD_DIM)
        pe = jnp.broadcast_to(pe, (n_half, CMP_STRIDE, NSA_GROUPS, HEAD_DIM)).reshape(n_half, -1)
        pes.append(jnp.concatenate([pe, jnp.zeros((8 - n_half, pe.shape[1]), F32)], axis=0))
    return jnp.stack(w1s).astype(BF16), jnp.stack(w2s).astype(BF16), jnp.stack(pes).astype(BF16)


def _overlap(nc_pad, nc, ns, ns_pad):
    start = np.arange(nc_pad) * CMP_STRIDE
    end = start + CMP_BLK - 1
    s0 = np.arange(ns_pad) * SLC_BLK
    s1 = s0 + SLC_BLK - 1
    m = (start[:, None] <= s1[None, :]) & (end[:, None] >= s0[None, :])
    m &= (np.arange(nc_pad) < nc)[:, None] & (np.arange(ns_pad) < ns)[None, :]
    return jnp.asarray(m, BF16)


def kernel(x_prompt, x_sample, cache_cmp_k, cache_cmp_v, cache_slc_k, cache_slc_v, cache_win_k, cache_win_v,
           state_hgrn, state_ffn_conv, page_table, c_prompt, c_sample, ada_w, ada_b, norm_attn, norm_ffn, w_in,
           q_gain, k_gain, cmp_w1, cmp_w2, cmp_pe, hg_lb, hg_norm, w_br_nsa, w_br_hg, w_out, w_up, conv_w,
           conv_b, w_down):
    depth = ada_w.shape[0]
    assert depth == 1, "single-layer trunk only"
    b, t, d = x_prompt.shape
    db, tn, _ = x_sample.shape
    assert tn == 1
    n_pages = page_table.shape[1]
    past = n_pages * PAGE_SIZE
    d_ff = w_down.shape[1]
    tq = 256
    tk = 256
    tm = 256
    lc = 128
    assert t >= WINDOW and n_pages % PAGES_PER_STEP == 0
    assert cache_win_k.shape[2] == WINDOW

    w_pack = _pack_w_in(w_in[0])
    qg = jnp.tile(q_gain[0], NSA_HEADS).reshape(1, NSA_WIDTH)
    kg = jnp.concatenate([jnp.tile(k_gain[0], (1, NSA_GROUPS)), jnp.zeros((5, KV_WIDTH), F32)], axis=0)
    hd = np.arange(NSA_WIDTH) // HEAD_DIM
    bd = jnp.asarray(hd[:, None] == hd[None, :], BF16)
    w1c, w2c, pe_x = _cmp_weights(cmp_w1[0], cmp_w2[0], cmp_pe[0])
    conv_w8 = jnp.concatenate([conv_w[0], jnp.zeros((8 - CONV_W, d_ff), F32)], axis=0)
    conv_b2 = conv_b[0].reshape(1, d_ff)
    wbn, wbh, wo = w_br_nsa[0].astype(BF16), w_br_hg[0].astype(BF16), w_out[0].astype(BF16)
    wu, wd = w_up[0].astype(BF16), w_down[0].astype(BF16)
    norm_a, norm_f, hg_n = norm_attn[0].reshape(1, d), norm_ffn[0].reshape(1, d), hg_norm[0].reshape(1, HG_DIM)

    n_c = b + db
    c_all = jnp.concatenate([c_prompt, c_sample, jnp.zeros((-n_c % 8, d), F32)], axis=0)
    ada = _ada(c_all, ada_w[0], ada_b[0])
    mods_p = [ada[:b, i * d:(i + 1) * d].reshape(b, 1, d) for i in range(6)]
    mods_s = [ada[b:n_c, i * d:(i + 1) * d].reshape(1, db, d) for i in range(6)]

    tpb = t // tm
    (q, kc_r, vc_r, ks_r, vs_r, kw_r, vw_r, kcb, vcb, kvb, gt, hz, sm) = _inproj(
        x_prompt.reshape(b * t, d), mods_p[0], mods_p[1], norm_a, w_pack, qg, kg, _rope_tables(jnp.arange(t)), bd,
        tm=tm, tiles_per_mod=tpb, rope_tiles=tpb)

    nsub = t // CMP_STRIDE
    pk = _mm(kcb.reshape(b * nsub, CMP_STRIDE * KV_WIDTH), w1c[0], 512, "cmp_proj_k").reshape(b, nsub, -1)
    pv = _mm(vcb.reshape(b * nsub, CMP_STRIDE * KV_WIDTH), w1c[1], 512, "cmp_proj_v").reshape(b, nsub, -1)
    kcv = _cmp_combine(pk, pv, pe_x, w1c, w2c)
    ov = _overlap(nsub, nsub - 1, t // SLC_BLK, LANES).T
    o_nsa = _nsa_prompt(q.reshape(b, t, -1), kcv, kvb.reshape(b, t, -1), gt.reshape(b, t, -1), ov, tq=tq, tk=tk)

    o_hg, st_p = _hgrn_prompt(hz.reshape(b, t, -1), hg_lb, hg_n, lc=lc)

    y_p, tail = _post_prompt(o_nsa.reshape(b * t, -1), o_hg.reshape(b * t, -1), sm, x_prompt.reshape(b * t, d),
                             mods_p[2:], norm_f, wbn, wbh, wo, wu, conv_w8, conv_b2, wd, b=b, tm=tm)

    def rows5(a):
        return a.reshape(1, b, t, NSA_GROUPS, HEAD_DIM)

    w = min(WINDOW, t)
    out_p = (rows5(kc_r), rows5(vc_r), rows5(ks_r), rows5(vs_r), rows5(kw_r)[:, :, -w:], rows5(vw_r)[:, :, -w:],
             st_p[None], tail[None, :, 8 - (CONV_W - 1):, :])

    pos_s = jnp.full((db,), past, jnp.int32)
    (q_s, kc_s, vc_s, ks_s, vs_s, kw_s, vw_s, _, _, _, gt_s, hz_s, sm_s) = _inproj(
        x_sample.reshape(db, d), mods_s[0], mods_s[1], norm_a, w_pack, qg, kg, _rope_tables(pos_s), bd,
        tm=db, tiles_per_mod=1, rope_tiles=1)

    def feature_major(a):
        return jnp.transpose(a, (0, 2, 3, 1))

    assert (past + tn) // CMP_STRIDE == past // CMP_STRIDE
    pk_s, pv_s = _cmp_proj_paged(feature_major(cache_cmp_k[0]).reshape(-1, KV_WIDTH, PAGE_SIZE),
                                 feature_major(cache_cmp_v[0]).reshape(-1, KV_WIDTH, PAGE_SIZE), page_table, w1c)
    kcv_s = _cmp_combine(pk_s, pv_s, pe_x, w1c, w2c)
    nsub_s = past // CMP_STRIDE
    ns_s = -(-(past + tn) // SLC_BLK)
    ns_pad = -(-ns_s // LANES) * LANES
    ov_s = _overlap(nsub_s, nsub_s - 1, ns_s, ns_pad)
    o_cmp_s, idx8 = _nsa_dec_select(q_s.reshape(db, NSA_HEADS, HEAD_DIM), kcv_s, ov_s, past)
    idx = idx8[:, :NSA_GROUPS, :N_SEL].reshape(db, NSA_GROUPS * N_SEL)
    new_cols = [a.reshape(db, NSA_GROUPS, HEAD_DIM, 1) for a in (ks_s, vs_s, kw_s, vw_s)]
    per_group = 3 * HPG
    gt4 = jnp.stack([gt_s[:, g * LANES:g * LANES + per_group] for g in range(NSA_GROUPS)], axis=1)
    o_nsa_s, wk_new, wv_new = _nsa_dec_attend(
        idx, page_table, feature_major(cache_slc_k[0]), feature_major(cache_slc_v[0]),
        q_s.astype(F32).reshape(db, NSA_GROUPS, HPG, HEAD_DIM), new_cols,
        feature_major(cache_win_k[0]), feature_major(cache_win_v[0]),
        o_cmp_s.reshape(db, NSA_GROUPS, HPG, HEAD_DIM), gt4.reshape(db, NSA_GROUPS, HPG, 3), past)
    wk_new, wv_new = (jnp.transpose(a, (0, 3, 1, 2)) for a in (wk_new, wv_new))

    o_hg_s, st_s = _hgrn_dec(hz_s, state_hgrn[0], hg_lb, hg_n)
    x1_s, h2_s = _merge(o_nsa_s.reshape(db, NSA_WIDTH), o_hg_s.reshape(db, HG_WIDTH), sm_s, x_sample.reshape(db, d),
                        mods_s[2], mods_s[3], mods_s[4], norm_f, wbn, wbh, wo, tm=db, tiles_per_mod=1)
    buf = state_ffn_conv[0]
    y_s, u_s = _ffn_dec(h2_s, x1_s, mods_s[5].reshape(db, d), wu, conv_w8, conv_b2, wd, buf[:, 1], buf[:, 0])
    conv_s = jnp.stack([buf[:, 1], u_s], axis=1)

    def rows5s(a):
        return a.reshape(1, db, tn, NSA_GROUPS, HEAD_DIM)

    out_s = (rows5s(kc_s), rows5s(vc_s), rows5s(ks_s), rows5s(vs_s),
             wk_new.reshape(1, db, WINDOW, NSA_GROUPS, HEAD_DIM), wv_new.reshape(1, db, WINDOW, NSA_GROUPS, HEAD_DIM),
             st_s[None], conv_s[None])
    return (y_p.reshape(b, t, d), y_s.reshape(db, tn, d)) + out_p + out_s
```

```python
import functools

import numpy as np
import jax
import jax.numpy as jnp
from jax import lax
from jax.experimental import pallas as pl
from jax.experimental.pallas import tpu as pltpu

F32 = jnp.float32
BF16 = jnp.bfloat16

HEAD_DIM = 64
NSA_HEADS = 8
NSA_GROUPS = 2
HPG = NSA_HEADS // NSA_GROUPS
ROT_DIM = HEAD_DIM // 4
ROPE_THETA = 500000.0
CMP_BLK = 32
CMP_STRIDE = 16
SLC_BLK = 64
N_SEL = 16
WINDOW = 512
PAGE_SIZE = 128
HG_HEADS = 4
HG_DIM = 128
HG_SUB_LOG2 = 4
HG_SUB = 1 << HG_SUB_LOG2
HG_SAFE_DECAY = 60.0
SUBLANES = 8
SOFTMAX_VREGS = 32
CONV_W = 3
EPS = 1e-6
NEG = -1e30
SEL_BIAS = -1e9
LOG2E = 1.4426950408889634
LANES = 128
VMEM_LIMIT = 56 * 1024 * 1024

NSA_WIDTH = NSA_HEADS * HEAD_DIM
KV_WIDTH = NSA_GROUPS * HEAD_DIM
HG_WIDTH = HG_HEADS * HG_DIM
GRP_Q = HPG * HEAD_DIM

SEG_Q = 0
SEG_KV = SEG_Q + NSA_WIDTH
SEG_G = SEG_KV + 6 * KV_WIDTH
SEG_H = SEG_G + NSA_GROUPS * LANES
SEG_M = SEG_H + 4 * HG_WIDTH


def _cparams(sem):
    return pltpu.CompilerParams(dimension_semantics=sem, vmem_limit_bytes=VMEM_LIMIT)


def _dot(a, b):
    return jnp.dot(a.astype(BF16), b.astype(BF16), preferred_element_type=F32)


def _dot_nt(a, b):
    return lax.dot_general(a.astype(BF16), b.astype(BF16), (((1,), (1,)), ((), ())), preferred_element_type=F32)


def _split2(x):
    hi = x.astype(BF16)
    lo = (x - hi.astype(F32)).astype(BF16)
    return hi, lo


def _split3(x):
    hi = x.astype(BF16)
    r = x - hi.astype(F32)
    mid = r.astype(BF16)
    lo = (r - mid.astype(F32)).astype(BF16)
    return hi, mid, lo


def _dot_x3(a, b):
    ah, al = _split2(a)
    bh, bl = _split2(b)
    return (jnp.dot(ah, bh, preferred_element_type=F32) + jnp.dot(al, bh, preferred_element_type=F32)
            + jnp.dot(ah, bl, preferred_element_type=F32))


def _sigmoid(x):
    return 1.0 / (1.0 + jnp.exp(-x))


def _const_spec(shape):
    nd = len(shape)
    return pl.BlockSpec(shape, lambda *_: (0,) * nd)


def _resident_spec(shape):
    nd = len(shape)
    return pl.BlockSpec(shape, lambda *_: (0,) * nd, pipeline_mode=pl.Buffered(1))


def _ada_kernel(c_ref, w_ref, b_ref, o_ref):
    o_ref[...] = _dot_x3(c_ref[...], w_ref[...]) + b_ref[...]


def _ada(c_all, w, b):
    r, d = c_all.shape
    n = w.shape[1]
    tn = 1536
    return pl.pallas_call(
        _ada_kernel,
        grid=(n // tn,),
        in_specs=[_const_spec((r, d)), pl.BlockSpec((d, tn), lambda j: (0, j)), pl.BlockSpec((1, tn), lambda j: (0, j))],
        out_specs=pl.BlockSpec((r, tn), lambda j: (0, j)),
        out_shape=jax.ShapeDtypeStruct((r, n), F32),
        compiler_params=_cparams(("arbitrary",)),
        name="ada",
    )(c_all, w, b.reshape(1, n))


def _rope(x, c, s_lo, s_hi):
    width = x.shape[1]
    reps = width // LANES
    if reps > 1:
        c = jnp.concatenate([c] * reps, axis=1)
        s_lo = jnp.concatenate([s_lo] * reps, axis=1)
        s_hi = jnp.concatenate([s_hi] * reps, axis=1)
    half = ROT_DIM // 2
    return x * c + pltpu.roll(x, half, 1) * s_hi + pltpu.roll(x, width - half, 1) * s_lo


def _head_rms(z, bd):
    hi, lo = _split2(z * z)
    ss = jnp.dot(hi, bd, preferred_element_type=F32) + jnp.dot(lo, bd, preferred_element_type=F32)
    return z * lax.rsqrt(ss * (1.0 / HEAD_DIM) + EPS)


def _inproj_kernel(x_ref, sh_ref, sc_ref, ng_ref, w_ref, qg_ref, kg_ref, rc_ref, rlo_ref, rhi_ref, bd_ref,
                   q_ref, kc_ref, vc_ref, ks_ref, vs_ref, kw_ref, vw_ref, kcb_ref, vcb_ref, kvb_ref, gt_ref, hz_ref,
                   sm_ref, kc_scr, vc_scr):
    x = x_ref[...]
    xn = x * lax.rsqrt(jnp.mean(x * x, axis=-1, keepdims=True) + EPS) * ng_ref[...]
    h = (xn * (1.0 + sc_ref[...]) + sh_ref[...]).astype(BF16)
    rc, rlo, rhi = rc_ref[...], rlo_ref[...], rhi_ref[...]
    bd = bd_ref[...]

    zq = jnp.dot(h, w_ref[:, SEG_Q:SEG_KV], preferred_element_type=F32)
    q = _rope(_head_rms(zq, bd) * qg_ref[...], rc, rlo, rhi)
    q_ref[...] = (q * (HEAD_DIM ** -0.5 * LOG2E)).astype(BF16)

    zkv = jnp.dot(h, w_ref[:, SEG_KV:SEG_G], preferred_element_type=F32)
    bd1 = bd[:LANES, :LANES]
    k_refs = (kc_ref, ks_ref, kw_ref)
    v_refs = (vc_ref, vs_ref, vw_ref)
    rows = []
    for j in range(3):
        zk = zkv[:, 2 * j * KV_WIDTH:(2 * j + 1) * KV_WIDTH]
        k = _rope(_head_rms(zk, bd1) * kg_ref[j:j + 1, :], rc, rlo, rhi)
        v = zkv[:, (2 * j + 1) * KV_WIDTH:(2 * j + 2) * KV_WIDTH]
        for g in range(NSA_GROUPS):
            k_refs[j][:, g, :] = k[:, g * HEAD_DIM:(g + 1) * HEAD_DIM]
            v_refs[j][:, g, :] = v[:, g * HEAD_DIM:(g + 1) * HEAD_DIM]
        rows.append((k, v))
    for src, scr, dst in ((rows[0][0], kc_scr, kcb_ref), (rows[0][1], vc_scr, vcb_ref)):
        scr[...] = src
        for s in range(CMP_STRIDE):
            dst[:, s * KV_WIDTH:(s + 1) * KV_WIDTH] = scr[pl.ds(s, dst.shape[0], stride=CMP_STRIDE), :].astype(BF16)
    parts = []
    for g in range(NSA_GROUPS):
        for k, v in rows[1:]:
            parts.append(k[:, g * HEAD_DIM:(g + 1) * HEAD_DIM])
            parts.append(v[:, g * HEAD_DIM:(g + 1) * HEAD_DIM])
    kvb_ref[...] = jnp.concatenate(parts, axis=1).astype(BF16)

    gt_ref[...] = _sigmoid(jnp.dot(h, w_ref[:, SEG_G:SEG_H], preferred_element_type=F32))
    hz_ref[...] = jnp.dot(h, w_ref[:, SEG_H:SEG_M], preferred_element_type=F32)
    sm_ref[...] = _sigmoid(jnp.dot(h, w_ref[:, SEG_M:], preferred_element_type=F32)).astype(BF16)


def _inproj(x2d, shift, scale, norm_g, w_pack, qg, kg, rope_tabs, bd, *, tm, tiles_per_mod, rope_tiles):
    r, d = x2d.shape
    n_tiles = r // tm
    rows_mod = shift.shape[1]
    mod_spec = pl.BlockSpec((None, rows_mod, d), lambda i: (i // tiles_per_mod, 0, 0))
    rope_spec = pl.BlockSpec((tm, LANES), lambda i: (i % rope_tiles, 0))

    def row_spec(width):
        return pl.BlockSpec((tm, width), lambda i: (i, 0))

    widths = (4 * KV_WIDTH, NSA_GROUPS * LANES, 4 * HG_WIDTH, w_pack.shape[1] - SEG_M)
    dtypes = (BF16, F32, F32, BF16)
    cache_spec = pl.BlockSpec((tm, NSA_GROUPS, HEAD_DIM), lambda i: (i, 0, 0))
    cache_shape = jax.ShapeDtypeStruct((r, NSA_GROUPS, HEAD_DIM), F32)
    assert tm % CMP_STRIDE == 0
    sub_spec = pl.BlockSpec((tm // CMP_STRIDE, CMP_STRIDE * KV_WIDTH), lambda i: (i, 0))
    sub_shape = jax.ShapeDtypeStruct((r // CMP_STRIDE, CMP_STRIDE * KV_WIDTH), BF16)
    return pl.pallas_call(
        _inproj_kernel,
        grid=(n_tiles,),
        in_specs=[row_spec(d), mod_spec, mod_spec, _const_spec((1, d)), _resident_spec(w_pack.shape),
                  _const_spec(qg.shape), _const_spec(kg.shape), rope_spec, rope_spec, rope_spec, _const_spec(bd.shape)],
        out_specs=[row_spec(NSA_WIDTH)] + [cache_spec] * 6 + [sub_spec] * 2 + [row_spec(w) for w in widths],
        out_shape=[jax.ShapeDtypeStruct((r, NSA_WIDTH), BF16)] + [cache_shape] * 6 + [sub_shape] * 2
        + [jax.ShapeDtypeStruct((r, w), dt) for w, dt in zip(widths, dtypes)],
        scratch_shapes=[pltpu.VMEM((tm, KV_WIDTH), F32)] * 2,
        compiler_params=_cparams(("arbitrary",)),
        name="inproj",
    )(x2d, shift, scale, norm_g, w_pack, qg, kg, *rope_tabs, bd)


def _mm_kernel(a_ref, w_ref, o_ref):
    o_ref[...] = _dot(a_ref[...], w_ref[...])


def _mm(a, w, tm, name):
    m, k = a.shape
    n = w.shape[1]
    tm = min(tm, m)
    return pl.pallas_call(
        _mm_kernel,
        grid=(m // tm,),
        in_specs=[pl.BlockSpec((tm, k), lambda i: (i, 0)), _const_spec((k, n))],
        out_specs=pl.BlockSpec((tm, n), lambda i: (i, 0)),
        out_shape=jax.ShapeDtypeStruct((m, n), F32),
        compiler_params=_cparams(("arbitrary",)),
        name=name,
    )(a, w)


PAGES_PER_STEP = 16


def _cmp_proj_paged_kernel(pt_ref, *refs):
    k_pages, v_pages = refs[:PAGES_PER_STEP], refs[PAGES_PER_STEP:2 * PAGES_PER_STEP]
    perm_ref, w_ref, ok_ref, ov_ref, xk_ref, xv_ref = refs[2 * PAGES_PER_STEP:]
    sub = PAGE_SIZE // CMP_STRIDE
    perm = perm_ref[...]
    half = PAGES_PER_STEP // 2
    for kv, (pages, x_ref, o_ref) in enumerate(((k_pages, xk_ref, ok_ref), (v_pages, xv_ref, ov_ref))):
        for j0 in (0, half):
            stacked = jnp.concatenate([pr[...].astype(BF16) for pr in pages[j0:j0 + half]], axis=0)
            rows = _dot_nt(perm, stacked)
            for s in range(CMP_STRIDE):
                for j in range(half):
                    x_ref[(j0 + j) * sub:(j0 + j + 1) * sub, s * KV_WIDTH:(s + 1) * KV_WIDTH] = (
                        rows[s * sub:(s + 1) * sub, j * KV_WIDTH:(j + 1) * KV_WIDTH])
        o_ref[...] = _dot(x_ref[...], w_ref[kv])


def _cmp_proj_paged(pool_k, pool_v, page_table, w):
    db, n_pages = page_table.shape
    sub = PAGE_SIZE // CMP_STRIDE
    n = w.shape[2]
    steps = n_pages // PAGES_PER_STEP
    pos = np.arange(PAGE_SIZE)
    perm = jnp.asarray((pos[:, None] % sub) * CMP_STRIDE + pos[:, None] // sub == pos[None, :], BF16)
    page_specs = [pl.BlockSpec((None, KV_WIDTH, PAGE_SIZE), lambda b, c, pt, j=j: (pt[b, c * PAGES_PER_STEP + j], 0, 0))
                  for j in range(PAGES_PER_STEP)]
    out_spec = pl.BlockSpec((None, PAGES_PER_STEP * sub, n), lambda b, c, pt: (b, c, 0))
    rows_scratch = pltpu.VMEM((PAGES_PER_STEP * sub, CMP_STRIDE * KV_WIDTH), F32)
    grid_spec = pltpu.PrefetchScalarGridSpec(
        num_scalar_prefetch=1,
        grid=(db, steps),
        in_specs=page_specs + page_specs + [pl.BlockSpec(perm.shape, lambda b, c, pt: (0, 0)),
                                            pl.BlockSpec(w.shape, lambda b, c, pt: (0, 0, 0))],
        out_specs=[out_spec, out_spec],
        scratch_shapes=[rows_scratch, rows_scratch],
    )
    out = jax.ShapeDtypeStruct((db, n_pages * sub, n), F32)
    return pl.pallas_call(
        _cmp_proj_paged_kernel,
        grid_spec=grid_spec,
        out_shape=[out, out],
        compiler_params=_cparams(("arbitrary", "arbitrary")),
        name="cmp_proj_paged",
    )(page_table, *([pool_k] * PAGES_PER_STEP), *([pool_v] * PAGES_PER_STEP), perm, w)


def _cmp_combine_kernel(pk_ref, pv_ref, pe_ref, w1_ref, w2_ref, o_ref):
    n = pk_ref.shape[0]
    outs = []
    for j, p_ref in enumerate((pk_ref, pv_ref)):
        pe = jnp.dot(pe_ref[j], w1_ref[j], preferred_element_type=F32)
        pe_hid = pe[0:1, :KV_WIDTH] + pe[1:2, KV_WIDTH:]
        p = p_ref[...]
        nxt = pltpu.roll(p[:, KV_WIDTH:], n - 1, 0)
        hid = pe_hid + p[:, :KV_WIDTH] + nxt
        outs.append(_dot(jax.nn.gelu(hid), w2_ref[j]))
    for g in range(NSA_GROUPS):
        o_ref[g] = jnp.concatenate([o[:, g * HEAD_DIM:(g + 1) * HEAD_DIM] for o in outs], axis=1).astype(BF16)


def _cmp_combine(pk, pv, pe_x, w1, w2bd):
    bx, n, c = pk.shape
    return pl.pallas_call(
        _cmp_combine_kernel,
        grid=(bx,),
        in_specs=[pl.BlockSpec((None, n, c), lambda b: (b, 0, 0)), pl.BlockSpec((None, n, c), lambda b: (b, 0, 0)),
                  _const_spec(pe_x.shape), _const_spec(w1.shape), _const_spec(w2bd.shape)],
        out_specs=pl.BlockSpec((None, NSA_GROUPS, n, 2 * HEAD_DIM), lambda b: (b, 0, 0, 0)),
        out_shape=jax.ShapeDtypeStruct((bx, NSA_GROUPS, n, 2 * HEAD_DIM), BF16),
        compiler_params=_cparams(("arbitrary",)),
        name="cmp_combine",
    )(pk, pv, pe_x, w1, w2bd)


def _masked_softmax(s, mask):
    s = jnp.where(mask, s, NEG)
    m = jnp.max(s, axis=-1, keepdims=True)
    e = jnp.where(mask, jnp.exp2(s - m), 0.0)
    return e / jnp.maximum(jnp.sum(e, axis=-1, keepdims=True), 1e-30)


def _top_k_mask(score, k, axis, idx_out=False):
    n = score.shape[axis]
    pos = lax.broadcasted_iota(jnp.int32, score.shape, axis).astype(F32)
    idx = jnp.zeros((score.shape[0], LANES), jnp.int32)
    lane_out = lax.broadcasted_iota(jnp.int32, (score.shape[0], LANES), 1)
    work = score
    for it in range(k):
        mx = jnp.max(work, axis=axis, keepdims=True)
        first = jnp.min(jnp.where(work == mx, pos, float(n)), axis=axis, keepdims=True)
        work = jnp.where(pos == first, -jnp.inf, work)
        if idx_out:
            idx = jnp.where(lane_out == it, first.astype(jnp.int32), idx)
    sel = work == -jnp.inf
    return (sel, idx) if idx_out else sel


def _block_scores(imp, qpos, axis):
    blk = lax.broadcasted_iota(jnp.int32, imp.shape, axis)
    cur = qpos >> 6
    valid = (blk << 6) <= qpos
    forced = (blk == 0) | (blk == cur) | (blk == cur - 1)
    return jnp.where(forced, 1e9, jnp.where(valid, imp, -1e9))


def _nsa_prompt_kernel(q_ref, kcv_ref, kvb_ref, gt_ref, ov_ref, o_ref, *scr, tq, tk):
    qi = pl.program_id(2)
    q0 = qi * tq
    nc = kcv_ref.shape[0]
    rows = HPG * tq

    q = q_ref[...]
    qs = jnp.concatenate([q[:, h * HEAD_DIM:(h + 1) * HEAD_DIM] for h in range(HPG)], axis=0)
    qpos = q0 + lax.broadcasted_iota(jnp.int32, (tq, 1), 0)
    qpos4 = jnp.concatenate([qpos] * HPG, axis=0)

    lane_blk = lax.broadcasted_iota(jnp.int32, (1, LANES), 1)
    kcol = lax.broadcasted_iota(jnp.int32, (tk, 1), 0)
    krow = lax.broadcasted_iota(jnp.int32, (1, tk), 1)
    ones_v = jnp.ones((tk, LANES - HEAD_DIM), BF16)
    rb = SOFTMAX_VREGS * SUBLANES * LANES // tk
    qpos_rb = [q0 + (r % (tq // rb)) * rb + lax.broadcasted_iota(jnp.int32, (rb, 1), 0) for r in range(rows // rb)]

    half = rows // 2

    def key_tile(j):
        return kvb_ref[pl.ds(pl.multiple_of(j * tk, tk), tk), :]

    def branch(s0, s1, p0, p1, a0, a1, m_scr, acc_scr, slc):
        s_bufs, p_bufs, a_bufs = (s0, s1), (p0, p1), (a0, a1)
        v_lane0 = HEAD_DIM if slc else 3 * HEAD_DIM

        def scores(slot, j):
            blk = key_tile(j)
            if slc:
                onehot = jnp.where(((j * tk + kcol) >> 6) == lane_blk, 1.0, 0.0).astype(BF16)
                q_lhs, k_rhs = q_aug, jnp.concatenate([onehot, blk[:, 0:HEAD_DIM]], axis=1)
            else:
                q_lhs, k_rhs = qs, blk[:, 2 * HEAD_DIM:3 * HEAD_DIM]
            for h0 in (0, half):
                s_bufs[slot][h0:h0 + half, :] = _dot_nt(q_lhs[h0:h0 + half], k_rhs)

        def softmax(slot, j, mask_kind, live=None):
            k0 = j * tk
            for r in range(rows // rb):
                sl = slice(r * rb, (r + 1) * rb)
                s = s_bufs[slot][sl, :]
                keep = None
                if mask_kind == "causal":
                    keep = (k0 + krow) <= qpos_rb[r]
                elif mask_kind == "window":
                    keep = qpos_rb[r] - (k0 + krow) < WINDOW
                if live is not None:
                    keep = live if keep is None else jnp.logical_and(keep, live)
                if keep is not None:
                    s = jnp.where(keep, s, NEG)
                m_old = m_scr[sl, :]
                m_new = jnp.maximum(m_old, jnp.max(s, axis=-1, keepdims=True))
                a_bufs[slot][sl, :] = jnp.exp2(m_old - m_new)
                m_scr[sl, :] = m_new
                p_bufs[slot][sl, :] = jnp.exp2(s - jnp.concatenate([m_new] * (tk // LANES), axis=1)).astype(BF16)

        def accumulate(slot, j):
            v_aug = jnp.concatenate([key_tile(j)[:, v_lane0:v_lane0 + HEAD_DIM], ones_v], axis=1)
            for h0 in (0, half):
                sl = slice(h0, h0 + half)
                acc_scr[sl, :] = a_bufs[slot][sl, :] * acc_scr[sl, :] + jnp.dot(p_bufs[slot][sl, :], v_aug,
                                                                               preferred_element_type=F32)

        def reset():
            m_scr[...] = jnp.full(m_scr.shape, NEG, F32)
            acc_scr[...] = jnp.zeros(acc_scr.shape, F32)

        def result():
            acc = acc_scr[...]
            return acc[:, :HEAD_DIM] / jnp.maximum(acc[:, HEAD_DIM:HEAD_DIM + 1], 1e-30)

        return scores, softmax, accumulate, reset, result

    jd = q0 // tk
    n_scr = len(scr) // 2

    scores, softmax, accumulate, reset, result = branch(*scr[n_scr:], slc=False)
    reset()
    n_back = -(-(WINDOW - 1) // tk)
    tiles = [(jd, "causal", None)] + [
        (jnp.maximum(jd - back, 0), "window" if (back + 1) * tk - 1 >= WINDOW else None, jd >= back)
        for back in range(1, n_back + 1)]
    scores(0, tiles[0][0])
    for i, (j, mask_kind, live) in enumerate(tiles):
        if i + 1 < len(tiles):
            scores((i + 1) % 2, tiles[i + 1][0])
        softmax(i % 2, j, mask_kind, live)
        accumulate(i % 2, j)
    o_win = result()

    kcv = kcv_ref[...]
    kc, vc = kcv[:, :HEAD_DIM], kcv[:, HEAD_DIM:]
    cmp_end = lax.broadcasted_iota(jnp.int32, (1, nc), 1) * CMP_STRIDE + (CMP_BLK - 1)
    p = _masked_softmax(_dot_nt(qs, kc), cmp_end <= qpos4)
    o_cmp = _dot(p, vc)
    psum = p[0:tq]
    for h in range(1, HPG):
        psum = psum + p[h * tq:(h + 1) * tq]
    p_hi, p_lo = _split2(psum)
    ov_t = ov_ref[...]
    imp_t = _dot_nt(ov_t, p_hi) + _dot_nt(ov_t, p_lo)

    qpos_row = q0 + lax.broadcasted_iota(jnp.int32, (1, tq), 1)
    sel_t = _top_k_mask(_block_scores(imp_t, qpos_row, 0), N_SEL, 0)
    bias = jnp.where(sel_t, 0.0, SEL_BIAS).T.astype(BF16)
    q_aug = jnp.concatenate([jnp.concatenate([bias] * HPG, axis=0), qs], axis=1)

    scores, softmax, accumulate, reset, result = branch(*scr[:n_scr], slc=True)
    reset()
    scores(0, jd)
    scores(1, 0)
    softmax(0, jd, "causal")

    def past_pair(u, carry):
        ta, tb = 2 * u, 2 * u + 1
        scores(0, jnp.minimum(tb, jd))
        softmax(1, ta, None)
        accumulate(0, jnp.where(u == 0, jd, ta - 1))
        scores(1, jnp.minimum(tb + 1, jd))
        softmax(0, tb, None, live=tb < jd)
        accumulate(1, ta)
        return carry

    n_pairs = (jd + 1) // 2
    lax.fori_loop(0, n_pairs, past_pair, 0)
    accumulate(0, jnp.where(n_pairs == 0, jd, jnp.minimum(2 * n_pairs - 1, jd)))
    o_slc = result()

    gt = gt_ref[...]
    outs = []
    for h in range(HPG):
        r0 = h * tq
        outs.append(gt[:, 3 * h:3 * h + 1] * o_cmp[r0:r0 + tq] + gt[:, 3 * h + 1:3 * h + 2] * o_slc[r0:r0 + tq]
                    + gt[:, 3 * h + 2:3 * h + 3] * o_win[r0:r0 + tq])
    o_ref[...] = jnp.concatenate(outs, axis=1).astype(BF16)


def _nsa_prompt(q, kcv, kvb, gt, ov, *, tq, tk):
    b, t, _ = q.shape
    nc = kcv.shape[2]
    assert tk % tq == 0 and t % tk == 0 and tk <= WINDOW
    return pl.pallas_call(
        functools.partial(_nsa_prompt_kernel, tq=tq, tk=tk),
        grid=(b, NSA_GROUPS, t // tq),
        in_specs=[pl.BlockSpec((None, tq, GRP_Q), lambda bi, g, i: (bi, i, g)),
                  pl.BlockSpec((None, None, nc, 2 * HEAD_DIM), lambda bi, g, i: (bi, g, 0, 0)),
                  pl.BlockSpec((None, t, 4 * HEAD_DIM), lambda bi, g, i: (bi, 0, g)),
                  pl.BlockSpec((None, tq, LANES), lambda bi, g, i: (bi, i, g)),
                  _const_spec(ov.shape)],
        out_specs=pl.BlockSpec((None, tq, GRP_Q), lambda bi, g, i: (bi, i, g)),
        out_shape=jax.ShapeDtypeStruct((b, t, NSA_WIDTH), BF16),
        scratch_shapes=([pltpu.VMEM((HPG * tq, tk), F32)] * 2 + [pltpu.VMEM((HPG * tq, tk), BF16)] * 2
                        + [pltpu.VMEM((HPG * tq, LANES), F32)] * 4) * 2,
        compiler_params=_cparams(("arbitrary", "arbitrary", "arbitrary")),
        name="nsa_prompt",
    )(q, kcv, kvb, gt, ov)


def _nsa_dec_select_kernel(q_ref, kcv_ref, ov_ref, ocmp_ref, idx_ref, *, qpos):
    nc = kcv_ref.shape[1]
    q = q_ref[...]
    cmp_end = lax.broadcasted_iota(jnp.int32, (1, nc), 1) * CMP_STRIDE + (CMP_BLK - 1)
    mask = cmp_end <= qpos
    ov = ov_ref[...]
    imps = []
    for g in range(NSA_GROUPS):
        kcv = kcv_ref[g]
        p = _masked_softmax(_dot_nt(q[g * HPG:(g + 1) * HPG], kcv[:, :HEAD_DIM]), mask)
        ocmp_ref[g * HPG:(g + 1) * HPG, :] = _dot(p, kcv[:, HEAD_DIM:])
        p_hi, p_lo = _split2(jnp.sum(p, axis=0, keepdims=True))
        imps.append(jnp.dot(p_hi, ov, preferred_element_type=F32) + jnp.dot(p_lo, ov, preferred_element_type=F32))
    imp = jnp.concatenate(imps + [jnp.zeros((8 - NSA_GROUPS, ov.shape[1]), F32)], axis=0)
    score = _block_scores(imp, jnp.full((8, 1), qpos, jnp.int32), 1)
    _, idx = _top_k_mask(score, N_SEL, 1, idx_out=True)
    idx_ref[...] = idx


def _nsa_dec_select(q3, kcv, ov, qpos):
    db, nh, _ = q3.shape
    nc = kcv.shape[2]
    return pl.pallas_call(
        functools.partial(_nsa_dec_select_kernel, qpos=qpos),
        grid=(db,),
        in_specs=[pl.BlockSpec((None, nh, HEAD_DIM), lambda b: (b, 0, 0)),
                  pl.BlockSpec((None, NSA_GROUPS, nc, 2 * HEAD_DIM), lambda b: (b, 0, 0, 0)),
                  _const_spec(ov.shape)],
        out_specs=[pl.BlockSpec((None, nh, HEAD_DIM), lambda b: (b, 0, 0)),
                   pl.BlockSpec((None, 8, LANES), lambda b: (b, 0, 0))],
        out_shape=[jax.ShapeDtypeStruct((db, nh, HEAD_DIM), F32), jax.ShapeDtypeStruct((db, 8, LANES), jnp.int32)],
        compiler_params=_cparams(("arbitrary",)),
        name="nsa_dec_select",
    )(q3, kcv, ov)


def _nsa_dec_attend_kernel(idx_ref, pt_ref, *refs, past):
    k_refs = refs[:N_SEL]
    v_refs = refs[N_SEL:2 * N_SEL]
    (q_ref, nk_ref, nv_ref, nwk_ref, nwv_ref, wk_ref, wv_ref, ocmp_ref, gt_ref, o_ref, wko_ref, wvo_ref) = refs[2 * N_SEL:]
    b = pl.program_id(0)
    g = pl.program_id(1)
    win = wk_ref.shape[1]
    q = q_ref[...]
    lane_blk = lax.broadcasted_iota(jnp.int32, (1, SLC_BLK), 1)

    s_parts, v_parts, pos_parts = [], [], []
    for i in range(N_SEL):
        blk = idx_ref[b, g * N_SEL + i]
        first_half = blk % (PAGE_SIZE // SLC_BLK) == 0
        kpos = blk * SLC_BLK + lane_blk
        from_new = kpos >= past
        k_t, v_t = k_refs[i][...], v_refs[i][...]
        k_t = jnp.where(from_new, nk_ref[...], jnp.where(first_half, k_t[:, :SLC_BLK], k_t[:, SLC_BLK:]))
        v_t = jnp.where(from_new, nv_ref[...], jnp.where(first_half, v_t[:, :SLC_BLK], v_t[:, SLC_BLK:]))
        s_parts.append(_dot(q, k_t))
        v_parts.append(v_t)
        pos_parts.append(kpos)
    p = _masked_softmax(jnp.concatenate(s_parts, axis=1), jnp.concatenate(pos_parts, axis=1) <= past)
    o_slc = _dot_nt(p, jnp.concatenate(v_parts, axis=1))

    lane = lax.broadcasted_iota(jnp.int32, (1, win), 1)
    wk_new = jnp.where(lane == win - 1, nwk_ref[...], pltpu.roll(wk_ref[...], win - 1, 1))
    wv_new = jnp.where(lane == win - 1, nwv_ref[...], pltpu.roll(wv_ref[...], win - 1, 1))
    wko_ref[...] = wk_new
    wvo_ref[...] = wv_new
    p = _masked_softmax(_dot(q, wk_new), (win - 1 - lane) < WINDOW)
    o_win = _dot_nt(p, wv_new)

    gt = gt_ref[...]
    o_ref[...] = gt[:, 0:1] * ocmp_ref[...] + gt[:, 1:2] * o_slc + gt[:, 2:3] * o_win


def _nsa_dec_attend(idx, page_table, pool_k, pool_v, q4, new_cols, win_k, win_v, o_cmp4, gt4, past):
    db, n_pages = page_table.shape
    win = win_k.shape[3]
    blocks_per_page = PAGE_SIZE // SLC_BLK
    assert blocks_per_page == 2

    def blk_map(i):
        def index(b, g, idx_s, pt_s):
            page = jnp.minimum(idx_s[b, g * N_SEL + i] // blocks_per_page, n_pages - 1)
            return pt_s[b, page], g, 0, 0
        return index

    sel_specs = [pl.BlockSpec((None, None, HEAD_DIM, PAGE_SIZE), blk_map(i)) for i in range(N_SEL)]

    def per_bg(shape):
        return pl.BlockSpec((None, None) + shape, lambda b, g, *_: (b, g, 0, 0))

    grid_spec = pltpu.PrefetchScalarGridSpec(
        num_scalar_prefetch=2,
        grid=(db, NSA_GROUPS),
        in_specs=sel_specs + sel_specs + [per_bg((HPG, HEAD_DIM))] + [per_bg((HEAD_DIM, 1))] * 4
        + [per_bg((HEAD_DIM, win))] * 2 + [per_bg((HPG, HEAD_DIM)), per_bg((HPG, 3))],
        out_specs=[per_bg((HPG, HEAD_DIM)), per_bg((HEAD_DIM, win)), per_bg((HEAD_DIM, win))],
    )
    return pl.pallas_call(
        functools.partial(_nsa_dec_attend_kernel, past=past),
        grid_spec=grid_spec,
        out_shape=[jax.ShapeDtypeStruct((db, NSA_GROUPS, HPG, HEAD_DIM), F32),
                   jax.ShapeDtypeStruct(win_k.shape, F32), jax.ShapeDtypeStruct(win_v.shape, F32)],
        compiler_params=_cparams(("arbitrary", "arbitrary")),
        name="nsa_dec_attend",
    )(idx, page_table, *([pool_k] * N_SEL), *([pool_v] * N_SEL), q4, *new_cols, win_k, win_v, o_cmp4, gt4)


def _forget_lower_bound(lb_raw):
    m = jnp.max(lb_raw, axis=0, keepdims=True)
    e = jnp.exp(lb_raw - m)
    return e[0:1] / jnp.sum(e, axis=0, keepdims=True)


def _hgrn_out(o, hg, norm_g):
    o = o * lax.rsqrt(jnp.mean(o * o, axis=-1, keepdims=True) + EPS) * norm_g
    return o * (hg * _sigmoid(hg))


def _hgrn_chunk_common(q, hf, v, lb, tri, st, b_s, k_s, v_s):
    heads = range(len(q))
    lc = q[0].shape[0]
    f = [lb[h] + (1.0 - lb[h]) * _sigmoid(hf[h]) for h in heads]
    k = [1.0 - f[h] for h in heads]
    parts = [p for h in heads for p in _split3(jnp.log(f[h]))]
    csum = jnp.dot(tri, jnp.concatenate(parts, axis=1), preferred_element_type=F32)
    bcum = [csum[:, 3 * h * HG_DIM:(3 * h + 1) * HG_DIM] + csum[:, (3 * h + 1) * HG_DIM:(3 * h + 2) * HG_DIM]
            + csum[:, (3 * h + 2) * HG_DIM:(3 * h + 3) * HG_DIM] for h in heads]
    for h in heads:
        b_s[h][...] = bcum[h]
        k_s[h][...] = k[h]
        v_s[h][...] = v[h]

    o_inter = [_dot_nt(q[h] * jnp.exp(bcum[h]), st[h]) for h in heads]
    span = None
    for i in range(lc // HG_SUB):
        r0 = i * HG_SUB
        for h in heads:
            d_i = bcum[h][r0:r0 + 1] - bcum[h][r0 + HG_SUB - 1:r0 + HG_SUB]
            span = d_i if span is None else jnp.maximum(span, d_i)
    st_new = [st[h] * jnp.exp(bcum[h][lc - 1:lc]) + _dot(v[h].T, k[h] * jnp.exp(bcum[h][lc - 1:lc] - bcum[h]))
              for h in heads]
    return o_inter, k, bcum, st_new, span


def _hgrn_intra_matmul(o_inter, q, k, bcum, v, o_s):
    heads = range(len(q))
    lc = q[0].shape[0]
    n_sub = lc // HG_SUB
    row_sub = lax.broadcasted_iota(jnp.int32, (lc, HG_DIM), 0) >> HG_SUB_LOG2
    causal = lax.broadcasted_iota(jnp.int32, (lc, lc), 1) <= lax.broadcasted_iota(jnp.int32, (lc, lc), 0)
    for h in heads:
        ref = [bcum[h][j * HG_SUB:j * HG_SUB + 1] for j in range(n_sub)]
        ref_rows = jnp.concatenate([jnp.broadcast_to(r, (HG_SUB, HG_DIM)) for r in ref], axis=0)
        q_t = q[h] * jnp.exp(bcum[h] - ref_rows)
        k_t = k[h] * jnp.exp(ref_rows - bcum[h])
        q_cols = [jnp.where(row_sub >= j, q_t * jnp.exp(jnp.minimum(ref_rows - ref[j], 0.0)), 0.0).astype(BF16)
                  for j in range(n_sub)]
        k_cols = [jnp.where(row_sub == j, k_t, 0.0).astype(BF16) for j in range(n_sub)]
        a = _dot_nt(jnp.concatenate(q_cols, axis=1), jnp.concatenate(k_cols, axis=1))
        o_s[h][...] = o_inter[h] + _dot(jnp.where(causal, a, 0.0), v[h])


def _hgrn_intra_direct(o_inter, q, k, bcum, v, b_s, k_s, v_s, o_s):
    lc = q.shape[0]
    t_idx = lax.broadcasted_iota(jnp.int32, (SUBLANES, 1), 0)
    for i in range(lc // HG_SUB):
        r0 = i * HG_SUB
        o_i = o_inter[r0:r0 + HG_SUB]
        if i > 0:
            ref_b = bcum[r0:r0 + 1]
            a = _dot_nt(q[r0:r0 + HG_SUB] * jnp.exp(bcum[r0:r0 + HG_SUB] - ref_b), k[:r0] * jnp.exp(ref_b - bcum[:r0]))
            o_i = o_i + _dot(a, v[:r0])
        for t0 in range(r0, r0 + HG_SUB, SUBLANES):
            b_t, q_t = bcum[t0:t0 + SUBLANES], q[t0:t0 + SUBLANES]
            acc = o_i[t0 - r0:t0 - r0 + SUBLANES]
            for s in range(r0, t0 + SUBLANES):
                d = b_t - b_s[s:s + 1, :]
                if s >= t0:
                    d = jnp.where(t_idx >= s - t0, d, NEG)
                w = jnp.exp(d) * q_t * k_s[s:s + 1, :]
                acc = acc + jnp.sum(w, axis=-1, keepdims=True) * v_s[s:s + 1, :]
            o_s[t0:t0 + SUBLANES, :] = acc


def _hgrn_prompt_kernel(hz_ref, lb_ref, ng_ref, tri_ref, o_ref, st_ref, state_t, b_s, k_s, v_s, o_s):
    c = pl.program_id(1)
    n_c = pl.num_programs(1)

    @pl.when(c == 0)
    def _():
        state_t[...] = jnp.zeros_like(state_t)

    lb_all = _forget_lower_bound(lb_ref[...])
    tri = tri_ref[...]

    def seg(j, h):
        return hz_ref[:, j * HG_WIDTH + h * HG_DIM:j * HG_WIDTH + (h + 1) * HG_DIM]

    heads = range(HG_HEADS)
    q, v = [seg(0, h) for h in heads], [seg(2, h) for h in heads]
    scratch = [[s.at[h] for h in heads] for s in (b_s, k_s, v_s, o_s)]
    o_inter, k, bcum, st_new, span = _hgrn_chunk_common(
        q, [seg(1, h) for h in heads], v, [lb_all[:, h * HG_DIM:(h + 1) * HG_DIM] for h in heads], tri,
        [state_t[h] for h in heads], *scratch[:3])
    for h in heads:
        state_t[h] = st_new[h]
    direct = jnp.max(span) > HG_SAFE_DECAY

    @pl.when(jnp.logical_not(direct))
    def _():
        _hgrn_intra_matmul(o_inter, q, k, bcum, v, scratch[3])

    @pl.when(direct)
    def _():
        for h in heads:
            _hgrn_intra_direct(o_inter[h], q[h], k[h], bcum[h], v[h], *(s[h] for s in scratch))

    for h in range(HG_HEADS):
        o_ref[:, h * HG_DIM:(h + 1) * HG_DIM] = _hgrn_out(o_s[h], seg(3, h), ng_ref[...]).astype(BF16)

    @pl.when(c == n_c - 1)
    def _():
        for h in range(HG_HEADS):
            st_ref[h] = state_t[h].T


def _hgrn_prompt(hz3, hg_lb, norm_g, *, lc):
    b, t, w = hz3.shape
    tri = jnp.tril(jnp.ones((lc, lc), F32)).astype(BF16)
    return pl.pallas_call(
        _hgrn_prompt_kernel,
        grid=(b, t // lc),
        in_specs=[pl.BlockSpec((None, lc, w), lambda bi, c: (bi, c, 0)), _const_spec(hg_lb.shape),
                  _const_spec((1, HG_DIM)), _const_spec((lc, lc))],
        out_specs=[pl.BlockSpec((None, lc, HG_WIDTH), lambda bi, c: (bi, c, 0)),
                   pl.BlockSpec((None, HG_HEADS, HG_DIM, HG_DIM), lambda bi, c: (bi, 0, 0, 0))],
        out_shape=[jax.ShapeDtypeStruct((b, t, HG_WIDTH), BF16),
                   jax.ShapeDtypeStruct((b, HG_HEADS, HG_DIM, HG_DIM), F32)],
        scratch_shapes=[pltpu.VMEM((HG_HEADS, HG_DIM, HG_DIM), F32)] + [pltpu.VMEM((HG_HEADS, lc, HG_DIM), F32)] * 4,
        compiler_params=_cparams(("arbitrary", "arbitrary")),
        name="hgrn_prompt",
    )(hz3, hg_lb, norm_g, tri)


def _hgrn_dec_kernel(hz_ref, s0_ref, lb_ref, ng_ref, o_ref, s_ref):
    hz = hz_ref[...]
    lb_all = _forget_lower_bound(lb_ref[...])
    eye = lax.broadcasted_iota(jnp.int32, (HG_DIM, HG_DIM), 0) == lax.broadcasted_iota(jnp.int32, (HG_DIM, HG_DIM), 1)

    def column(row):
        return jnp.sum(jnp.where(eye, row, 0.0), axis=1, keepdims=True)

    outs = []
    for h in range(HG_HEADS):
        def seg(j):
            return hz[:, j * HG_WIDTH + h * HG_DIM:j * HG_WIDTH + (h + 1) * HG_DIM]

        lb = lb_all[:, h * HG_DIM:(h + 1) * HG_DIM]
        f = lb + (1.0 - lb) * _sigmoid(seg(1))
        q, k, v = seg(0), 1.0 - f, seg(2)
        s0 = s0_ref[h]
        s_ref[h] = column(f) * s0 + column(k) * v
        o = jnp.sum(column(q * f) * s0, axis=0, keepdims=True) + jnp.sum(q * k, axis=-1, keepdims=True) * v
        outs.append(_hgrn_out(o, seg(3), ng_ref[...]))
    o_ref[...] = jnp.concatenate(outs, axis=1)


def _hgrn_dec(hz, s0, hg_lb, norm_g):
    db = hz.shape[0]
    return pl.pallas_call(
        _hgrn_dec_kernel,
        grid=(db,),
        in_specs=[pl.BlockSpec((None, 1, hz.shape[1]), lambda b: (b, 0, 0)),
                  pl.BlockSpec((None, HG_HEADS, HG_DIM, HG_DIM), lambda b: (b, 0, 0, 0)),
                  _const_spec(hg_lb.shape), _const_spec((1, HG_DIM))],
        out_specs=[pl.BlockSpec((None, 1, HG_WIDTH), lambda b: (b, 0, 0)),
                   pl.BlockSpec((None, HG_HEADS, HG_DIM, HG_DIM), lambda b: (b, 0, 0, 0))],
        out_shape=[jax.ShapeDtypeStruct((db, 1, HG_WIDTH), F32), jax.ShapeDtypeStruct(s0.shape, F32)],
        compiler_params=_cparams(("arbitrary",)),
        name="hgrn_dec",
    )(hz.reshape(db, 1, -1), s0, hg_lb, norm_g)


def _merge_core(on_ref, oh_ref, sm_ref, x_ref, g1_ref, sh_ref, sc_ref, ng_ref, wn_ref, wh_ref, wo_ref):
    d = x_ref.shape[1]
    sm = sm_ref[...]
    mix = sm[:, :d] * _dot(on_ref[...], wn_ref[...]) + sm[:, d:] * _dot(oh_ref[...], wh_ref[...])
    x1 = x_ref[...] + g1_ref[...] * _dot(mix, wo_ref[...])
    xn = x1 * lax.rsqrt(jnp.mean(x1 * x1, axis=-1, keepdims=True) + EPS) * ng_ref[...]
    return x1, (xn * (1.0 + sc_ref[...]) + sh_ref[...]).astype(BF16)


def _merge_kernel(*refs):
    x1_ref, h2_ref = refs[-2:]
    x1_ref[...], h2_ref[...] = _merge_core(*refs[:-2])


def _merge(o_nsa, o_hg, sm, x2d, g1, sh2, sc2, norm_g, w_br_nsa, w_br_hg, w_out, *, tm, tiles_per_mod):
    r, d = x2d.shape
    rows_mod = g1.shape[1]
    mod_spec = pl.BlockSpec((None, rows_mod, d), lambda i: (i // tiles_per_mod, 0, 0))

    def row_spec(width):
        return pl.BlockSpec((tm, width), lambda i: (i, 0))

    return pl.pallas_call(
        _merge_kernel,
        grid=(r // tm,),
        in_specs=[row_spec(o_nsa.shape[1]), row_spec(o_hg.shape[1]), row_spec(sm.shape[1]), row_spec(d),
                  mod_spec, mod_spec, mod_spec, _const_spec((1, d)),
                  _const_spec(w_br_nsa.shape), _const_spec(w_br_hg.shape), _const_spec(w_out.shape)],
        out_specs=[row_spec(d), row_spec(d)],
        out_shape=[jax.ShapeDtypeStruct((r, d), F32), jax.ShapeDtypeStruct((r, d), BF16)],
        compiler_params=_cparams(("arbitrary",)),
        name="merge",
    )(o_nsa, o_hg, sm, x2d, g1, sh2, sc2, norm_g, w_br_nsa, w_br_hg, w_out)


FF_CHUNKS = 2


def _ffn_core(h2, x1, g2, wu_ref, cw_ref, cb_ref, wd_ref, prev_rows):
    d_ff = wd_ref.shape[0]
    ch = d_ff // FF_CHUNKS
    acc = None
    us = []
    for ci in range(FF_CHUNKS):
        c0, c1 = ci * ch, (ci + 1) * ch
        u = jnp.dot(h2, wu_ref[:, c0:c1], preferred_element_type=F32)
        v = jnp.dot(h2, wu_ref[:, d_ff + c0:d_ff + c1], preferred_element_type=F32)
        u_m1, u_m2 = prev_rows(u, c0, c1)
        y = cb_ref[:, c0:c1] + cw_ref[0:1, c0:c1] * u_m2 + cw_ref[1:2, c0:c1] * u_m1 + cw_ref[2:3, c0:c1] * u
        part = _dot(y * _sigmoid(y) * v, wd_ref[c0:c1, :])
        acc = part if acc is None else acc + part
        us.append(u)
    return x1 + g2 * acc, us


def _post_prompt_kernel(*refs):
    merge_refs = refs[:11]
    g2_ref, wu_ref, cw_ref, cb_ref, wd_ref, y_ref, tail_ref, carry = refs[11:]
    ti = pl.program_id(1)
    tm = y_ref.shape[0]

    @pl.when(ti == 0)
    def _():
        carry[...] = jnp.zeros_like(carry)

    row = lax.broadcasted_iota(jnp.int32, (tm, 1), 0)

    def prev_rows(u, c0, c1):
        last = carry[:, c0:c1]
        m1 = jnp.where(row == 0, last[7:8], pltpu.roll(u, 1, 0))
        m2 = jnp.where(row == 0, last[6:7], jnp.where(row == 1, last[7:8], pltpu.roll(u, 2, 0)))
        return m1, m2

    x1, h2 = _merge_core(*merge_refs)
    out, us = _ffn_core(h2, x1, g2_ref[...], wu_ref, cw_ref, cb_ref, wd_ref, prev_rows)
    y_ref[...] = out
    ch = us[0].shape[1]
    for ci, u in enumerate(us):
        carry[:, ci * ch:(ci + 1) * ch] = u[tm - 8:, :]
    tail_ref[...] = carry[...]


def _post_prompt(o_nsa, o_hg, sm, x2d, mods, norm_g, w_br_nsa, w_br_hg, w_out, w_up, conv_w8, conv_b, w_down, *, b, tm):
    r, d = x2d.shape
    d_ff = w_down.shape[0]
    tpb = r // b // tm
    g1, sh2, sc2, g2 = mods

    def row_spec(width):
        return pl.BlockSpec((tm, width), lambda bi, i: (bi * tpb + i, 0))

    mod_spec = pl.BlockSpec((None, 1, d), lambda bi, i: (bi, 0, 0))
    return pl.pallas_call(
        _post_prompt_kernel,
        grid=(b, tpb),
        in_specs=[row_spec(o_nsa.shape[1]), row_spec(o_hg.shape[1]), row_spec(sm.shape[1]), row_spec(d),
                  mod_spec, mod_spec, mod_spec, _const_spec((1, d)),
                  _resident_spec(w_br_nsa.shape), _resident_spec(w_br_hg.shape), _resident_spec(w_out.shape),
                  mod_spec, _resident_spec(w_up.shape), _const_spec(conv_w8.shape), _const_spec(conv_b.shape),
                  _resident_spec(w_down.shape)],
        out_specs=[row_spec(d), pl.BlockSpec((None, 8, d_ff), lambda bi, i: (bi, 0, 0))],
        out_shape=[jax.ShapeDtypeStruct((r, d), F32), jax.ShapeDtypeStruct((b, 8, d_ff), F32)],
        scratch_shapes=[pltpu.VMEM((8, d_ff), F32)],
        compiler_params=_cparams(("arbitrary", "arbitrary")),
        name="post_prompt",
    )(o_nsa, o_hg, sm, x2d, g1, sh2, sc2, norm_g, w_br_nsa, w_br_hg, w_out, g2, w_up, conv_w8, conv_b, w_down)


def _ffn_dec_kernel(h2_ref, x1_ref, g2_ref, wu_ref, cw_ref, cb_ref, wd_ref, m1_ref, m2_ref, y_ref, u_ref):
    def prev_rows(u, c0, c1):
        return m1_ref[:, c0:c1], m2_ref[:, c0:c1]

    out, us = _ffn_core(h2_ref[...], x1_ref[...], g2_ref[...], wu_ref, cw_ref, cb_ref, wd_ref, prev_rows)
    y_ref[...] = out
    u_ref[...] = jnp.concatenate(us, axis=1)


def _ffn_dec(h2, x1, g2, w_up, conv_w8, conv_b, w_down, u_m1, u_m2):
    r, d = x1.shape
    d_ff = w_down.shape[0]
    args = (h2, x1, g2, w_up, conv_w8, conv_b, w_down, u_m1, u_m2)
    return pl.pallas_call(
        _ffn_dec_kernel,
        grid=(1,),
        in_specs=[_const_spec(a.shape) for a in args],
        out_specs=[_const_spec((r, d)), _const_spec((r, d_ff))],
        out_shape=[jax.ShapeDtypeStruct((r, d), F32), jax.ShapeDtypeStruct((r, d_ff), F32)],
        compiler_params=_cparams(("arbitrary",)),
        name="ffn_dec",
    )(*args)


def _pack_w_in(w_in):
    cuts = np.cumsum([NSA_WIDTH] + [KV_WIDTH] * 6 + [3 * NSA_HEADS] + [HG_WIDTH] * 4)
    q, kv, gates, rest = w_in[:, :cuts[0]], w_in[:, cuts[0]:cuts[6]], w_in[:, cuts[6]:cuts[7]], w_in[:, cuts[7]:]
    hz, m = rest[:, :4 * HG_WIDTH], rest[:, 4 * HG_WIDTH:]
    per_group = 3 * HPG
    pad = jnp.zeros((w_in.shape[0], LANES - per_group), w_in.dtype)
    gate_cols = []
    for g in range(NSA_GROUPS):
        gate_cols += [gates[:, g * per_group:(g + 1) * per_group], pad]
    return jnp.concatenate([q, kv] + gate_cols + [hz, m], axis=1).astype(BF16)


def _rope_tables(pos):
    pos = np.asarray(pos, np.float64)
    half = ROT_DIM // 2
    inv = ROPE_THETA ** (-np.arange(0, ROT_DIM, 2, dtype=np.float64) / ROT_DIM)
    ang = pos[:, None] * inv[None, :]
    cos, sin = np.cos(ang), np.sin(ang)
    n = pos.shape[0]
    ones = np.ones((n, HEAD_DIM - ROT_DIM))
    zeros = np.zeros((n, HEAD_DIM - ROT_DIM))
    z8 = np.zeros((n, half))
    c = np.concatenate([cos, cos, ones], axis=1)
    s_lo = np.concatenate([-sin, z8, zeros], axis=1)
    s_hi = np.concatenate([z8, sin, zeros], axis=1)
    return tuple(jnp.asarray(np.concatenate([t, t], axis=1), F32) for t in (c, s_lo, s_hi))


def _cmp_weights(cmp_w1, cmp_w2, cmp_pe):
    n_half = CMP_BLK // CMP_STRIDE
    eye = jnp.eye(NSA_GROUPS, dtype=F32)
    w1s, w2s, pes = [], [], []
    for j in range(2):
        w = cmp_w1[j].reshape(n_half, CMP_STRIDE, HEAD_DIM, HEAD_DIM)
        big = jnp.einsum('rsdh,ge->sgdreh', w, eye)
        w1s.append(big.reshape(CMP_STRIDE * KV_WIDTH, n_half * KV_WIDTH))
        w2s.append(jnp.einsum('dh,ge->gdeh', cmp_w2[j], eye).reshape(KV_WIDTH, KV_WIDTH))
        pe = cmp_pe[j].reshape(n_half, CMP_STRIDE, 1, HEAD_DIM)
        pe = jnp.broadcast_to(pe, (n_half, CMP_STRIDE, NSA_GROUPS, HEAD_DIM)).reshape(n_half, -1)
        pes.append(jnp.concatenate([pe, jnp.zeros((8 - n_half, pe.shape[1]), F32)], axis=0))
    return jnp.stack(w1s).astype(BF16), jnp.stack(w2s).astype(BF16), jnp.stack(pes).astype(BF16)


def _overlap(nc_pad, nc, ns, ns_pad):
    start = np.arange(nc_pad) * CMP_STRIDE
    end = start + CMP_BLK - 1
    s0 = np.arange(ns_pad) * SLC_BLK
    s1 = s0 + SLC_BLK - 1
    m = (start[:, None] <= s1[None, :]) & (end[:, None] >= s0[None, :])
    m &= (np.arange(nc_pad) < nc)[:, None] & (np.arange(ns_pad) < ns)[None, :]
    return jnp.asarray(m, BF16)


def kernel(x_prompt, x_sample, cache_cmp_k, cache_cmp_v, cache_slc_k, cache_slc_v, cache_win_k, cache_win_v,
           state_hgrn, state_ffn_conv, page_table, c_prompt, c_sample, ada_w, ada_b, norm_attn, norm_ffn, w_in,
           q_gain, k_gain, cmp_w1, cmp_w2, cmp_pe, hg_lb, hg_norm, w_br_nsa, w_br_hg, w_out, w_up, conv_w,
           conv_b, w_down):
    depth = ada_w.shape[0]
    assert depth == 1, "single-layer trunk only"
    b, t, d = x_prompt.shape
    db, tn, _ = x_sample.shape
    assert tn == 1
    n_pages = page_table.shape[1]
    past = n_pages * PAGE_SIZE
    d_ff = w_down.shape[1]
    tq = 256
    tk = 256
    tm = 256
    lc = 128
    assert t >= WINDOW and n_pages % PAGES_PER_STEP == 0
    assert cache_win_k.shape[2] == WINDOW

    w_pack = _pack_w_in(w_in[0])
    qg = jnp.tile(q_gain[0], NSA_HEADS).reshape(1, NSA_WIDTH)
    kg = jnp.concatenate([jnp.tile(k_gain[0], (1, NSA_GROUPS)), jnp.zeros((5, KV_WIDTH), F32)], axis=0)
    hd = np.arange(NSA_WIDTH) // HEAD_DIM
    bd = jnp.asarray(hd[:, None] == hd[None, :], BF16)
    w1c, w2c, pe_x = _cmp_weights(cmp_w1[0], cmp_w2[0], cmp_pe[0])
    conv_w8 = jnp.concatenate([conv_w[0], jnp.zeros((8 - CONV_W, d_ff), F32)], axis=0)
    conv_b2 = conv_b[0].reshape(1, d_ff)
    wbn, wbh, wo = w_br_nsa[0].astype(BF16), w_br_hg[0].astype(BF16), w_out[0].astype(BF16)
    wu, wd = w_up[0].astype(BF16), w_down[0].astype(BF16)
    norm_a, norm_f, hg_n = norm_attn[0].reshape(1, d), norm_ffn[0].reshape(1, d), hg_norm[0].reshape(1, HG_DIM)

    n_c = b + db
    c_all = jnp.concatenate([c_prompt, c_sample, jnp.zeros((-n_c % 8, d), F32)], axis=0)
    ada = _ada(c_all, ada_w[0], ada_b[0])
    mods_p = [ada[:b, i * d:(i + 1) * d].reshape(b, 1, d) for i in range(6)]
    mods_s = [ada[b:n_c, i * d:(i + 1) * d].reshape(1, db, d) for i in range(6)]

    tpb = t // tm
    (q, kc_r, vc_r, ks_r, vs_r, kw_r, vw_r, kcb, vcb, kvb, gt, hz, sm) = _inproj(
        x_prompt.reshape(b * t, d), mods_p[0], mods_p[1], norm_a, w_pack, qg, kg, _rope_tables(np.arange(t)), bd,
        tm=tm, tiles_per_mod=tpb, rope_tiles=tpb)

    nsub = t // CMP_STRIDE
    pk = _mm(kcb, w1c[0], 512, "cmp_proj_k").reshape(b, nsub, -1)
    pv = _mm(vcb, w1c[1], 512, "cmp_proj_v").reshape(b, nsub, -1)
    kcv = _cmp_combine(pk, pv, pe_x, w1c, w2c)
    ov = _overlap(nsub, nsub - 1, t // SLC_BLK, LANES).T
    o_nsa = _nsa_prompt(q.reshape(b, t, -1), kcv, kvb.reshape(b, t, -1), gt.reshape(b, t, -1), ov, tq=tq, tk=tk)

    o_hg, st_p = _hgrn_prompt(hz.reshape(b, t, -1), hg_lb, hg_n, lc=lc)

    y_p, tail = _post_prompt(o_nsa.reshape(b * t, -1), o_hg.reshape(b * t, -1), sm, x_prompt.reshape(b * t, d),
                             mods_p[2:], norm_f, wbn, wbh, wo, wu, conv_w8, conv_b2, wd, b=b, tm=tm)

    def rows5(a):
        return a.reshape(1, b, t, NSA_GROUPS, HEAD_DIM)

    w = min(WINDOW, t)
    out_p = (rows5(kc_r), rows5(vc_r), rows5(ks_r), rows5(vs_r), rows5(kw_r)[:, :, -w:], rows5(vw_r)[:, :, -w:],
             st_p[None], tail[None, :, 8 - (CONV_W - 1):, :])

    pos_s = np.full((db,), past)
    (q_s, kc_s, vc_s, ks_s, vs_s, kw_s, vw_s, _, _, _, gt_s, hz_s, sm_s) = _inproj(
        x_sample.reshape(db, d), mods_s[0], mods_s[1], norm_a, w_pack, qg, kg, _rope_tables(pos_s), bd,
        tm=db, tiles_per_mod=1, rope_tiles=1)

    def feature_major(a):
        return jnp.transpose(a, (0, 2, 3, 1))

    assert (past + tn) // CMP_STRIDE == past // CMP_STRIDE
    pk_s, pv_s = _cmp_proj_paged(feature_major(cache_cmp_k[0]).reshape(-1, KV_WIDTH, PAGE_SIZE),
                                 feature_major(cache_cmp_v[0]).reshape(-1, KV_WIDTH, PAGE_SIZE), page_table, w1c)
    kcv_s = _cmp_combine(pk_s, pv_s, pe_x, w1c, w2c)
    nsub_s = past // CMP_STRIDE
    ns_s = -(-(past + tn) // SLC_BLK)
    ns_pad = -(-ns_s // LANES) * LANES
    ov_s = _overlap(nsub_s, nsub_s - 1, ns_s, ns_pad)
    o_cmp_s, idx8 = _nsa_dec_select(q_s.reshape(db, NSA_HEADS, HEAD_DIM), kcv_s, ov_s, past)
    idx = idx8[:, :NSA_GROUPS, :N_SEL].reshape(db, NSA_GROUPS * N_SEL)
    new_cols = [a.reshape(db, NSA_GROUPS, HEAD_DIM, 1) for a in (ks_s, vs_s, kw_s, vw_s)]
    per_group = 3 * HPG
    gt4 = jnp.stack([gt_s[:, g * LANES:g * LANES + per_group] for g in range(NSA_GROUPS)], axis=1)
    o_nsa_s, wk_new, wv_new = _nsa_dec_attend(
        idx, page_table, feature_major(cache_slc_k[0]), feature_major(cache_slc_v[0]),
        q_s.astype(F32).reshape(db, NSA_GROUPS, HPG, HEAD_DIM), new_cols,
        feature_major(cache_win_k[0]), feature_major(cache_win_v[0]),
        o_cmp_s.reshape(db, NSA_GROUPS, HPG, HEAD_DIM), gt4.reshape(db, NSA_GROUPS, HPG, 3), past)
    wk_new, wv_new = (jnp.transpose(a, (0, 3, 1, 2)) for a in (wk_new, wv_new))

    o_hg_s, st_s = _hgrn_dec(hz_s, state_hgrn[0], hg_lb, hg_n)
    x1_s, h2_s = _merge(o_nsa_s.reshape(db, NSA_WIDTH), o_hg_s.reshape(db, HG_WIDTH), sm_s, x_sample.reshape(db, d),
                        mods_s[2], mods_s[3], mods_s[4], norm_f, wbn, wbh, wo, tm=db, tiles_per_mod=1)
    buf = state_ffn_conv[0]
    y_s, u_s = _ffn_dec(h2_s, x1_s, mods_s[5].reshape(db, d), wu, conv_w8, conv_b2, wd, buf[:, 1], buf[:, 0])
    conv_s = jnp.stack([buf[:, 1], u_s], axis=1)

    def rows5s(a):
        return a.reshape(1, db, tn, NSA_GROUPS, HEAD_DIM)

    out_s = (rows5s(kc_s), rows5s(vc_s), rows5s(ks_s), rows5s(vs_s),
             wk_new.reshape(1, db, WINDOW, NSA_GROUPS, HEAD_DIM), wv_new.reshape(1, db, WINDOW, NSA_GROUPS, HEAD_DIM),
             st_s[None], conv_s[None])
    return (y_p.reshape(b, t, d), y_s.reshape(db, tn, d)) + out_p + out_s
```

```python
import functools

import numpy as np
import jax
import jax.numpy as jnp
from jax import lax
from jax.experimental import pallas as pl
from jax.experimental.pallas import tpu as pltpu

F32 = jnp.float32
BF16 = jnp.bfloat16

HEAD_DIM = 64
NSA_HEADS = 8
NSA_GROUPS = 2
HPG = NSA_HEADS // NSA_GROUPS
ROT_DIM = HEAD_DIM // 4
ROPE_THETA = 500000.0
CMP_BLK = 32
CMP_STRIDE = 16
SLC_BLK = 64
N_SEL = 16
WINDOW = 512
PAGE_SIZE = 128
HG_HEADS = 4
HG_DIM = 128
HG_SUB_LOG2 = 4
HG_SUB = 1 << HG_SUB_LOG2
HG_SAFE_DECAY = 60.0
SUBLANES = 8
SOFTMAX_VREGS = 32
CONV_W = 3
EPS = 1e-6
NEG = -1e30
SEL_BIAS = -1e9
LOG2E = 1.4426950408889634
LANES = 128
VMEM_LIMIT = 56 * 1024 * 1024

NSA_WIDTH = NSA_HEADS * HEAD_DIM
KV_WIDTH = NSA_GROUPS * HEAD_DIM
HG_WIDTH = HG_HEADS * HG_DIM
GRP_Q = HPG * HEAD_DIM

SEG_Q = 0
SEG_KV = SEG_Q + NSA_WIDTH
SEG_G = SEG_KV + 6 * KV_WIDTH
SEG_H = SEG_G + NSA_GROUPS * LANES
SEG_M = SEG_H + 4 * HG_WIDTH


def _cparams(sem):
    return pltpu.CompilerParams(dimension_semantics=sem, vmem_limit_bytes=VMEM_LIMIT)


def _dot(a, b):
    return jnp.dot(a.astype(BF16), b.astype(BF16), preferred_element_type=F32)


def _dot_nt(a, b):
    return lax.dot_general(a.astype(BF16), b.astype(BF16), (((1,), (1,)), ((), ())), preferred_element_type=F32)


def _split2(x):
    hi = x.astype(BF16)
    lo = (x - hi.astype(F32)).astype(BF16)
    return hi, lo


def _split3(x):
    hi = x.astype(BF16)
    r = x - hi.astype(F32)
    mid = r.astype(BF16)
    lo = (r - mid.astype(F32)).astype(BF16)
    return hi, mid, lo


def _dot_x3(a, b):
    ah, al = _split2(a)
    bh, bl = _split2(b)
    return (jnp.dot(ah, bh, preferred_element_type=F32) + jnp.dot(al, bh, preferred_element_type=F32)
            + jnp.dot(ah, bl, preferred_element_type=F32))


def _sigmoid(x):
    return 1.0 / (1.0 + jnp.exp(-x))


def _const_spec(shape):
    nd = len(shape)
    return pl.BlockSpec(shape, lambda *_: (0,) * nd)


def _resident_spec(shape):
    nd = len(shape)
    return pl.BlockSpec(shape, lambda *_: (0,) * nd, pipeline_mode=pl.Buffered(1))


def _ada_kernel(c_ref, w_ref, b_ref, o_ref):
    o_ref[...] = _dot_x3(c_ref[...], w_ref[...]) + b_ref[...]


def _ada(c_all, w, b):
    r, d = c_all.shape
    n = w.shape[1]
    tn = 1536
    return pl.pallas_call(
        _ada_kernel,
        grid=(n // tn,),
        in_specs=[_const_spec((r, d)), pl.BlockSpec((d, tn), lambda j: (0, j)), pl.BlockSpec((1, tn), lambda j: (0, j))],
        out_specs=pl.BlockSpec((r, tn), lambda j: (0, j)),
        out_shape=jax.ShapeDtypeStruct((r, n), F32),
        compiler_params=_cparams(("arbitrary",)),
        name="ada",
    )(c_all, w, b.reshape(1, n))


def _rope(x, c, s_lo, s_hi):
    width = x.shape[1]
    reps = width // LANES
    if reps > 1:
        c = jnp.concatenate([c] * reps, axis=1)
        s_lo = jnp.concatenate([s_lo] * reps, axis=1)
        s_hi = jnp.concatenate([s_hi] * reps, axis=1)
    half = ROT_DIM // 2
    return x * c + pltpu.roll(x, half, 1) * s_hi + pltpu.roll(x, width - half, 1) * s_lo


def _head_rms(z, bd):
    hi, lo = _split2(z * z)
    ss = jnp.dot(hi, bd, preferred_element_type=F32) + jnp.dot(lo, bd, preferred_element_type=F32)
    return z * lax.rsqrt(ss * (1.0 / HEAD_DIM) + EPS)


def _inproj_kernel(x_ref, sh_ref, sc_ref, ng_ref, w_ref, qg_ref, kg_ref, rc_ref, rlo_ref, rhi_ref, bd_ref,
                   q_ref, kc_ref, vc_ref, ks_ref, vs_ref, kw_ref, vw_ref, kcb_ref, vcb_ref, kvb_ref, gt_ref, hz_ref,
                   sm_ref, kc_scr, vc_scr):
    x = x_ref[...]
    xn = x * lax.rsqrt(jnp.mean(x * x, axis=-1, keepdims=True) + EPS) * ng_ref[...]
    h = (xn * (1.0 + sc_ref[...]) + sh_ref[...]).astype(BF16)
    rc, rlo, rhi = rc_ref[...], rlo_ref[...], rhi_ref[...]
    bd = bd_ref[...]

    zq = jnp.dot(h, w_ref[:, SEG_Q:SEG_KV], preferred_element_type=F32)
    q = _rope(_head_rms(zq, bd) * qg_ref[...], rc, rlo, rhi)
    q_ref[...] = (q * (HEAD_DIM ** -0.5 * LOG2E)).astype(BF16)

    zkv = jnp.dot(h, w_ref[:, SEG_KV:SEG_G], preferred_element_type=F32)
    bd1 = bd[:LANES, :LANES]
    k_refs = (kc_ref, ks_ref, kw_ref)
    v_refs = (vc_ref, vs_ref, vw_ref)
    rows = []
    for j in range(3):
        zk = zkv[:, 2 * j * KV_WIDTH:(2 * j + 1) * KV_WIDTH]
        k = _rope(_head_rms(zk, bd1) * kg_ref[j:j + 1, :], rc, rlo, rhi)
        v = zkv[:, (2 * j + 1) * KV_WIDTH:(2 * j + 2) * KV_WIDTH]
        for g in range(NSA_GROUPS):
            k_refs[j][:, g, :] = k[:, g * HEAD_DIM:(g + 1) * HEAD_DIM]
            v_refs[j][:, g, :] = v[:, g * HEAD_DIM:(g + 1) * HEAD_DIM]
        rows.append((k, v))
    for src, scr, dst in ((rows[0][0], kc_scr, kcb_ref), (rows[0][1], vc_scr, vcb_ref)):
        scr[...] = src
        for s in range(CMP_STRIDE):
            dst[:, s * KV_WIDTH:(s + 1) * KV_WIDTH] = scr[pl.ds(s, dst.shape[0], stride=CMP_STRIDE), :].astype(BF16)
    parts = []
    for g in range(NSA_GROUPS):
        for k, v in rows[1:]:
            parts.append(k[:, g * HEAD_DIM:(g + 1) * HEAD_DIM])
            parts.append(v[:, g * HEAD_DIM:(g + 1) * HEAD_DIM])
    kvb_ref[...] = jnp.concatenate(parts, axis=1).astype(BF16)

    gt_ref[...] = _sigmoid(jnp.dot(h, w_ref[:, SEG_G:SEG_H], preferred_element_type=F32))
    hz_ref[...] = jnp.dot(h, w_ref[:, SEG_H:SEG_M], preferred_element_type=F32)
    sm_ref[...] = _sigmoid(jnp.dot(h, w_ref[:, SEG_M:], preferred_element_type=F32)).astype(BF16)


def _inproj(x2d, shift, scale, norm_g, w_pack, qg, kg, rope_tabs, bd, *, tm, tiles_per_mod, rope_tiles):
    r, d = x2d.shape
    n_tiles = r // tm
    rows_mod = shift.shape[1]
    mod_spec = pl.BlockSpec((None, rows_mod, d), lambda i: (i // tiles_per_mod, 0, 0))
    rope_spec = pl.BlockSpec((tm, LANES), lambda i: (i % rope_tiles, 0))

    def row_spec(width):
        return pl.BlockSpec((tm, width), lambda i: (i, 0))

    widths = (4 * KV_WIDTH, NSA_GROUPS * LANES, 4 * HG_WIDTH, w_pack.shape[1] - SEG_M)
    dtypes = (BF16, F32, F32, BF16)
    cache_spec = pl.BlockSpec((tm, NSA_GROUPS, HEAD_DIM), lambda i: (i, 0, 0))
    cache_shape = jax.ShapeDtypeStruct((r, NSA_GROUPS, HEAD_DIM), F32)
    assert tm % CMP_STRIDE == 0
    sub_spec = pl.BlockSpec((tm // CMP_STRIDE, CMP_STRIDE * KV_WIDTH), lambda i: (i, 0))
    sub_shape = jax.ShapeDtypeStruct((r // CMP_STRIDE, CMP_STRIDE * KV_WIDTH), BF16)
    return pl.pallas_call(
        _inproj_kernel,
        grid=(n_tiles,),
        in_specs=[row_spec(d), mod_spec, mod_spec, _const_spec((1, d)), _resident_spec(w_pack.shape),
                  _const_spec(qg.shape), _const_spec(kg.shape), rope_spec, rope_spec, rope_spec, _const_spec(bd.shape)],
        out_specs=[row_spec(NSA_WIDTH)] + [cache_spec] * 6 + [sub_spec] * 2 + [row_spec(w) for w in widths],
        out_shape=[jax.ShapeDtypeStruct((r, NSA_WIDTH), BF16)] + [cache_shape] * 6 + [sub_shape] * 2
        + [jax.ShapeDtypeStruct((r, w), dt) for w, dt in zip(widths, dtypes)],
        scratch_shapes=[pltpu.VMEM((tm, KV_WIDTH), F32)] * 2,
        compiler_params=_cparams(("arbitrary",)),
        name="inproj",
    )(x2d, shift, scale, norm_g, w_pack, qg, kg, *rope_tabs, bd)


def _mm_kernel(a_ref, w_ref, o_ref):
    o_ref[...] = _dot(a_ref[...], w_ref[...])


def _mm(a, w, tm, name):
    m, k = a.shape
    n = w.shape[1]
    tm = min(tm, m)
    return pl.pallas_call(
        _mm_kernel,
        grid=(m // tm,),
        in_specs=[pl.BlockSpec((tm, k), lambda i: (i, 0)), _const_spec((k, n))],
        out_specs=pl.BlockSpec((tm, n), lambda i: (i, 0)),
        out_shape=jax.ShapeDtypeStruct((m, n), F32),
        compiler_params=_cparams(("arbitrary",)),
        name=name,
    )(a, w)


PAGES_PER_STEP = 16


def _cmp_proj_paged_kernel(pt_ref, *refs):
    k_pages, v_pages = refs[:PAGES_PER_STEP], refs[PAGES_PER_STEP:2 * PAGES_PER_STEP]
    perm_ref, w_ref, ok_ref, ov_ref, xk_ref, xv_ref = refs[2 * PAGES_PER_STEP:]
    sub = PAGE_SIZE // CMP_STRIDE
    perm = perm_ref[...]
    half = PAGES_PER_STEP // 2
    for kv, (pages, x_ref, o_ref) in enumerate(((k_pages, xk_ref, ok_ref), (v_pages, xv_ref, ov_ref))):
        for j0 in (0, half):
            stacked = jnp.concatenate([pr[...].astype(BF16) for pr in pages[j0:j0 + half]], axis=0)
            rows = _dot_nt(perm, stacked)
            for s in range(CMP_STRIDE):
                for j in range(half):
                    x_ref[(j0 + j) * sub:(j0 + j + 1) * sub, s * KV_WIDTH:(s + 1) * KV_WIDTH] = (
                        rows[s * sub:(s + 1) * sub, j * KV_WIDTH:(j + 1) * KV_WIDTH])
        o_ref[...] = _dot(x_ref[...], w_ref[kv])


def _cmp_proj_paged(pool_k, pool_v, page_table, w):
    db, n_pages = page_table.shape
    sub = PAGE_SIZE // CMP_STRIDE
    n = w.shape[2]
    steps = n_pages // PAGES_PER_STEP
    pos = np.arange(PAGE_SIZE)
    perm = jnp.asarray((pos[:, None] % sub) * CMP_STRIDE + pos[:, None] // sub == pos[None, :], BF16)
    page_specs = [pl.BlockSpec((None, KV_WIDTH, PAGE_SIZE), lambda b, c, pt, j=j: (pt[b, c * PAGES_PER_STEP + j], 0, 0))
                  for j in range(PAGES_PER_STEP)]
    out_spec = pl.BlockSpec((None, PAGES_PER_STEP * sub, n), lambda b, c, pt: (b, c, 0))
    rows_scratch = pltpu.VMEM((PAGES_PER_STEP * sub, CMP_STRIDE * KV_WIDTH), F32)
    grid_spec = pltpu.PrefetchScalarGridSpec(
        num_scalar_prefetch=1,
        grid=(db, steps),
        in_specs=page_specs + page_specs + [pl.BlockSpec(perm.shape, lambda b, c, pt: (0, 0)),
                                            pl.BlockSpec(w.shape, lambda b, c, pt: (0, 0, 0))],
        out_specs=[out_spec, out_spec],
        scratch_shapes=[rows_scratch, rows_scratch],
    )
    out = jax.ShapeDtypeStruct((db, n_pages * sub, n), F32)
    return pl.pallas_call(
        _cmp_proj_paged_kernel,
        grid_spec=grid_spec,
        out_shape=[out, out],
        compiler_params=_cparams(("arbitrary", "arbitrary")),
        name="cmp_proj_paged",
    )(page_table, *([pool_k] * PAGES_PER_STEP), *([pool_v] * PAGES_PER_STEP), perm, w)


def _cmp_combine_kernel(pk_ref, pv_ref, pe_ref, w1_ref, w2_ref, o_ref):
    n = pk_ref.shape[0]
    outs = []
    for j, p_ref in enumerate((pk_ref, pv_ref)):
        pe = jnp.dot(pe_ref[j], w1_ref[j], preferred_element_type=F32)
        pe_hid = pe[0:1, :KV_WIDTH] + pe[1:2, KV_WIDTH:]
        p = p_ref[...]
        nxt = pltpu.roll(p[:, KV_WIDTH:], n - 1, 0)
        hid = pe_hid + p[:, :KV_WIDTH] + nxt
        outs.append(_dot(jax.nn.gelu(hid), w2_ref[j]))
    for g in range(NSA_GROUPS):
        o_ref[g] = jnp.concatenate([o[:, g * HEAD_DIM:(g + 1) * HEAD_DIM] for o in outs], axis=1).astype(BF16)


def _cmp_combine(pk, pv, pe_x, w1, w2bd):
    bx, n, c = pk.shape
    return pl.pallas_call(
        _cmp_combine_kernel,
        grid=(bx,),
        in_specs=[pl.BlockSpec((None, n, c), lambda b: (b, 0, 0)), pl.BlockSpec((None, n, c), lambda b: (b, 0, 0)),
                  _const_spec(pe_x.shape), _const_spec(w1.shape), _const_spec(w2bd.shape)],
        out_specs=pl.BlockSpec((None, NSA_GROUPS, n, 2 * HEAD_DIM), lambda b: (b, 0, 0, 0)),
        out_shape=jax.ShapeDtypeStruct((bx, NSA_GROUPS, n, 2 * HEAD_DIM), BF16),
        compiler_params=_cparams(("arbitrary",)),
        name="cmp_combine",
    )(pk, pv, pe_x, w1, w2bd)


def _masked_softmax(s, mask):
    s = jnp.where(mask, s, NEG)
    m = jnp.max(s, axis=-1, keepdims=True)
    e = jnp.where(mask, jnp.exp2(s - m), 0.0)
    return e / jnp.maximum(jnp.sum(e, axis=-1, keepdims=True), 1e-30)


def _top_k_mask(score, k, axis, idx_out=False):
    n = score.shape[axis]
    pos = lax.broadcasted_iota(jnp.int32, score.shape, axis).astype(F32)
    idx = jnp.zeros((score.shape[0], LANES), jnp.int32)
    lane_out = lax.broadcasted_iota(jnp.int32, (score.shape[0], LANES), 1)
    work = score
    for it in range(k):
        mx = jnp.max(work, axis=axis, keepdims=True)
        first = jnp.min(jnp.where(work == mx, pos, float(n)), axis=axis, keepdims=True)
        work = jnp.where(pos == first, -jnp.inf, work)
        if idx_out:
            idx = jnp.where(lane_out == it, first.astype(jnp.int32), idx)
    sel = work == -jnp.inf
    return (sel, idx) if idx_out else sel


def _block_scores(imp, qpos, axis):
    blk = lax.broadcasted_iota(jnp.int32, imp.shape, axis)
    cur = qpos >> 6
    valid = (blk << 6) <= qpos
    forced = (blk == 0) | (blk == cur) | (blk == cur - 1)
    return jnp.where(forced, 1e9, jnp.where(valid, imp, -1e9))


def _nsa_prompt_kernel(q_ref, kcv_ref, kvb_ref, gt_ref, ov_ref, o_ref, *scr, tq, tk):
    qi = pl.program_id(2)
    q0 = qi * tq
    nc = kcv_ref.shape[0]
    rows = HPG * tq

    q = q_ref[...]
    qs = jnp.concatenate([q[:, h * HEAD_DIM:(h + 1) * HEAD_DIM] for h in range(HPG)], axis=0)
    qpos = q0 + lax.broadcasted_iota(jnp.int32, (tq, 1), 0)
    qpos4 = jnp.concatenate([qpos] * HPG, axis=0)

    lane_blk = lax.broadcasted_iota(jnp.int32, (1, LANES), 1)
    kcol = lax.broadcasted_iota(jnp.int32, (tk, 1), 0)
    krow = lax.broadcasted_iota(jnp.int32, (1, tk), 1)
    ones_v = jnp.ones((tk, LANES - HEAD_DIM), BF16)
    rb = SOFTMAX_VREGS * SUBLANES * LANES // tk
    qpos_rb = [q0 + (r % (tq // rb)) * rb + lax.broadcasted_iota(jnp.int32, (rb, 1), 0) for r in range(rows // rb)]

    def key_tile(j):
        return kvb_ref[pl.ds(pl.multiple_of(j * tk, tk), tk), :]

    def branch(s0, s1, p0, p1, a0, a1, m_scr, acc_scr, slc):
        s_bufs, p_bufs, a_bufs = (s0, s1), (p0, p1), (a0, a1)
        v_lane0 = HEAD_DIM if slc else 3 * HEAD_DIM

        def scores(slot, j):
            blk = key_tile(j)
            if slc:
                onehot = jnp.where(((j * tk + kcol) >> 6) == lane_blk, 1.0, 0.0).astype(BF16)
                q_lhs, k_rhs = q_aug, jnp.concatenate([onehot, blk[:, 0:HEAD_DIM]], axis=1)
            else:
                q_lhs, k_rhs = qs, blk[:, 2 * HEAD_DIM:3 * HEAD_DIM]
            s_bufs[slot][...] = _dot_nt(q_lhs, k_rhs)

        def softmax(slot, j, mask_kind, live=None):
            k0 = j * tk
            for r in range(rows // rb):
                sl = slice(r * rb, (r + 1) * rb)
                s = s_bufs[slot][sl, :]
                keep = None
                if mask_kind == "causal":
                    keep = (k0 + krow) <= qpos_rb[r]
                elif mask_kind == "window":
                    keep = qpos_rb[r] - (k0 + krow) < WINDOW
                if live is not None:
                    keep = live if keep is None else jnp.logical_and(keep, live)
                if keep is not None:
                    s = jnp.where(keep, s, NEG)
                m_old = m_scr[sl, :]
                m_new = jnp.maximum(m_old, jnp.max(s, axis=-1, keepdims=True))
                a_bufs[slot][sl, :] = jnp.exp2(m_old - m_new)
                m_scr[sl, :] = m_new
                p_bufs[slot][sl, :] = jnp.exp2(s - jnp.concatenate([m_new] * (tk // LANES), axis=1)).astype(BF16)

        def accumulate(slot, j):
            v_aug = jnp.concatenate([key_tile(j)[:, v_lane0:v_lane0 + HEAD_DIM], ones_v], axis=1)
            acc_scr[...] = a_bufs[slot][...] * acc_scr[...] + jnp.dot(p_bufs[slot][...], v_aug,
                                                                      preferred_element_type=F32)

        def reset():
            m_scr[...] = jnp.full(m_scr.shape, NEG, F32)
            acc_scr[...] = jnp.zeros(acc_scr.shape, F32)

        def result():
            acc = acc_scr[...]
            return acc[:, :HEAD_DIM] / jnp.maximum(acc[:, HEAD_DIM:HEAD_DIM + 1], 1e-30)

        return scores, softmax, accumulate, reset, result

    jd = q0 // tk
    n_scr = len(scr) // 2

    scores, softmax, accumulate, reset, result = branch(*scr[n_scr:], slc=False)
    reset()
    n_back = -(-(WINDOW - 1) // tk)
    tiles = [(jd, "causal", None)] + [
        (jnp.maximum(jd - back, 0), "window" if (back + 1) * tk - 1 >= WINDOW else None, jd >= back)
        for back in range(1, n_back + 1)]
    scores(0, tiles[0][0])
    for i, (j, mask_kind, live) in enumerate(tiles):
        if i + 1 < len(tiles):
            scores((i + 1) % 2, tiles[i + 1][0])
        softmax(i % 2, j, mask_kind, live)
        accumulate(i % 2, j)
    o_win = result()

    kcv = kcv_ref[...]
    kc, vc = kcv[:, :HEAD_DIM], kcv[:, HEAD_DIM:]
    cmp_end = lax.broadcasted_iota(jnp.int32, (1, nc), 1) * CMP_STRIDE + (CMP_BLK - 1)
    p = _masked_softmax(_dot_nt(qs, kc), cmp_end <= qpos4)
    o_cmp = _dot(p, vc)
    psum = p[0:tq]
    for h in range(1, HPG):
        psum = psum + p[h * tq:(h + 1) * tq]
    p_hi, p_lo = _split2(psum)
    ov_t = ov_ref[...]
    imp_t = _dot_nt(ov_t, p_hi) + _dot_nt(ov_t, p_lo)

    qpos_row = q0 + lax.broadcasted_iota(jnp.int32, (1, tq), 1)
    sel_t = _top_k_mask(_block_scores(imp_t, qpos_row, 0), N_SEL, 0)
    bias = jnp.where(sel_t, 0.0, SEL_BIAS).T.astype(BF16)
    q_aug = jnp.concatenate([jnp.concatenate([bias] * HPG, axis=0), qs], axis=1)

    scores, softmax, accumulate, reset, result = branch(*scr[:n_scr], slc=True)
    reset()
    scores(0, jd)
    scores(1, 0)
    softmax(0, jd, "causal")

    def past_pair(u, carry):
        ta, tb = 2 * u, 2 * u + 1
        scores(0, jnp.minimum(tb, jd))
        softmax(1, ta, None)
        accumulate(0, jnp.where(u == 0, jd, ta - 1))
        scores(1, jnp.minimum(tb + 1, jd))
        softmax(0, tb, None, live=tb < jd)
        accumulate(1, ta)
        return carry

    n_pairs = (jd + 1) // 2
    lax.fori_loop(0, n_pairs, past_pair, 0)
    accumulate(0, jnp.where(n_pairs == 0, jd, jnp.minimum(2 * n_pairs - 1, jd)))
    o_slc = result()

    gt = gt_ref[...]
    outs = []
    for h in range(HPG):
        r0 = h * tq
        outs.append(gt[:, 3 * h:3 * h + 1] * o_cmp[r0:r0 + tq] + gt[:, 3 * h + 1:3 * h + 2] * o_slc[r0:r0 + tq]
                    + gt[:, 3 * h + 2:3 * h + 3] * o_win[r0:r0 + tq])
    o_ref[...] = jnp.concatenate(outs, axis=1).astype(BF16)


def _nsa_prompt(q, kcv, kvb, gt, ov, *, tq, tk):
    b, t, _ = q.shape
    nc = kcv.shape[2]
    assert tk % tq == 0 and t % tk == 0 and tk <= WINDOW
    return pl.pallas_call(
        functools.partial(_nsa_prompt_kernel, tq=tq, tk=tk),
        grid=(b, NSA_GROUPS, t // tq),
        in_specs=[pl.BlockSpec((None, tq, GRP_Q), lambda bi, g, i: (bi, i, g)),
                  pl.BlockSpec((None, None, nc, 2 * HEAD_DIM), lambda bi, g, i: (bi, g, 0, 0)),
                  pl.BlockSpec((None, t, 4 * HEAD_DIM), lambda bi, g, i: (bi, 0, g)),
                  pl.BlockSpec((None, tq, LANES), lambda bi, g, i: (bi, i, g)),
                  _const_spec(ov.shape)],
        out_specs=pl.BlockSpec((None, tq, GRP_Q), lambda bi, g, i: (bi, i, g)),
        out_shape=jax.ShapeDtypeStruct((b, t, NSA_WIDTH), BF16),
        scratch_shapes=([pltpu.VMEM((HPG * tq, tk), F32)] * 2 + [pltpu.VMEM((HPG * tq, tk), BF16)] * 2
                        + [pltpu.VMEM((HPG * tq, LANES), F32)] * 4) * 2,
        compiler_params=_cparams(("arbitrary", "arbitrary", "arbitrary")),
        name="nsa_prompt",
    )(q, kcv, kvb, gt, ov)


def _nsa_dec_select_kernel(q_ref, kcv_ref, ov_ref, ocmp_ref, idx_ref, *, qpos):
    nc = kcv_ref.shape[1]
    q = q_ref[...]
    cmp_end = lax.broadcasted_iota(jnp.int32, (1, nc), 1) * CMP_STRIDE + (CMP_BLK - 1)
    mask = cmp_end <= qpos
    ov = ov_ref[...]
    imps = []
    for g in range(NSA_GROUPS):
        kcv = kcv_ref[g]
        p = _masked_softmax(_dot_nt(q[g * HPG:(g + 1) * HPG], kcv[:, :HEAD_DIM]), mask)
        ocmp_ref[g * HPG:(g + 1) * HPG, :] = _dot(p, kcv[:, HEAD_DIM:])
        p_hi, p_lo = _split2(jnp.sum(p, axis=0, keepdims=True))
        imps.append(jnp.dot(p_hi, ov, preferred_element_type=F32) + jnp.dot(p_lo, ov, preferred_element_type=F32))
    imp = jnp.concatenate(imps + [jnp.zeros((8 - NSA_GROUPS, ov.shape[1]), F32)], axis=0)
    score = _block_scores(imp, jnp.full((8, 1), qpos, jnp.int32), 1)
    _, idx = _top_k_mask(score, N_SEL, 1, idx_out=True)
    idx_ref[...] = idx


def _nsa_dec_select(q3, kcv, ov, qpos):
    db, nh, _ = q3.shape
    nc = kcv.shape[2]
    return pl.pallas_call(
        functools.partial(_nsa_dec_select_kernel, qpos=qpos),
        grid=(db,),
        in_specs=[pl.BlockSpec((None, nh, HEAD_DIM), lambda b: (b, 0, 0)),
                  pl.BlockSpec((None, NSA_GROUPS, nc, 2 * HEAD_DIM), lambda b: (b, 0, 0, 0)),
                  _const_spec(ov.shape)],
        out_specs=[pl.BlockSpec((None, nh, HEAD_DIM), lambda b: (b, 0, 0)),
                   pl.BlockSpec((None, 8, LANES), lambda b: (b, 0, 0))],
        out_shape=[jax.ShapeDtypeStruct((db, nh, HEAD_DIM), F32), jax.ShapeDtypeStruct((db, 8, LANES), jnp.int32)],
        compiler_params=_cparams(("arbitrary",)),
        name="nsa_dec_select",
    )(q3, kcv, ov)


def _nsa_dec_attend_kernel(idx_ref, pt_ref, *refs, past):
    k_refs = refs[:N_SEL]
    v_refs = refs[N_SEL:2 * N_SEL]
    (q_ref, nk_ref, nv_ref, nwk_ref, nwv_ref, wk_ref, wv_ref, ocmp_ref, gt_ref, o_ref, wko_ref, wvo_ref) = refs[2 * N_SEL:]
    b = pl.program_id(0)
    g = pl.program_id(1)
    win = wk_ref.shape[1]
    q = q_ref[...]
    lane_blk = lax.broadcasted_iota(jnp.int32, (1, SLC_BLK), 1)

    s_parts, v_parts, pos_parts = [], [], []
    for i in range(N_SEL):
        blk = idx_ref[b, g * N_SEL + i]
        first_half = blk % (PAGE_SIZE // SLC_BLK) == 0
        kpos = blk * SLC_BLK + lane_blk
        from_new = kpos >= past
        k_t, v_t = k_refs[i][...], v_refs[i][...]
        k_t = jnp.where(from_new, nk_ref[...], jnp.where(first_half, k_t[:, :SLC_BLK], k_t[:, SLC_BLK:]))
        v_t = jnp.where(from_new, nv_ref[...], jnp.where(first_half, v_t[:, :SLC_BLK], v_t[:, SLC_BLK:]))
        s_parts.append(_dot(q, k_t))
        v_parts.append(v_t)
        pos_parts.append(kpos)
    p = _masked_softmax(jnp.concatenate(s_parts, axis=1), jnp.concatenate(pos_parts, axis=1) <= past)
    o_slc = _dot_nt(p, jnp.concatenate(v_parts, axis=1))

    lane = lax.broadcasted_iota(jnp.int32, (1, win), 1)
    wk_new = jnp.where(lane == win - 1, nwk_ref[...], pltpu.roll(wk_ref[...], win - 1, 1))
    wv_new = jnp.where(lane == win - 1, nwv_ref[...], pltpu.roll(wv_ref[...], win - 1, 1))
    wko_ref[...] = wk_new
    wvo_ref[...] = wv_new
    p = _masked_softmax(_dot(q, wk_new), (win - 1 - lane) < WINDOW)
    o_win = _dot_nt(p, wv_new)

    gt = gt_ref[...]
    o_ref[...] = gt[:, 0:1] * ocmp_ref[...] + gt[:, 1:2] * o_slc + gt[:, 2:3] * o_win


def _nsa_dec_attend(idx, page_table, pool_k, pool_v, q4, new_cols, win_k, win_v, o_cmp4, gt4, past):
    db, n_pages = page_table.shape
    win = win_k.shape[3]
    blocks_per_page = PAGE_SIZE // SLC_BLK
    assert blocks_per_page == 2

    def blk_map(i):
        def index(b, g, idx_s, pt_s):
            page = jnp.minimum(idx_s[b, g * N_SEL + i] // blocks_per_page, n_pages - 1)
            return pt_s[b, page], g, 0, 0
        return index

    sel_specs = [pl.BlockSpec((None, None, HEAD_DIM, PAGE_SIZE), blk_map(i)) for i in range(N_SEL)]

    def per_bg(shape):
        return pl.BlockSpec((None, None) + shape, lambda b, g, *_: (b, g, 0, 0))

    grid_spec = pltpu.PrefetchScalarGridSpec(
        num_scalar_prefetch=2,
        grid=(db, NSA_GROUPS),
        in_specs=sel_specs + sel_specs + [per_bg((HPG, HEAD_DIM))] + [per_bg((HEAD_DIM, 1))] * 4
        + [per_bg((HEAD_DIM, win))] * 2 + [per_bg((HPG, HEAD_DIM)), per_bg((HPG, 3))],
        out_specs=[per_bg((HPG, HEAD_DIM)), per_bg((HEAD_DIM, win)), per_bg((HEAD_DIM, win))],
    )
    return pl.pallas_call(
        functools.partial(_nsa_dec_attend_kernel, past=past),
        grid_spec=grid_spec,
        out_shape=[jax.ShapeDtypeStruct((db, NSA_GROUPS, HPG, HEAD_DIM), F32),
                   jax.ShapeDtypeStruct(win_k.shape, F32), jax.ShapeDtypeStruct(win_v.shape, F32)],
        compiler_params=_cparams(("arbitrary", "arbitrary")),
        name="nsa_dec_attend",
    )(idx, page_table, *([pool_k] * N_SEL), *([pool_v] * N_SEL), q4, *new_cols, win_k, win_v, o_cmp4, gt4)


def _forget_lower_bound(lb_raw):
    m = jnp.max(lb_raw, axis=0, keepdims=True)
    e = jnp.exp(lb_raw - m)
    return e[0:1] / jnp.sum(e, axis=0, keepdims=True)


def _hgrn_out(o, hg, norm_g):
    o = o * lax.rsqrt(jnp.mean(o * o, axis=-1, keepdims=True) + EPS) * norm_g
    return o * (hg * _sigmoid(hg))


def _hgrn_chunk_common(q, hf, v, lb, tri, st, b_s, k_s, v_s):
    heads = range(len(q))
    lc = q[0].shape[0]
    f = [lb[h] + (1.0 - lb[h]) * _sigmoid(hf[h]) for h in heads]
    k = [1.0 - f[h] for h in heads]
    parts = [p for h in heads for p in _split3(jnp.log(f[h]))]
    csum = jnp.dot(tri, jnp.concatenate(parts, axis=1), preferred_element_type=F32)
    bcum = [csum[:, 3 * h * HG_DIM:(3 * h + 1) * HG_DIM] + csum[:, (3 * h + 1) * HG_DIM:(3 * h + 2) * HG_DIM]
            + csum[:, (3 * h + 2) * HG_DIM:(3 * h + 3) * HG_DIM] for h in heads]
    for h in heads:
        b_s[h][...] = bcum[h]
        k_s[h][...] = k[h]
        v_s[h][...] = v[h]

    o_inter = [_dot_nt(q[h] * jnp.exp(bcum[h]), st[h]) for h in heads]
    span = None
    for i in range(lc // HG_SUB):
        r0 = i * HG_SUB
        for h in heads:
            d_i = bcum[h][r0:r0 + 1] - bcum[h][r0 + HG_SUB - 1:r0 + HG_SUB]
            span = d_i if span is None else jnp.maximum(span, d_i)
    st_new = [st[h] * jnp.exp(bcum[h][lc - 1:lc]) + _dot(v[h].T, k[h] * jnp.exp(bcum[h][lc - 1:lc] - bcum[h]))
              for h in heads]
    return o_inter, k, bcum, st_new, span


def _hgrn_intra_matmul(o_inter, q, k, bcum, v, o_s):
    heads = range(len(q))
    lc = q[0].shape[0]
    n_sub = lc // HG_SUB
    row_sub = lax.broadcasted_iota(jnp.int32, (lc, HG_DIM), 0) >> HG_SUB_LOG2
    causal = lax.broadcasted_iota(jnp.int32, (lc, lc), 1) <= lax.broadcasted_iota(jnp.int32, (lc, lc), 0)
    for h in heads:
        ref = [bcum[h][j * HG_SUB:j * HG_SUB + 1] for j in range(n_sub)]
        ref_rows = jnp.concatenate([jnp.broadcast_to(r, (HG_SUB, HG_DIM)) for r in ref], axis=0)
        q_t = q[h] * jnp.exp(bcum[h] - ref_rows)
        k_t = k[h] * jnp.exp(ref_rows - bcum[h])
        q_cols = [jnp.where(row_sub >= j, q_t * jnp.exp(jnp.minimum(ref_rows - ref[j], 0.0)), 0.0).astype(BF16)
                  for j in range(n_sub)]
        k_cols = [jnp.where(row_sub == j, k_t, 0.0).astype(BF16) for j in range(n_sub)]
        a = _dot_nt(jnp.concatenate(q_cols, axis=1), jnp.concatenate(k_cols, axis=1))
        o_s[h][...] = o_inter[h] + _dot(jnp.where(causal, a, 0.0), v[h])


def _hgrn_intra_direct(o_inter, q, k, bcum, v, b_s, k_s, v_s, o_s):
    lc = q.shape[0]
    t_idx = lax.broadcasted_iota(jnp.int32, (SUBLANES, 1), 0)
    for i in range(lc // HG_SUB):
        r0 = i * HG_SUB
        o_i = o_inter[r0:r0 + HG_SUB]
        if i > 0:
            ref_b = bcum[r0:r0 + 1]
            a = _dot_nt(q[r0:r0 + HG_SUB] * jnp.exp(bcum[r0:r0 + HG_SUB] - ref_b), k[:r0] * jnp.exp(ref_b - bcum[:r0]))
            o_i = o_i + _dot(a, v[:r0])
        for t0 in range(r0, r0 + HG_SUB, SUBLANES):
            b_t, q_t = bcum[t0:t0 + SUBLANES], q[t0:t0 + SUBLANES]
            acc = o_i[t0 - r0:t0 - r0 + SUBLANES]
            for s in range(r0, t0 + SUBLANES):
                d = b_t - b_s[s:s + 1, :]
                if s >= t0:
                    d = jnp.where(t_idx >= s - t0, d, NEG)
                w = jnp.exp(d) * q_t * k_s[s:s + 1, :]
                acc = acc + jnp.sum(w, axis=-1, keepdims=True) * v_s[s:s + 1, :]
            o_s[t0:t0 + SUBLANES, :] = acc


def _hgrn_prompt_kernel(hz_ref, lb_ref, ng_ref, tri_ref, o_ref, st_ref, state_t, b_s, k_s, v_s, o_s):
    c = pl.program_id(1)
    n_c = pl.num_programs(1)

    @pl.when(c == 0)
    def _():
        state_t[...] = jnp.zeros_like(state_t)

    lb_all = _forget_lower_bound(lb_ref[...])
    tri = tri_ref[...]

    def seg(j, h):
        return hz_ref[:, j * HG_WIDTH + h * HG_DIM:j * HG_WIDTH + (h + 1) * HG_DIM]

    heads = range(HG_HEADS)
    q, v = [seg(0, h) for h in heads], [seg(2, h) for h in heads]
    scratch = [[s.at[h] for h in heads] for s in (b_s, k_s, v_s, o_s)]
    o_inter, k, bcum, st_new, span = _hgrn_chunk_common(
        q, [seg(1, h) for h in heads], v, [lb_all[:, h * HG_DIM:(h + 1) * HG_DIM] for h in heads], tri,
        [state_t[h] for h in heads], *scratch[:3])
    for h in heads:
        state_t[h] = st_new[h]
    direct = jnp.max(span) > HG_SAFE_DECAY

    @pl.when(jnp.logical_not(direct))
    def _():
        _hgrn_intra_matmul(o_inter, q, k, bcum, v, scratch[3])

    @pl.when(direct)
    def _():
        for h in heads:
            _hgrn_intra_direct(o_inter[h], q[h], k[h], bcum[h], v[h], *(s[h] for s in scratch))

    for h in range(HG_HEADS):
        o_ref[:, h * HG_DIM:(h + 1) * HG_DIM] = _hgrn_out(o_s[h], seg(3, h), ng_ref[...]).astype(BF16)

    @pl.when(c == n_c - 1)
    def _():
        for h in range(HG_HEADS):
            st_ref[h] = state_t[h].T


def _hgrn_prompt(hz3, hg_lb, norm_g, *, lc):
    b, t, w = hz3.shape
    tri = jnp.tril(jnp.ones((lc, lc), F32)).astype(BF16)
    return pl.pallas_call(
        _hgrn_prompt_kernel,
        grid=(b, t // lc),
        in_specs=[pl.BlockSpec((None, lc, w), lambda bi, c: (bi, c, 0)), _const_spec(hg_lb.shape),
                  _const_spec((1, HG_DIM)), _const_spec((lc, lc))],
        out_specs=[pl.BlockSpec((None, lc, HG_WIDTH), lambda bi, c: (bi, c, 0)),
                   pl.BlockSpec((None, HG_HEADS, HG_DIM, HG_DIM), lambda bi, c: (bi, 0, 0, 0))],
        out_shape=[jax.ShapeDtypeStruct((b, t, HG_WIDTH), BF16),
                   jax.ShapeDtypeStruct((b, HG_HEADS, HG_DIM, HG_DIM), F32)],
        scratch_shapes=[pltpu.VMEM((HG_HEADS, HG_DIM, HG_DIM), F32)] + [pltpu.VMEM((HG_HEADS, lc, HG_DIM), F32)] * 4,
        compiler_params=_cparams(("arbitrary", "arbitrary")),
        name="hgrn_prompt",
    )(hz3, hg_lb, norm_g, tri)


def _hgrn_dec_kernel(hz_ref, s0_ref, lb_ref, ng_ref, o_ref, s_ref):
    hz = hz_ref[...]
    lb_all = _forget_lower_bound(lb_ref[...])
    eye = lax.broadcasted_iota(jnp.int32, (HG_DIM, HG_DIM), 0) == lax.broadcasted_iota(jnp.int32, (HG_DIM, HG_DIM), 1)

    def column(row):
        return jnp.sum(jnp.where(eye, row, 0.0), axis=1, keepdims=True)

    outs = []
    for h in range(HG_HEADS):
        def seg(j):
            return hz[:, j * HG_WIDTH + h * HG_DIM:j * HG_WIDTH + (h + 1) * HG_DIM]

        lb = lb_all[:, h * HG_DIM:(h + 1) * HG_DIM]
        f = lb + (1.0 - lb) * _sigmoid(seg(1))
        q, k, v = seg(0), 1.0 - f, seg(2)
        s0 = s0_ref[h]
        s_ref[h] = column(f) * s0 + column(k) * v
        o = jnp.sum(column(q * f) * s0, axis=0, keepdims=True) + jnp.sum(q * k, axis=-1, keepdims=True) * v
        outs.append(_hgrn_out(o, seg(3), ng_ref[...]))
    o_ref[...] = jnp.concatenate(outs, axis=1)


def _hgrn_dec(hz, s0, hg_lb, norm_g):
    db = hz.shape[0]
    return pl.pallas_call(
        _hgrn_dec_kernel,
        grid=(db,),
        in_specs=[pl.BlockSpec((None, 1, hz.shape[1]), lambda b: (b, 0, 0)),
                  pl.BlockSpec((None, HG_HEADS, HG_DIM, HG_DIM), lambda b: (b, 0, 0, 0)),
                  _const_spec(hg_lb.shape), _const_spec((1, HG_DIM))],
        out_specs=[pl.BlockSpec((None, 1, HG_WIDTH), lambda b: (b, 0, 0)),
                   pl.BlockSpec((None, HG_HEADS, HG_DIM, HG_DIM), lambda b: (b, 0, 0, 0))],
        out_shape=[jax.ShapeDtypeStruct((db, 1, HG_WIDTH), F32), jax.ShapeDtypeStruct(s0.shape, F32)],
        compiler_params=_cparams(("arbitrary",)),
        name="hgrn_dec",
    )(hz.reshape(db, 1, -1), s0, hg_lb, norm_g)


def _merge_core(on_ref, oh_ref, sm_ref, x_ref, g1_ref, sh_ref, sc_ref, ng_ref, wn_ref, wh_ref, wo_ref):
    d = x_ref.shape[1]
    sm = sm_ref[...]
    mix = sm[:, :d] * _dot(on_ref[...], wn_ref[...]) + sm[:, d:] * _dot(oh_ref[...], wh_ref[...])
    x1 = x_ref[...] + g1_ref[...] * _dot(mix, wo_ref[...])
    xn = x1 * lax.rsqrt(jnp.mean(x1 * x1, axis=-1, keepdims=True) + EPS) * ng_ref[...]
    return x1, (xn * (1.0 + sc_ref[...]) + sh_ref[...]).astype(BF16)


def _merge_kernel(*refs):
    x1_ref, h2_ref = refs[-2:]
    x1_ref[...], h2_ref[...] = _merge_core(*refs[:-2])


def _merge(o_nsa, o_hg, sm, x2d, g1, sh2, sc2, norm_g, w_br_nsa, w_br_hg, w_out, *, tm, tiles_per_mod):
    r, d = x2d.shape
    rows_mod = g1.shape[1]
    mod_spec = pl.BlockSpec((None, rows_mod, d), lambda i: (i // tiles_per_mod, 0, 0))

    def row_spec(width):
        return pl.BlockSpec((tm, width), lambda i: (i, 0))

    return pl.pallas_call(
        _merge_kernel,
        grid=(r // tm,),
        in_specs=[row_spec(o_nsa.shape[1]), row_spec(o_hg.shape[1]), row_spec(sm.shape[1]), row_spec(d),
                  mod_spec, mod_spec, mod_spec, _const_spec((1, d)),
                  _const_spec(w_br_nsa.shape), _const_spec(w_br_hg.shape), _const_spec(w_out.shape)],
        out_specs=[row_spec(d), row_spec(d)],
        out_shape=[jax.ShapeDtypeStruct((r, d), F32), jax.ShapeDtypeStruct((r, d), BF16)],
        compiler_params=_cparams(("arbitrary",)),
        name="merge",
    )(o_nsa, o_hg, sm, x2d, g1, sh2, sc2, norm_g, w_br_nsa, w_br_hg, w_out)


FF_CHUNKS = 2


def _ffn_core(h2, x1, g2, wu_ref, cw_ref, cb_ref, wd_ref, prev_rows):
    d_ff = wd_ref.shape[0]
    ch = d_ff // FF_CHUNKS
    acc = None
    us = []
    for ci in range(FF_CHUNKS):
        c0, c1 = ci * ch, (ci + 1) * ch
        u = jnp.dot(h2, wu_ref[:, c0:c1], preferred_element_type=F32)
        v = jnp.dot(h2, wu_ref[:, d_ff + c0:d_ff + c1], preferred_element_type=F32)
        u_m1, u_m2 = prev_rows(u, c0, c1)
        y = cb_ref[:, c0:c1] + cw_ref[0:1, c0:c1] * u_m2 + cw_ref[1:2, c0:c1] * u_m1 + cw_ref[2:3, c0:c1] * u
        part = _dot(y * _sigmoid(y) * v, wd_ref[c0:c1, :])
        acc = part if acc is None else acc + part
        us.append(u)
    return x1 + g2 * acc, us


def _post_prompt_kernel(*refs):
    merge_refs = refs[:11]
    g2_ref, wu_ref, cw_ref, cb_ref, wd_ref, y_ref, tail_ref, carry = refs[11:]
    ti = pl.program_id(1)
    tm = y_ref.shape[0]

    @pl.when(ti == 0)
    def _():
        carry[...] = jnp.zeros_like(carry)

    row = lax.broadcasted_iota(jnp.int32, (tm, 1), 0)

    def prev_rows(u, c0, c1):
        last = carry[:, c0:c1]
        m1 = jnp.where(row == 0, last[7:8], pltpu.roll(u, 1, 0))
        m2 = jnp.where(row == 0, last[6:7], jnp.where(row == 1, last[7:8], pltpu.roll(u, 2, 0)))
        return m1, m2

    x1, h2 = _merge_core(*merge_refs)
    out, us = _ffn_core(h2, x1, g2_ref[...], wu_ref, cw_ref, cb_ref, wd_ref, prev_rows)
    y_ref[...] = out
    ch = us[0].shape[1]
    for ci, u in enumerate(us):
        carry[:, ci * ch:(ci + 1) * ch] = u[tm - 8:, :]
    tail_ref[...] = carry[...]


def _post_prompt(o_nsa, o_hg, sm, x2d, mods, norm_g, w_br_nsa, w_br_hg, w_out, w_up, conv_w8, conv_b, w_down, *, b, tm):
    r, d = x2d.shape
    d_ff = w_down.shape[0]
    tpb = r // b // tm
    g1, sh2, sc2, g2 = mods

    def row_spec(width):
        return pl.BlockSpec((tm, width), lambda bi, i: (bi * tpb + i, 0))

    mod_spec = pl.BlockSpec((None, 1, d), lambda bi, i: (bi, 0, 0))
    return pl.pallas_call(
        _post_prompt_kernel,
        grid=(b, tpb),
        in_specs=[row_spec(o_nsa.shape[1]), row_spec(o_hg.shape[1]), row_spec(sm.shape[1]), row_spec(d),
                  mod_spec, mod_spec, mod_spec, _const_spec((1, d)),
                  _resident_spec(w_br_nsa.shape), _resident_spec(w_br_hg.shape), _resident_spec(w_out.shape),
                  mod_spec, _resident_spec(w_up.shape), _const_spec(conv_w8.shape), _const_spec(conv_b.shape),
                  _resident_spec(w_down.shape)],
        out_specs=[row_spec(d), pl.BlockSpec((None, 8, d_ff), lambda bi, i: (bi, 0, 0))],
        out_shape=[jax.ShapeDtypeStruct((r, d), F32), jax.ShapeDtypeStruct((b, 8, d_ff), F32)],
        scratch_shapes=[pltpu.VMEM((8, d_ff), F32)],
        compiler_params=_cparams(("arbitrary", "arbitrary")),
        name="post_prompt",
    )(o_nsa, o_hg, sm, x2d, g1, sh2, sc2, norm_g, w_br_nsa, w_br_hg, w_out, g2, w_up, conv_w8, conv_b, w_down)


def _ffn_dec_kernel(h2_ref, x1_ref, g2_ref, wu_ref, cw_ref, cb_ref, wd_ref, m1_ref, m2_ref, y_ref, u_ref):
    def prev_rows(u, c0, c1):
        return m1_ref[:, c0:c1], m2_ref[:, c0:c1]

    out, us = _ffn_core(h2_ref[...], x1_ref[...], g2_ref[...], wu_ref, cw_ref, cb_ref, wd_ref, prev_rows)
    y_ref[...] = out
    u_ref[...] = jnp.concatenate(us, axis=1)


def _ffn_dec(h2, x1, g2, w_up, conv_w8, conv_b, w_down, u_m1, u_m2):
    r, d = x1.shape
    d_ff = w_down.shape[0]
    args = (h2, x1, g2, w_up, conv_w8, conv_b, w_down, u_m1, u_m2)
    return pl.pallas_call(
        _ffn_dec_kernel,
        grid=(1,),
        in_specs=[_const_spec(a.shape) for a in args],
        out_specs=[_const_spec((r, d)), _const_spec((r, d_ff))],
        out_shape=[jax.ShapeDtypeStruct((r, d), F32), jax.ShapeDtypeStruct((r, d_ff), F32)],
        compiler_params=_cparams(("arbitrary",)),
        name="ffn_dec",
    )(*args)


def _pack_w_in(w_in):
    cuts = np.cumsum([NSA_WIDTH] + [KV_WIDTH] * 6 + [3 * NSA_HEADS] + [HG_WIDTH] * 4)
    q, kv, gates, rest = w_in[:, :cuts[0]], w_in[:, cuts[0]:cuts[6]], w_in[:, cuts[6]:cuts[7]], w_in[:, cuts[7]:]
    hz, m = rest[:, :4 * HG_WIDTH], rest[:, 4 * HG_WIDTH:]
    per_group = 3 * HPG
    pad = jnp.zeros((w_in.shape[0], LANES - per_group), w_in.dtype)
    gate_cols = []
    for g in range(NSA_GROUPS):
        gate_cols += [gates[:, g * per_group:(g + 1) * per_group], pad]
    return jnp.concatenate([q, kv] + gate_cols + [hz, m], axis=1).astype(BF16)


def _rope_tables(pos):
    pos = np.asarray(pos, np.float64)
    half = ROT_DIM // 2
    inv = ROPE_THETA ** (-np.arange(0, ROT_DIM, 2, dtype=np.float64) / ROT_DIM)
    ang = pos[:, None] * inv[None, :]
    cos, sin = np.cos(ang), np.sin(ang)
    n = pos.shape[0]
    ones = np.ones((n, HEAD_DIM - ROT_DIM))
    zeros = np.zeros((n, HEAD_DIM - ROT_DIM))
    z8 = np.zeros((n, half))
    c = np.concatenate([cos, cos, ones], axis=1)
    s_lo = np.concatenate([-sin, z8, zeros], axis=1)
    s_hi = np.concatenate([z8, sin, zeros], axis=1)
    return tuple(jnp.asarray(np.concatenate([t, t], axis=1), F32) for t in (c, s_lo, s_hi))


def _cmp_weights(cmp_w1, cmp_w2, cmp_pe):
    n_half = CMP_BLK // CMP_STRIDE
    eye = jnp.eye(NSA_GROUPS, dtype=F32)
    w1s, w2s, pes = [], [], []
    for j in range(2):
        w = cmp_w1[j].reshape(n_half, CMP_STRIDE, HEAD_DIM, HEAD_DIM)
        big = jnp.einsum('rsdh,ge->sgdreh', w, eye)
        w1s.append(big.reshape(CMP_STRIDE * KV_WIDTH, n_half * KV_WIDTH))
        w2s.append(jnp.einsum('dh,ge->gdeh', cmp_w2[j], eye).reshape(KV_WIDTH, KV_WIDTH))
        pe = cmp_pe[j].reshape(n_half, CMP_STRIDE, 1, HEAD_DIM)
        pe = jnp.broadcast_to(pe, (n_half, CMP_STRIDE, NSA_GROUPS, HEAD_DIM)).reshape(n_half, -1)
        pes.append(jnp.concatenate([pe, jnp.zeros((8 - n_half, pe.shape[1]), F32)], axis=0))
    return jnp.stack(w1s).astype(BF16), jnp.stack(w2s).astype(BF16), jnp.stack(pes).astype(BF16)


def _overlap(nc_pad, nc, ns, ns_pad):
    start = np.arange(nc_pad) * CMP_STRIDE
    end = start + CMP_BLK - 1
    s0 = np.arange(ns_pad) * SLC_BLK
    s1 = s0 + SLC_BLK - 1
    m = (start[:, None] <= s1[None, :]) & (end[:, None] >= s0[None, :])
    m &= (np.arange(nc_pad) < nc)[:, None] & (np.arange(ns_pad) < ns)[None, :]
    return jnp.asarray(m, BF16)


def kernel(x_prompt, x_sample, cache_cmp_k, cache_cmp_v, cache_slc_k, cache_slc_v, cache_win_k, cache_win_v,
           state_hgrn, state_ffn_conv, page_table, c_prompt, c_sample, ada_w, ada_b, norm_attn, norm_ffn, w_in,
           q_gain, k_gain, cmp_w1, cmp_w2, cmp_pe, hg_lb, hg_norm, w_br_nsa, w_br_hg, w_out, w_up, conv_w,
           conv_b, w_down):
    depth = ada_w.shape[0]
    assert depth == 1, "single-layer trunk only"
    b, t, d = x_prompt.shape
    db, tn, _ = x_sample.shape
    assert tn == 1
    n_pages = page_table.shape[1]
    past = n_pages * PAGE_SIZE
    d_ff = w_down.shape[1]
    tq = 256
    tk = 256
    tm = 256
    lc = 128
    assert t >= WINDOW and n_pages % PAGES_PER_STEP == 0
    assert cache_win_k.shape[2] == WINDOW

    w_pack = _pack_w_in(w_in[0])
    qg = jnp.tile(q_gain[0], NSA_HEADS).reshape(1, NSA_WIDTH)
    kg = jnp.concatenate([jnp.tile(k_gain[0], (1, NSA_GROUPS)), jnp.zeros((5, KV_WIDTH), F32)], axis=0)
    hd = np.arange(NSA_WIDTH) // HEAD_DIM
    bd = jnp.asarray(hd[:, None] == hd[None, :], BF16)
    w1c, w2c, pe_x = _cmp_weights(cmp_w1[0], cmp_w2[0], cmp_pe[0])
    conv_w8 = jnp.concatenate([conv_w[0], jnp.zeros((8 - CONV_W, d_ff), F32)], axis=0)
    conv_b2 = conv_b[0].reshape(1, d_ff)
    wbn, wbh, wo = w_br_nsa[0].astype(BF16), w_br_hg[0].astype(BF16), w_out[0].astype(BF16)
    wu, wd = w_up[0].astype(BF16), w_down[0].astype(BF16)
    norm_a, norm_f, hg_n = norm_attn[0].reshape(1, d), norm_ffn[0].reshape(1, d), hg_norm[0].reshape(1, HG_DIM)

    n_c = b + db
    c_all = jnp.concatenate([c_prompt, c_sample, jnp.zeros((-n_c % 8, d), F32)], axis=0)
    ada = _ada(c_all, ada_w[0], ada_b[0])
    mods_p = [ada[:b, i * d:(i + 1) * d].reshape(b, 1, d) for i in range(6)]
    mods_s = [ada[b:n_c, i * d:(i + 1) * d].reshape(1, db, d) for i in range(6)]

    tpb = t // tm
    (q, kc_r, vc_r, ks_r, vs_r, kw_r, vw_r, kcb, vcb, kvb, gt, hz, sm) = _inproj(
        x_prompt.reshape(b * t, d), mods_p[0], mods_p[1], norm_a, w_pack, qg, kg, _rope_tables(np.arange(t)), bd,
        tm=tm, tiles_per_mod=tpb, rope_tiles=tpb)

    nsub = t // CMP_STRIDE
    pk = _mm(kcb, w1c[0], 512, "cmp_proj_k").reshape(b, nsub, -1)
    pv = _mm(vcb, w1c[1], 512, "cmp_proj_v").reshape(b, nsub, -1)
    kcv = _cmp_combine(pk, pv, pe_x, w1c, w2c)
    ov = _overlap(nsub, nsub - 1, t // SLC_BLK, LANES).T
    o_nsa = _nsa_prompt(q.reshape(b, t, -1), kcv, kvb.reshape(b, t, -1), gt.reshape(b, t, -1), ov, tq=tq, tk=tk)

    o_hg, st_p = _hgrn_prompt(hz.reshape(b, t, -1), hg_lb, hg_n, lc=lc)

    y_p, tail = _post_prompt(o_nsa.reshape(b * t, -1), o_hg.reshape(b * t, -1), sm, x_prompt.reshape(b * t, d),
                             mods_p[2:], norm_f, wbn, wbh, wo, wu, conv_w8, conv_b2, wd, b=b, tm=tm)

    def rows5(a):
        return a.reshape(1, b, t, NSA_GROUPS, HEAD_DIM)

    w = min(WINDOW, t)
    out_p = (rows5(kc_r), rows5(vc_r), rows5(ks_r), rows5(vs_r), rows5(kw_r)[:, :, -w:], rows5(vw_r)[:, :, -w:],
             st_p[None], tail[None, :, 8 - (CONV_W - 1):, :])

    pos_s = np.full((db,), past)
    (q_s, kc_s, vc_s, ks_s, vs_s, kw_s, vw_s, _, _, _, gt_s, hz_s, sm_s) = _inproj(
        x_sample.reshape(db, d), mods_s[0], mods_s[1], norm_a, w_pack, qg, kg, _rope_tables(pos_s), bd,
        tm=db, tiles_per_mod=1, rope_tiles=1)

    def feature_major(a):
        return jnp.transpose(a, (0, 2, 3, 1))

    assert (past + tn) // CMP_STRIDE == past // CMP_STRIDE
    pk_s, pv_s = _cmp_proj_paged(feature_major(cache_cmp_k[0]).reshape(-1, KV_WIDTH, PAGE_SIZE),
                                 feature_major(cache_cmp_v[0]).reshape(-1, KV_WIDTH, PAGE_SIZE), page_table, w1c)
    kcv_s = _cmp_combine(pk_s, pv_s, pe_x, w1c, w2c)
    nsub_s = past // CMP_STRIDE
    ns_s = -(-(past + tn) // SLC_BLK)
    ns_pad = -(-ns_s // LANES) * LANES
    ov_s = _overlap(nsub_s, nsub_s - 1, ns_s, ns_pad)
    o_cmp_s, idx8 = _nsa_dec_select(q_s.reshape(db, NSA_HEADS, HEAD_DIM), kcv_s, ov_s, past)
    idx = idx8[:, :NSA_GROUPS, :N_SEL].reshape(db, NSA_GROUPS * N_SEL)
    new_cols = [a.reshape(db, NSA_GROUPS, HEAD_DIM, 1) for a in (ks_s, vs_s, kw_s, vw_s)]
    per_group = 3 * HPG
    gt4 = jnp.stack([gt_s[:, g * LANES:g * LANES + per_group] for g in range(NSA_GROUPS)], axis=1)
    o_nsa_s, wk_new, wv_new = _nsa_dec_attend(
        idx, page_table, feature_major(cache_slc_k[0]), feature_major(cache_slc_v[0]),
        q_s.astype(F32).reshape(db, NSA_GROUPS, HPG, HEAD_DIM), new_cols,
        feature_major(cache_win_k[0]), feature_major(cache_win_v[0]),
        o_cmp_s.reshape(db, NSA_GROUPS, HPG, HEAD_DIM), gt4.reshape(db, NSA_GROUPS, HPG, 3), past)
    wk_new, wv_new = (jnp.transpose(a, (0, 3, 1, 2)) for a in (wk_new, wv_new))

    o_hg_s, st_s = _hgrn_dec(hz_s, state_hgrn[0], hg_lb, hg_n)
    x1_s, h2_s = _merge(o_nsa_s.reshape(db, NSA_WIDTH), o_hg_s.reshape(db, HG_WIDTH), sm_s, x_sample.reshape(db, d),
                        mods_s[2], mods_s[3], mods_s[4], norm_f, wbn, wbh, wo, tm=db, tiles_per_mod=1)
    buf = state_ffn_conv[0]
    y_s, u_s = _ffn_dec(h2_s, x1_s, mods_s[5].reshape(db, d), wu, conv_w8, conv_b2, wd, buf[:, 1], buf[:, 0])
    conv_s = jnp.stack([buf[:, 1], u_s], axis=1)

    def rows5s(a):
        return a.reshape(1, db, tn, NSA_GROUPS, HEAD_DIM)

    out_s = (rows5s(kc_s), rows5s(vc_s), rows5s(ks_s), rows5s(vs_s),
             wk_new.reshape(1, db, WINDOW, NSA_GROUPS, HEAD_DIM), wv_new.reshape(1, db, WINDOW, NSA_GROUPS, HEAD_DIM),
             st_s[None], conv_s[None])
    return (y_p.reshape(b, t, d), y_s.reshape(db, tn, d)) + out_p + out_s
```
